```python
import jax, jax.numpy as jnp
from jax import lax
import numpy as np

D_MODEL = 1024
BATCH = 8
SEQ = 4096
DEPTH = 4

MIX_WIDTH = D_MODEL
HEAD_DIM = 64
NSA_WIDTH = MIX_WIDTH // 2
NSA_HEADS = NSA_WIDTH // HEAD_DIM
NSA_KV_GROUPS = 2
HEADS_PER_GROUP = NSA_HEADS // NSA_KV_GROUPS
KV_WIDTH = NSA_KV_GROUPS * HEAD_DIM
CMP_BLOCK = 32
CMP_STRIDE = 16
CMP_HIDDEN = 128
SEL_BLOCK = 64
N_SEL = 8
WINDOW = 512
Q_BLOCK = 128
POOL_WIDTHS = (2, 4, 8, 16)
POOL_CH = MIX_WIDTH - NSA_WIDTH
POOL_GROUP = POOL_CH // len(POOL_WIDTHS)
N_GATES = 3 * NSA_HEADS
IN_SPLITS = (NSA_WIDTH, KV_WIDTH, KV_WIDTH, KV_WIDTH, KV_WIDTH, KV_WIDTH, KV_WIDTH, N_GATES, POOL_CH)
IN_WIDTH = sum(IN_SPLITS)
D_FF = 2816
EPS = 1e-6

kernel_name = "hymba_style_nsa_pool_macaron"


def rms_norm(x, g):
    xf = x.astype(jnp.float32)
    y = xf * lax.rsqrt(jnp.mean(xf * xf, axis=-1, keepdims=True) + EPS)
    return (y * g.astype(jnp.float32)).astype(x.dtype)


def swiglu(h, wg, wu, wd):
    return (jax.nn.silu(h @ wg) * (h @ wu)) @ wd


def alibi_slopes(n):
    return (2.0 ** (-8.0 * np.arange(1, n + 1) / n)).astype(np.float32)


def masked_softmax(s, mask, axis=-1):
    s = jnp.where(mask, s.astype(jnp.float32), -jnp.inf)
    m = jnp.max(s, axis=axis, keepdims=True)
    m = jnp.where(jnp.isfinite(m), m, 0.0)
    p = jnp.exp(s - m)
    return p / jnp.maximum(jnp.sum(p, axis=axis, keepdims=True), 1e-30)


def compress(kv, pe, w1, w2):
    T = kv.shape[2]
    n_cmp = (T - CMP_BLOCK) // CMP_STRIDE + 1
    idx = np.arange(n_cmp)[:, None] * CMP_STRIDE + np.arange(CMP_BLOCK)[None, :]
    blocks = kv[:, :, idx] + pe
    flat = blocks.reshape(blocks.shape[:3] + (CMP_BLOCK * HEAD_DIM,))
    return jax.nn.gelu(flat @ w1) @ w2


def nsa_mixer(q, k_c, v_c, k_s, v_s, k_w, v_w, gate_logits, pe_k, wk1, wk2, pe_v, wv1, wv2):
    B, T, _ = q.shape
    G, Hg, dh = NSA_KV_GROUPS, HEADS_PER_GROUP, HEAD_DIM
    n_qb = T // Q_BLOCK
    n_cmp = (T - CMP_BLOCK) // CMP_STRIDE + 1
    n_blk = T // SEL_BLOCK
    n_sel = min(N_SEL, n_blk)

    q = q.reshape(B, T, G, Hg, dh).transpose(0, 2, 3, 1, 4) * (dh ** -0.5)
    to_kv = lambda a: a.reshape(B, T, G, dh).transpose(0, 2, 1, 3)
    kc = compress(to_kv(k_c), pe_k, wk1, wk2)
    vc = compress(to_kv(v_c), pe_v, wv1, wv2)
    ks = to_kv(k_s).reshape(B, G, n_blk, SEL_BLOCK, dh)
    vs = to_kv(v_s).reshape(B, G, n_blk, SEL_BLOCK, dh)
    pad = ((0, 0), (0, 0), (WINDOW, 0), (0, 0))
    kw = jnp.pad(to_kv(k_w), pad)
    vw = jnp.pad(to_kv(v_w), pad)
    gates = jax.nn.sigmoid(gate_logits.astype(jnp.float32)).reshape(B, T, G, Hg, 3).transpose(0, 2, 3, 1, 4)

    slopes = jnp.asarray(alibi_slopes(NSA_HEADS)).reshape(G, Hg, 1, 1)
    cmp_pos = jnp.arange(n_cmp) * CMP_STRIDE + (CMP_BLOCK - 1)
    ci = np.arange(n_cmp)[:, None] * CMP_STRIDE
    sj = np.arange(n_blk)[None, :] * SEL_BLOCK
    overlap = jnp.asarray(((ci <= sj + SEL_BLOCK - 1) & (ci + CMP_BLOCK - 1 >= sj)).astype(np.float32))
    blk_ids = jnp.arange(n_blk)
    b_i = jnp.arange(B)[:, None, None, None]
    g_i = jnp.arange(G)[None, :, None, None]

    def one_block(args):
        qb, gb, qi = args
        t = qi * Q_BLOCK + jnp.arange(Q_BLOCK)
        dist_c = t[:, None] - cmp_pos[None, :]
        s_c = jnp.einsum('bghqd,bgcd->bghqc', qb, kc) - slopes * dist_c
        p_c = masked_softmax(s_c, dist_c >= 0)
        o_cmp = jnp.einsum('bghqc,bgcd->bghqd', p_c, vc.astype(jnp.float32))
        imp = jnp.einsum('bghqc,cn->bgqn', p_c, overlap)
        cur = t // SEL_BLOCK
        valid = blk_ids[None, :] <= cur[:, None]
        forced = ((blk_ids[None, :] == 0) | (blk_ids[None, :] == cur[:, None]) |
                  (blk_ids[None, :] == cur[:, None] - 1)) & valid
        imp = jnp.where(forced, jnp.inf, jnp.where(valid, imp, -jnp.inf))
        _, idx = lax.top_k(imp, n_sel)
        k_sel = ks[b_i, g_i, idx]
        v_sel = vs[b_i, g_i, idx]
        pos = idx[..., None] * SEL_BLOCK + jnp.arange(SEL_BLOCK)
        dist_s = t[None, None, :, None, None] - pos
        s_s = jnp.einsum('bghqd,bgqnld->bghqnl', qb, k_sel) - slopes[..., None] * dist_s[:, :, None]
        n_keys = n_sel * SEL_BLOCK
        p_s = masked_softmax(s_s.reshape(B, G, Hg, Q_BLOCK, n_keys),
                             (dist_s >= 0).reshape(B, G, 1, Q_BLOCK, n_keys))
        o_slc = jnp.einsum('bghqk,bgqkd->bghqd', p_s,
                           v_sel.reshape(B, G, Q_BLOCK, n_keys, dh).astype(jnp.float32))
        kwb = lax.dynamic_slice_in_dim(kw, qi * Q_BLOCK, WINDOW + Q_BLOCK, axis=2)
        vwb = lax.dynamic_slice_in_dim(vw, qi * Q_BLOCK, WINDOW + Q_BLOCK, axis=2)
        key_pos = qi * Q_BLOCK - WINDOW + jnp.arange(WINDOW + Q_BLOCK)
        dist_w = t[:, None] - key_pos[None, :]
        mask_w = (dist_w >= 0) & (dist_w < WINDOW) & (key_pos[None, :] >= 0)
        s_w = jnp.einsum('bghqd,bgkd->bghqk', qb, kwb) - slopes * dist_w
        p_w = masked_softmax(s_w, mask_w)
        o_win = jnp.einsum('bghqk,bgkd->bghqd', p_w, vwb.astype(jnp.float32))
        return gb[..., 0:1] * o_cmp + gb[..., 1:2] * o_slc + gb[..., 2:3] * o_win

    qs = q.reshape(B, G, Hg, n_qb, Q_BLOCK, dh).transpose(3, 0, 1, 2, 4, 5)
    gs = gates.reshape(B, G, Hg, n_qb, Q_BLOCK, 3).transpose(3, 0, 1, 2, 4, 5)
    out = lax.map(one_block, (qs, gs, jnp.arange(n_qb)))
    out = out.transpose(1, 0, 4, 2, 3, 5).reshape(B, T, NSA_HEADS * dh)
    return out.astype(q.dtype)


def pool_mixer(u, w_pool, scale):
    B, T, _ = u.shape
    uf = u.astype(jnp.float32)
    c = jnp.pad(jnp.cumsum(uf, axis=1), ((0, 0), (1, 0), (0, 0)))
    t = jnp.arange(T)
    outs = []
    for gi, w in enumerate(POOL_WIDTHS):
        sl = slice(gi * POOL_GROUP, (gi + 1) * POOL_GROUP)
        lo = jnp.maximum(t + 1 - w, 0)
        cnt = jnp.minimum(t + 1, w).astype(jnp.float32)[:, None]
        outs.append((c[:, t + 1, sl] - c[:, lo, sl]) / cnt - uf[..., sl])
    y = jnp.stack(outs, axis=2)
    y = jnp.einsum('btgc,gcd->btgd', y, w_pool.astype(jnp.float32)).reshape(B, T, POOL_CH)
    return (y * scale.astype(jnp.float32)).astype(u.dtype)


def setup_inputs(seed: int = 0) -> dict:
    key = jax.random.key(seed)
    ks = jax.random.split(key, 24)
    n = lambda k, shape, s: jax.random.normal(k, shape, jnp.float32) * s
    gain = lambda k, shape: 1.0 + 0.02 * jax.random.normal(k, shape, jnp.float32)
    L = DEPTH
    return {
        "x": jax.random.normal(ks[0], (BATCH, SEQ, D_MODEL), jnp.float32),
        "ffn1_norm": gain(ks[1], (L, D_MODEL)),
        "ffn1_wg": n(ks[2], (L, D_MODEL, D_FF), D_MODEL ** -0.5),
        "ffn1_wu": n(ks[3], (L, D_MODEL, D_FF), D_MODEL ** -0.5),
        "ffn1_wd": n(ks[4], (L, D_FF, D_MODEL), D_FF ** -0.5),
        "mix_norm": gain(ks[5], (L, D_MODEL)),
        "w_in": n(ks[6], (L, D_MODEL, IN_WIDTH), D_MODEL ** -0.5),
        "cmp_pe_k": n(ks[7], (L, CMP_BLOCK, HEAD_DIM), 0.02),
        "cmp_wk1": n(ks[8], (L, CMP_BLOCK * HEAD_DIM, CMP_HIDDEN), (CMP_BLOCK * HEAD_DIM) ** -0.5),
        "cmp_wk2": n(ks[9], (L, CMP_HIDDEN, HEAD_DIM), CMP_HIDDEN ** -0.5),
        "cmp_pe_v": n(ks[10], (L, CMP_BLOCK, HEAD_DIM), 0.02),
        "cmp_wv1": n(ks[11], (L, CMP_BLOCK * HEAD_DIM, CMP_HIDDEN), (CMP_BLOCK * HEAD_DIM) ** -0.5),
        "cmp_wv2": n(ks[12], (L, CMP_HIDDEN, HEAD_DIM), CMP_HIDDEN ** -0.5),
        "pool_w": n(ks[13], (L, len(POOL_WIDTHS), POOL_GROUP, POOL_GROUP), POOL_GROUP ** -0.5),
        "pool_scale": 1.0 + 0.1 * jax.random.normal(ks[14], (L, POOL_CH), jnp.float32),
        "w_out": n(ks[15], (L, MIX_WIDTH, D_MODEL), MIX_WIDTH ** -0.5),
        "ffn2_norm": gain(ks[16], (L, D_MODEL)),
        "ffn2_wg": n(ks[17], (L, D_MODEL, D_FF), D_MODEL ** -0.5),
        "ffn2_wu": n(ks[18], (L, D_MODEL, D_FF), D_MODEL ** -0.5),
        "ffn2_wd": n(ks[19], (L, D_FF, D_MODEL), D_FF ** -0.5),
        "final_norm": gain(ks[20], (D_MODEL,)),
    }


def reference(x, ffn1_norm, ffn1_wg, ffn1_wu, ffn1_wd, mix_norm, w_in, cmp_pe_k, cmp_wk1, cmp_wk2,
              cmp_pe_v, cmp_wv1, cmp_wv2, pool_w, pool_scale, w_out, ffn2_norm, ffn2_wg, ffn2_wu,
              ffn2_wd, final_norm):
    split_at = list(np.cumsum(IN_SPLITS)[:-1])
    for l in range(DEPTH):
        x = x + 0.5 * swiglu(rms_norm(x, ffn1_norm[l]), ffn1_wg[l], ffn1_wu[l], ffn1_wd[l])
        z = rms_norm(x, mix_norm[l]) @ w_in[l]
        q, k_c, v_c, k_s, v_s, k_w, v_w, g_logit, u = jnp.split(z, split_at, axis=-1)
        o_nsa = nsa_mixer(q, k_c, v_c, k_s, v_s, k_w, v_w, g_logit,
                          cmp_pe_k[l], cmp_wk1[l], cmp_wk2[l], cmp_pe_v[l], cmp_wv1[l], cmp_wv2[l])
        o_pool = pool_mixer(u, pool_w[l], pool_scale[l])
        x = x + jnp.concatenate([o_nsa, o_pool.astype(o_nsa.dtype)], axis=-1).astype(x.dtype) @ w_out[l]
        x = x + 0.5 * swiglu(rms_norm(x, ffn2_norm[l]), ffn2_wg[l], ffn2_wu[l], ffn2_wd[l])
    return rms_norm(x, final_norm)
```

```python
import functools

import numpy as np
import jax
import jax.numpy as jnp
from jax import lax
from jax.experimental import pallas as pl
from jax.experimental.pallas import tpu as pltpu

HEAD_DIM = 64
NSA_HEADS = 8
KV_GROUPS = 2
HEADS_PER_GROUP = NSA_HEADS // KV_GROUPS
NSA_WIDTH = NSA_HEADS * HEAD_DIM
KV_WIDTH = KV_GROUPS * HEAD_DIM
CMP_BLOCK = 32
CMP_STRIDE = 16
CMP_HIDDEN = 128
SEL_BLOCK = 64
N_SEL = 8
WINDOW = 512
Q_BLOCK = 128
POOL_WIDTHS = (2, 4, 8, 16)
POOL_GROUP = 128
POOL_CH = POOL_GROUP * len(POOL_WIDTHS)
N_GATES = 3 * NSA_HEADS
EPS = 1e-6

LANES = 128
SEL_CHUNK = 512
WIN_KEYS = WINDOW + Q_BLOCK
POOL_HALO = 16
FF_CHUNK = 256
NEG = -1e30
VMEM_LIMIT = 56 * 1024 * 1024

F32 = jnp.float32
BF16 = jnp.bfloat16


def _slopes():
    n = NSA_HEADS
    return (2.0 ** (-8.0 * np.arange(1, n + 1) / n)).astype(np.float32)


def _dot(a, b):
    return jnp.dot(a, b, preferred_element_type=F32)


def _dot_nt(a, b):
    return lax.dot_general(a, b, (((1,), (1,)), ((), ())), preferred_element_type=F32)


def _rms(x, g):
    return x * lax.rsqrt(jnp.mean(x * x, axis=-1, keepdims=True) + EPS) * g


def _const_spec(shape):
    nd = len(shape)
    return pl.BlockSpec(shape, lambda *_: (0,) * nd)


def _ffn_kernel(x_ref, g_ref, wg_ref, wu_ref, wd_ref, *rest, n_chunks, final):
    if final:
        fn_ref, o_ref, h_ref, acc_ref = rest
    else:
        o_ref, h_ref, acc_ref = rest
    h_ref[...] = _rms(x_ref[...], g_ref[...]).astype(BF16)
    acc_ref[...] = jnp.zeros_like(acc_ref)

    def body(c, carry):
        h = h_ref[...]
        gate = _dot(h, wg_ref[c])
        up = _dot(h, wu_ref[c])
        a = (gate * jax.nn.sigmoid(gate) * up).astype(BF16)
        acc_ref[...] += _dot(a, wd_ref[c])
        return carry

    lax.fori_loop(0, n_chunks, body, 0)
    y = x_ref[...] + 0.5 * acc_ref[...]
    if final:
        y = _rms(y, fn_ref[...])
    o_ref[...] = y


def _ffn(x2, g, wg, wu, wd, final_g=None, tm=512):
    n, d = x2.shape
    n_chunks = wg.shape[0]
    final = final_g is not None
    in_specs = [
        pl.BlockSpec((tm, d), lambda i: (i, 0)),
        _const_spec((1, d)),
        _const_spec(wg.shape),
        _const_spec(wu.shape),
        _const_spec(wd.shape),
    ]
    args = [x2, g, wg, wu, wd]
    if final:
        in_specs.append(_const_spec((1, d)))
        args.append(final_g)
    return pl.pallas_call(
        functools.partial(_ffn_kernel, n_chunks=n_chunks, final=final),
        grid=(n // tm,),
        in_specs=in_specs,
        out_specs=pl.BlockSpec((tm, d), lambda i: (i, 0)),
        out_shape=jax.ShapeDtypeStruct((n, d), F32),
        scratch_shapes=[pltpu.VMEM((tm, d), BF16), pltpu.VMEM((tm, d), F32)],
        compiler_params=pltpu.CompilerParams(
            dimension_semantics=("parallel",), vmem_limit_bytes=VMEM_LIMIT),
        name="ffn_final" if final else "ffn",
    )(*args)


_PROJ_Q = (0, 512)
_PROJ_KC = (512, 640)
_PROJ_VC = (640, 768)
_PROJ_KS = (768, 896)
_PROJ_VS = (896, 1024)
_PROJ_KW = (1024, 1152)
_PROJ_VW = (1152, 1280)
_PROJ_U = (1280, 1792)
_PROJ_GT = (1792, 1920)
_PROJ_WIDTH = 1920


def _proj_kernel(x_ref, g_ref, w_ref, q_ref, kcin_ref, vcin_ref, ks_ref, vs_ref, kw_ref, vw_ref,
                 u_ref, gt_ref):
    h = _rms(x_ref[...], g_ref[...]).astype(BF16)

    def part(lo_hi):
        return _dot(h, w_ref[:, lo_hi[0]:lo_hi[1]])

    q_ref[...] = (part(_PROJ_Q) * (HEAD_DIM ** -0.5)).astype(BF16)
    kcin_ref[...] = part(_PROJ_KC)
    vcin_ref[...] = part(_PROJ_VC)
    ks_ref[...] = part(_PROJ_KS).astype(BF16)
    vs_ref[...] = part(_PROJ_VS).astype(BF16)
    kw_ref[...] = part(_PROJ_KW).astype(BF16)
    vw_ref[...] = part(_PROJ_VW).astype(BF16)
    u_ref[...] = part(_PROJ_U)
    gt_ref[...] = part(_PROJ_GT)


def _proj(x2, g, w, tm=512):
    n, d = x2.shape
    widths = [(512, BF16), (128, F32), (128, F32), (128, BF16), (128, BF16), (128, BF16), (128, BF16),
              (512, F32), (128, F32)]
    return pl.pallas_call(
        _proj_kernel,
        grid=(n // tm,),
        in_specs=[pl.BlockSpec((tm, d), lambda i: (i, 0)), _const_spec((1, d)), _const_spec(w.shape)],
        out_specs=[pl.BlockSpec((tm, wd), lambda i: (i, 0)) for wd, _ in widths],
        out_shape=[jax.ShapeDtypeStruct((n, wd), dt) for wd, dt in widths],
        compiler_params=pltpu.CompilerParams(
            dimension_semantics=("parallel",), vmem_limit_bytes=VMEM_LIMIT),
        name="proj",
    )(x2, g, w)


def _gelu_tanh(x):
    c = np.float32(np.sqrt(2.0 / np.pi))
    return 0.5 * x * (1.0 + jnp.tanh(c * (x + 0.044715 * (x * x * x))))


def _compress_kernel(kcin_ref, vcin_ref, pek_ref, wk1_ref, wk2_ref, pev_ref, wv1_ref, wv2_ref,
                     kc_ref, vc_ref, *, n_rows):
    nl = CMP_STRIDE
    hid2 = KV_GROUPS * CMP_HIDDEN

    def hidden(src_ref, pe_ref, w1_ref):
        acc = jnp.zeros((n_rows + 16, 2 * hid2), F32)
        for l in range(nl):
            xl = src_ref[pl.ds(l, n_rows, stride=nl), :]
            pa = jnp.broadcast_to(pe_ref[l:l + 1, :], (8, KV_WIDTH))
            pb = jnp.broadcast_to(pe_ref[nl + l:nl + l + 1, :], (8, KV_WIDTH))
            lhs = jnp.concatenate([xl, pa, pb], axis=0).astype(BF16)
            acc = acc + _dot(lhs, w1_ref[l])
        sa = acc[0:n_rows, 0:hid2]
        sb = acc[0:n_rows, hid2:2 * hid2]
        bias = acc[n_rows:n_rows + 1, 0:hid2] + acc[n_rows + 8:n_rows + 9, hid2:2 * hid2]
        hcur = sa + pltpu.roll(sb, n_rows - 1, axis=0) + bias
        return _gelu_tanh(hcur).astype(BF16)

    hk = hidden(kcin_ref, pek_ref, wk1_ref)
    kc = _dot(hk, wk2_ref[...])
    for g in range(KV_GROUPS):
        kc_ref[g] = kc[:, g * LANES:(g + 1) * LANES].astype(BF16)
    hv = hidden(vcin_ref, pev_ref, wv1_ref)
    vc = _dot(hv, wv2_ref[...])
    for j in range(2 * KV_GROUPS):
        vc_ref[j] = vc[:, j * LANES:(j + 1) * LANES].astype(BF16)


def _compress(kcin, vcin, pek2, wk1b, wk2b, pev2, wv1b, wv2b):
    b, t, _ = kcin.shape
    n_rows = t // CMP_STRIDE
    seq_spec = pl.BlockSpec((None, t, KV_WIDTH), lambda i: (i, 0, 0))
    return pl.pallas_call(
        functools.partial(_compress_kernel, n_rows=n_rows),
        grid=(b,),
        in_specs=[seq_spec, seq_spec,
                  _const_spec(pek2.shape), _const_spec(wk1b.shape), _const_spec(wk2b.shape),
                  _const_spec(pev2.shape), _const_spec(wv1b.shape), _const_spec(wv2b.shape)],
        out_specs=[pl.BlockSpec((None, KV_GROUPS, n_rows, LANES), lambda i: (i, 0, 0, 0)),
                   pl.BlockSpec((None, 2 * KV_GROUPS, n_rows, LANES), lambda i: (i, 0, 0, 0))],
        out_shape=[jax.ShapeDtypeStruct((b, KV_GROUPS, n_rows, LANES), BF16),
                   jax.ShapeDtypeStruct((b, 2 * KV_GROUPS, n_rows, LANES), BF16)],
        compiler_params=pltpu.CompilerParams(
            dimension_semantics=("parallel",), vmem_limit_bytes=VMEM_LIMIT),
        name="compress",
    )(kcin, vcin, pek2, wk1b, wk2b, pev2, wv1b, wv2b)


def _compress_weights(pe, w1, w2, value_layout):
    nl, dh, hid = CMP_STRIDE, HEAD_DIM, CMP_HIDDEN
    w1r = w1.reshape(2, nl, dh, hid)
    z = jnp.zeros((nl, dh, hid), w1.dtype)
    row_g0 = jnp.concatenate([w1r[0], z, w1r[1], z], axis=-1)
    row_g1 = jnp.concatenate([z, w1r[0], z, w1r[1]], axis=-1)
    w1b = jnp.concatenate([row_g0, row_g1], axis=1).astype(BF16)
    pe2 = jnp.concatenate([pe, pe], axis=-1)
    zz = jnp.zeros((hid, dh), w2.dtype)
    if value_layout:
        top = jnp.concatenate([w2, zz, zz, w2, zz, zz, zz, zz], axis=-1)
        bot = jnp.concatenate([zz, zz, zz, zz, w2, zz, zz, w2], axis=-1)
    else:
        top = jnp.concatenate([w2, zz, zz, zz], axis=-1)
        bot = jnp.concatenate([zz, zz, w2, zz], axis=-1)
    w2b = jnp.concatenate([top, bot], axis=0).astype(BF16)
    return pe2, w1b, w2b


def _nsa_tables(t):
    slopes = _slopes().reshape(KV_GROUPS, HEADS_PER_GROUP)
    n_rows = t // CMP_STRIDE
    n_blk = t // SEL_BLOCK
    i = np.arange(Q_BLOCK)
    c = np.arange(n_rows)
    d0c = (i[:, None] - (c[None, :] * CMP_STRIDE + CMP_BLOCK - 1)).astype(np.int32)
    tblc = -slopes[:, :, None, None] * d0c[None, None].astype(np.float32)
    tblc = tblc.reshape(KV_GROUPS, HEADS_PER_GROUP * Q_BLOCK, n_rows)
    d0c = np.tile(d0c, (HEADS_PER_GROUP, 1))
    j = np.arange(WIN_KEYS)
    dist = i[:, None] + WINDOW - j[None, :]
    ok = (dist >= 0) & (dist < WINDOW)
    tblw = np.where(ok[None, None], -slopes[:, :, None, None] * dist[None, None].astype(np.float32),
                    np.float32(NEG)).astype(np.float32)
    tblw = tblw.reshape(KV_GROUPS, HEADS_PER_GROUP * Q_BLOCK, WIN_KEYS)
    jj = np.arange(SEL_CHUNK)
    per = SEL_CHUNK // Q_BLOCK
    tailm = np.stack([np.where(jj[None, :] > i[:, None] + Q_BLOCK * r, np.float32(NEG), np.float32(0))
                      for r in range(per)]).astype(np.float32)
    ci = c[None, :] * CMP_STRIDE
    sj = np.arange(n_blk)[:, None] * SEL_BLOCK
    n_cmp = (t - CMP_BLOCK) // CMP_STRIDE + 1
    ovt = ((ci <= sj + SEL_BLOCK - 1) & (ci + CMP_BLOCK - 1 >= sj) & (c[None, :] < n_cmp))
    ovt = ovt.astype(np.float32)
    pos = np.arange(t)
    ktag = np.zeros((t, 2 * LANES), np.float32)
    ktag[pos, HEAD_DIM + pos // SEL_BLOCK] = 1.0
    ktag[:, LANES] = pos % SEL_BLOCK
    return (jnp.asarray(tblc), jnp.asarray(d0c), jnp.asarray(tblw), jnp.asarray(tailm),
            jnp.asarray(ovt, dtype=BF16), jnp.asarray(ktag, dtype=BF16))


def _nsa_kernel(q_ref, gt_ref, ks_ref, vs_ref, kw_ref, vw_ref, kc_ref, vc_ref,
                tblc_ref, d0c_ref, tblw_ref, tailm_ref, ovt_ref, ktag_ref,
                o_ref,
                kaug_ref, kwaug_ref, vsx_ref, vwx_ref, *, t, n_blk):
    qi = pl.program_id(1)
    slopes = _slopes()
    hg = HEADS_PER_GROUP
    prep_rows = 512

    @pl.when(qi == 0)
    def _prep():
        low = lax.broadcasted_iota(jnp.int32, (prep_rows, LANES), 1) < HEAD_DIM
        flag = jnp.where(lax.broadcasted_iota(jnp.int32, (WINDOW, LANES), 1) == HEAD_DIM, 1.0, 0.0)
        zpad = jnp.zeros((WINDOW, LANES), BF16)
        for g in range(KV_GROUPS):
            kwaug_ref[g, 0:WINDOW, :] = flag.astype(BF16)
        for j in range(2 * KV_GROUPS):
            vwx_ref[j, 0:WINDOW, :] = zpad

        def body(c, carry):
            r0 = pl.multiple_of(c * prep_rows, prep_rows)
            rows = pl.ds(r0, prep_rows)
            rows_w = pl.ds(r0 + WINDOW, prep_rows)
            tag = ktag_ref[rows, 0:LANES].astype(F32)
            ks = ks_ref[rows, :].astype(F32)
            kw = kw_ref[rows, :].astype(F32)
            vs = vs_ref[rows, :].astype(F32)
            vw = vw_ref[rows, :].astype(F32)
            swapped = [pltpu.roll(a, HEAD_DIM, axis=1) for a in (ks, kw, vs, vw)]
            for g in range(KV_GROUPS):
                k_lo = ks if g == 0 else swapped[0]
                kw_lo = kw if g == 0 else swapped[1]
                kaug_ref[g, rows, 0:LANES] = jnp.where(low, k_lo, tag).astype(BF16)
                kaug_ref[g, rows, LANES:2 * LANES] = ktag_ref[rows, LANES:2 * LANES]
                kwaug_ref[g, rows_w, :] = jnp.where(low, kw_lo, 0.0).astype(BF16)
                v_lo = vs if g == 0 else swapped[2]
                v_hi = swapped[2] if g == 0 else vs
                vsx_ref[2 * g, rows, :] = jnp.where(low, v_lo, 0.0).astype(BF16)
                vsx_ref[2 * g + 1, rows, :] = jnp.where(low, 0.0, v_hi).astype(BF16)
                w_lo = vw if g == 0 else swapped[3]
                w_hi = swapped[3] if g == 0 else vw
                vwx_ref[2 * g, rows_w, :] = jnp.where(low, w_lo, 0.0).astype(BF16)
                vwx_ref[2 * g + 1, rows_w, :] = jnp.where(low, 0.0, w_hi).astype(BF16)
            return carry

        lax.fori_loop(0, t // prep_rows, body, 0)

    lane = lax.broadcasted_iota(jnp.int32, (Q_BLOCK, LANES), 1)
    low = lane < HEAD_DIM
    row = lax.broadcasted_iota(jnp.int32, (Q_BLOCK, 1), 0)
    cur_col = 2 * qi + (row >= SEL_BLOCK).astype(jnp.int32)
    gates = jax.nn.sigmoid(gt_ref[...])

    def lane_pair(a, b):
        return jnp.where(low, a, b)

    def softmax_rows(s):
        m = jnp.max(s, axis=-1, keepdims=True)
        m = jnp.where(m < 0.5 * NEG, 0.0, m)
        p = jnp.exp(s - m)
        l = jnp.sum(p, axis=-1, keepdims=True)
        return p, 1.0 / jnp.maximum(l, 1e-30)

    for g in range(KV_GROUPS):
        slab_a = q_ref[:, g * 256:g * 256 + LANES].astype(F32)
        slab_b = q_ref[:, g * 256 + LANES:(g + 1) * 256].astype(F32)
        q_low = [slab_a, pltpu.roll(slab_a, HEAD_DIM, axis=1), slab_b, pltpu.roll(slab_b, HEAD_DIM, axis=1)]
        flagc = jnp.where(lane == HEAD_DIM, NEG, 0.0)
        qw = jnp.concatenate([jnp.where(low, q_low[h], flagc) for h in range(hg)], axis=0).astype(BF16)

        s = _dot_nt(qw, kc_ref[g]) + tblc_ref[g]
        s = jnp.where(d0c_ref[...] + Q_BLOCK * qi >= 0, s, NEG)
        p, inv = softmax_rows(s)
        pn = p * inv
        o_cmp = []
        psum = None
        for pr in range(hg // 2):
            pa = pn[(2 * pr) * Q_BLOCK:(2 * pr + 1) * Q_BLOCK]
            pb = pn[(2 * pr + 1) * Q_BLOCK:(2 * pr + 2) * Q_BLOCK]
            o_cmp.append(_dot(pa.astype(BF16), vc_ref[2 * g]) + _dot(pb.astype(BF16), vc_ref[2 * g + 1]))
            psum = pa + pb if psum is None else psum + pa + pb

        p_hi = psum.astype(BF16)
        p_lo = (psum - p_hi.astype(F32)).astype(BF16)
        imp = _dot_nt(ovt_ref[...], p_hi) + _dot_nt(ovt_ref[...], p_lo)
        bi = lax.broadcasted_iota(jnp.int32, (n_blk, Q_BLOCK), 0)
        lq = lax.broadcasted_iota(jnp.int32, (n_blk, Q_BLOCK), 1)
        cur = 2 * qi + (lq >= SEL_BLOCK).astype(jnp.int32)
        valid = bi <= cur
        near = (bi == 0) | (bi >= cur - 1)
        val = jnp.where(valid, jnp.where(near, 3e38, imp), -1.0)
        sel = jnp.zeros((n_blk, Q_BLOCK), F32)
        for _ in range(min(N_SEL, n_blk)):
            top = jnp.max(val, axis=0, keepdims=True)
            idx = jnp.min(jnp.where(val == top, bi, n_blk), axis=0, keepdims=True)
            pick = bi == idx
            sel = jnp.where(pick, 1.0, sel)
            val = jnp.where(pick, -2.0, val)
        sel = jnp.where(valid, sel, 0.0)
        parts = [jnp.zeros((HEAD_DIM, Q_BLOCK), F32), sel]
        if n_blk < LANES - HEAD_DIM:
            parts.append(jnp.zeros((LANES - HEAD_DIM - n_blk, Q_BLOCK), F32))
        sel_q = jnp.concatenate(parts, axis=0).T

        rel_blk = (lane - HEAD_DIM - cur_col).astype(F32)
        q_rows = []
        for h in range(hg):
            sl = float(slopes[g * hg + h])
            bterm = jnp.where(sel_q > 0.0, (sl * SEL_BLOCK) * rel_blk, NEG)
            lo = jnp.where(low, q_low[h], bterm)
            hi = jnp.where(lane == 0, sl, 0.0)
            q_rows.append(jnp.concatenate([lo, hi], axis=1))
        qaug = jnp.concatenate(q_rows, axis=0).astype(BF16)

        def sel_step(start, mask, carry):
            m, l, acc0, acc1 = carry
            rows = pl.ds(start, SEL_CHUNK)
            s = _dot_nt(qaug, kaug_ref[g, rows, :])
            if mask is not None:
                s = s + mask
            m_new = jnp.maximum(m, jnp.max(s, axis=-1, keepdims=True))
            alpha = jnp.exp(m - m_new)
            p = jnp.exp(s - m_new)
            l = alpha * l + jnp.sum(p, axis=-1, keepdims=True)
            pb16 = p.astype(BF16)
            accs = []
            for pr, acc in enumerate((acc0, acc1)):
                ra = slice((2 * pr) * Q_BLOCK, (2 * pr + 1) * Q_BLOCK)
                rb = slice((2 * pr + 1) * Q_BLOCK, (2 * pr + 2) * Q_BLOCK)
                upd = _dot(pb16[ra], vsx_ref[2 * g, rows, :]) + _dot(pb16[rb], vsx_ref[2 * g + 1, rows, :])
                accs.append(acc * lane_pair(alpha[ra], alpha[rb]) + upd)
            return m_new, l, accs[0], accs[1]

        init = (jnp.full((hg * Q_BLOCK, 1), NEG, F32), jnp.zeros((hg * Q_BLOCK, 1), F32),
                jnp.zeros((Q_BLOCK, LANES), F32), jnp.zeros((Q_BLOCK, LANES), F32))
        per = SEL_CHUNK // Q_BLOCK
        n_full = qi // per
        carry = lax.fori_loop(
            0, n_full,
            lambda c, cr: sel_step(pl.multiple_of(c * SEL_CHUNK, SEL_CHUNK), None, cr), init)
        tail_mask = tailm_ref[qi % per]
        tail_mask = jnp.concatenate([tail_mask] * hg, axis=0)
        m, l, acc0, acc1 = sel_step(pl.multiple_of(n_full * SEL_CHUNK, SEL_CHUNK), tail_mask, carry)
        inv = 1.0 / l
        o_slc = [acc0 * lane_pair(inv[0:Q_BLOCK], inv[Q_BLOCK:2 * Q_BLOCK]),
                 acc1 * lane_pair(inv[2 * Q_BLOCK:3 * Q_BLOCK], inv[3 * Q_BLOCK:4 * Q_BLOCK])]

        rows = pl.ds(pl.multiple_of(qi * Q_BLOCK, Q_BLOCK), WIN_KEYS)
        s = _dot_nt(qw, kwaug_ref[g, rows, :]) + tblw_ref[g]
        p, inv = softmax_rows(s)
        pb16 = p.astype(BF16)
        o_win = []
        for pr in range(hg // 2):
            ra = slice((2 * pr) * Q_BLOCK, (2 * pr + 1) * Q_BLOCK)
            rb = slice((2 * pr + 1) * Q_BLOCK, (2 * pr + 2) * Q_BLOCK)
            o = _dot(pb16[ra], vwx_ref[2 * g, rows, :]) + _dot(pb16[rb], vwx_ref[2 * g + 1, rows, :])
            o_win.append(o * lane_pair(inv[ra], inv[rb]))

        for pr in range(hg // 2):
            ha = g * hg + 2 * pr
            out = None
            for j, branch in enumerate((o_cmp, o_slc, o_win)):
                gate = lane_pair(gates[:, 3 * ha + j:3 * ha + j + 1], gates[:, 3 * ha + 3 + j:3 * ha + 4 + j])
                term = gate * branch[pr]
                out = term if out is None else out + term
            o_ref[:, (g * 2 + pr) * LANES:(g * 2 + pr + 1) * LANES] = out.astype(BF16)


def _nsa(q, gt, ks, vs, kw, vw, kc, vc, tables):
    b, t, _ = q.shape
    n_blk = t // SEL_BLOCK
    n_rows = t // CMP_STRIDE
    assert t % SEL_CHUNK == 0 and N_SEL <= n_blk <= LANES - HEAD_DIM
    seq_spec = pl.BlockSpec((None, t, LANES), lambda i, j: (i, 0, 0))
    blk = lambda w: pl.BlockSpec((None, Q_BLOCK, w), lambda i, j: (i, j, 0))
    tblc, d0c, tblw, tailm, ovt, ktag = tables
    return pl.pallas_call(
        functools.partial(_nsa_kernel, t=t, n_blk=n_blk),
        grid=(b, t // Q_BLOCK),
        in_specs=[blk(NSA_WIDTH), blk(LANES), seq_spec, seq_spec, seq_spec, seq_spec,
                  pl.BlockSpec((None, KV_GROUPS, n_rows, LANES), lambda i, j: (i, 0, 0, 0)),
                  pl.BlockSpec((None, 2 * KV_GROUPS, n_rows, LANES), lambda i, j: (i, 0, 0, 0)),
                  _const_spec(tblc.shape), _const_spec(d0c.shape), _const_spec(tblw.shape),
                  _const_spec(tailm.shape), _const_spec(ovt.shape), _const_spec(ktag.shape)],
        out_specs=blk(NSA_WIDTH),
        out_shape=jax.ShapeDtypeStruct((b, t, NSA_WIDTH), BF16),
        scratch_shapes=[pltpu.VMEM((KV_GROUPS, t, 2 * LANES), BF16),
                        pltpu.VMEM((KV_GROUPS, t + WINDOW, LANES), BF16),
                        pltpu.VMEM((2 * KV_GROUPS, t, LANES), BF16),
                        pltpu.VMEM((2 * KV_GROUPS, t + WINDOW, LANES), BF16)],
        compiler_params=pltpu.CompilerParams(
            dimension_semantics=("arbitrary", "arbitrary"), vmem_limit_bytes=VMEM_LIMIT),
        name="nsa",
    )(q, gt, ks, vs, kw, vw, kc, vc, tblc, d0c, tblw, tailm, ovt, ktag)


def _mix_out_kernel(x_ref, o_ref, u_ref, halo_ref, wp_ref, sc_ref, wo_ref, y_ref, mix_ref, *, tm):
    ti = pl.program_id(1)
    mix_ref[:, 0:NSA_WIDTH] = o_ref[...]
    halo = jnp.where(ti > 0, halo_ref[...], 0.0)
    pos = ti * tm + lax.broadcasted_iota(jnp.int32, (tm, 1), 0)
    for gi, w in enumerate(POOL_WIDTHS):
        lanes = slice(gi * POOL_GROUP, (gi + 1) * POOL_GROUP)
        u = u_ref[:, lanes]
        ext = jnp.concatenate([halo[:, lanes], u], axis=0)
        run = ext
        step = 1
        while step < w:
            run = run + pltpu.roll(run, step, axis=0)
            step *= 2
        cnt = jnp.minimum(pos + 1, w).astype(F32)
        y = run[POOL_HALO:, :] / cnt - u
        yo = _dot(y.astype(BF16), wp_ref[gi]) * sc_ref[:, lanes]
        mix_ref[:, NSA_WIDTH + gi * POOL_GROUP:NSA_WIDTH + (gi + 1) * POOL_GROUP] = yo.astype(BF16)
    y_ref[...] = x_ref[...] + _dot(mix_ref[...], wo_ref[...])


def _mix_out(x3, o_nsa, u, wp, sc, wo, tm=512):
    b, t, d = x3.shape
    per = tm // POOL_HALO
    return pl.pallas_call(
        functools.partial(_mix_out_kernel, tm=tm),
        grid=(b, t // tm),
        in_specs=[pl.BlockSpec((None, tm, d), lambda i, j: (i, j, 0)),
                  pl.BlockSpec((None, tm, NSA_WIDTH), lambda i, j: (i, j, 0)),
                  pl.BlockSpec((None, tm, POOL_CH), lambda i, j: (i, j, 0)),
                  pl.BlockSpec((None, POOL_HALO, POOL_CH), lambda i, j: (i, jnp.maximum(j * per - 1, 0), 0)),
                  _const_spec(wp.shape), _const_spec(sc.shape), _const_spec(wo.shape)],
        out_specs=pl.BlockSpec((None, tm, d), lambda i, j: (i, j, 0)),
        out_shape=jax.ShapeDtypeStruct((b, t, d), F32),
        scratch_shapes=[pltpu.VMEM((tm, NSA_WIDTH + POOL_CH), BF16)],
        compiler_params=pltpu.CompilerParams(
            dimension_semantics=("parallel", "parallel"), vmem_limit_bytes=VMEM_LIMIT),
        name="mix_out",
    )(x3, o_nsa, u, u, wp, sc, wo)


def _chunk_cols(w):
    d, f = w.shape
    return w.reshape(d, f // FF_CHUNK, FF_CHUNK).transpose(1, 0, 2).astype(BF16)


def _chunk_rows(w):
    f, d = w.shape
    return w.reshape(f // FF_CHUNK, FF_CHUNK, d).astype(BF16)


def _rearrange_w_in(w):
    a = NSA_WIDTH + 6 * KV_WIDTH
    pad = jnp.zeros((w.shape[0], _PROJ_WIDTH - w.shape[1]), w.dtype)
    return jnp.concatenate([w[:, :a], w[:, a + N_GATES:], w[:, a:a + N_GATES], pad], axis=1).astype(BF16)


def kernel(x, ffn1_norm, ffn1_wg, ffn1_wu, ffn1_wd, mix_norm, w_in, cmp_pe_k, cmp_wk1, cmp_wk2,
           cmp_pe_v, cmp_wv1, cmp_wv2, pool_w, pool_scale, w_out, ffn2_norm, ffn2_wg, ffn2_wu,
           ffn2_wd, final_norm):
    b, t, d = x.shape
    depth = ffn1_norm.shape[0]
    n = b * t
    tables = _nsa_tables(t)
    x2 = x.reshape(n, d)
    for l in range(depth):
        x2 = _ffn(x2, ffn1_norm[l][None], _chunk_cols(ffn1_wg[l]), _chunk_cols(ffn1_wu[l]),
                  _chunk_rows(ffn1_wd[l]))
        q, kcin, vcin, ks, vs, kw, vw, u, gt = _proj(x2, mix_norm[l][None], _rearrange_w_in(w_in[l]))
        kc, vc = _compress(kcin.reshape(b, t, -1), vcin.reshape(b, t, -1),
                           *_compress_weights(cmp_pe_k[l], cmp_wk1[l], cmp_wk2[l], False),
                           *_compress_weights(cmp_pe_v[l], cmp_wv1[l], cmp_wv2[l], True))
        seq = lambda a: a.reshape(b, t, a.shape[-1])
        o_nsa = _nsa(seq(q), seq(gt), seq(ks), seq(vs), seq(kw), seq(vw), kc, vc, tables)
        x2 = _mix_out(x2.reshape(b, t, d), o_nsa, seq(u), pool_w[l].astype(BF16),
                      pool_scale[l][None], w_out[l].astype(BF16)).reshape(n, d)
        x2 = _ffn(x2, ffn2_norm[l][None], _chunk_cols(ffn2_wg[l]), _chunk_cols(ffn2_wu[l]),
                  _chunk_rows(ffn2_wd[l]), final_g=final_norm[None] if l == depth - 1 else None)
    return x2.reshape(b, t, d)
```

```python
import functools

import numpy as np
import jax
import jax.numpy as jnp
from jax import lax
from jax.experimental import pallas as pl
from jax.experimental.pallas import tpu as pltpu

HEAD_DIM = 64
NSA_HEADS = 8
KV_GROUPS = 2
HEADS_PER_GROUP = NSA_HEADS // KV_GROUPS
NSA_WIDTH = NSA_HEADS * HEAD_DIM
KV_WIDTH = KV_GROUPS * HEAD_DIM
CMP_BLOCK = 32
CMP_STRIDE = 16
CMP_HIDDEN = 128
SEL_BLOCK = 64
N_SEL = 8
WINDOW = 512
Q_BLOCK = 128
POOL_WIDTHS = (2, 4, 8, 16)
POOL_GROUP = 128
POOL_CH = POOL_GROUP * len(POOL_WIDTHS)
N_GATES = 3 * NSA_HEADS
GATE_ROWS = 32
EPS = 1e-6

LANES = 128
GQ = HEADS_PER_GROUP * Q_BLOCK
SEL_CHUNK = 512
WIN_KEYS = WINDOW + Q_BLOCK
ONES_ROWS = 16
POOL_HALO = 16
FF_CHUNK = 256
NEG = -1e30
VMEM_LIMIT = 56 * 1024 * 1024

F32 = jnp.float32
BF16 = jnp.bfloat16


def _slopes():
    n = NSA_HEADS
    return (2.0 ** (-8.0 * np.arange(1, n + 1) / n)).astype(np.float32)


def _dot(a, b):
    return jnp.dot(a, b, preferred_element_type=F32)


def _dot_nt(a, b):
    return lax.dot_general(a, b, (((1,), (1,)), ((), ())), preferred_element_type=F32)


def _rms(x, g):
    return x * lax.rsqrt(jnp.mean(x * x, axis=-1, keepdims=True) + EPS) * g


def _const_spec(shape):
    nd = len(shape)
    return pl.BlockSpec(shape, lambda *_: (0,) * nd)


def _ffn_kernel(x_ref, g_ref, wg_ref, wu_ref, wd_ref, *rest, n_chunks, final):
    if final:
        fn_ref, o_ref, h_ref, acc_ref = rest
    else:
        o_ref, h_ref, acc_ref = rest
    h_ref[...] = _rms(x_ref[...], g_ref[...]).astype(BF16)
    acc_ref[...] = jnp.zeros_like(acc_ref)

    def body(c, carry):
        h = h_ref[...]
        gate = _dot(h, wg_ref[c])
        up = _dot(h, wu_ref[c])
        a = (gate * jax.nn.sigmoid(gate) * up).astype(BF16)
        acc_ref[...] += _dot(a, wd_ref[c])
        return carry

    lax.fori_loop(0, n_chunks, body, 0)
    y = x_ref[...] + 0.5 * acc_ref[...]
    if final:
        y = _rms(y, fn_ref[...])
    o_ref[...] = y


def _ffn(x2, g, wg, wu, wd, final_g=None, tm=512):
    n, d = x2.shape
    n_chunks = wg.shape[0]
    final = final_g is not None
    in_specs = [
        pl.BlockSpec((tm, d), lambda i: (i, 0)),
        _const_spec((1, d)),
        _const_spec(wg.shape),
        _const_spec(wu.shape),
        _const_spec(wd.shape),
    ]
    args = [x2, g, wg, wu, wd]
    if final:
        in_specs.append(_const_spec((1, d)))
        args.append(final_g)
    return pl.pallas_call(
        functools.partial(_ffn_kernel, n_chunks=n_chunks, final=final),
        grid=(n // tm,),
        in_specs=in_specs,
        out_specs=pl.BlockSpec((tm, d), lambda i: (i, 0)),
        out_shape=jax.ShapeDtypeStruct((n, d), F32),
        scratch_shapes=[pltpu.VMEM((tm, d), BF16), pltpu.VMEM((tm, d), F32)],
        compiler_params=pltpu.CompilerParams(
            dimension_semantics=("parallel",), vmem_limit_bytes=VMEM_LIMIT),
        name="ffn_final" if final else "ffn",
    )(*args)


_TOK_KC = (0, 128)
_TOK_VC = (128, 256)
_TOK_KS = (256, 384)
_TOK_KW = (384, 512)
_TOK_U = (512, 1024)
_FEAT_Q = (0, 512)
_FEAT_VS = (512, 640)
_FEAT_VW = (640, 768)
_FEAT_GT = (768, 768 + GATE_ROWS)


def _proj_kernel(x_ref, g_ref, wt_ref, wf_ref, kcin_ref, vcin_ref, ks_ref, kw_ref, u_ref,
                 qt_ref, vst_ref, vwt_ref, gt_ref, *, tm):
    h = _rms(x_ref[...], g_ref[...]).astype(BF16)

    def tok(lo_hi):
        return _dot(h, wt_ref[:, lo_hi[0]:lo_hi[1]])

    kcin_ref[...] = tok(_TOK_KC)
    vcin_ref[...] = tok(_TOK_VC)
    ks_ref[...] = tok(_TOK_KS).astype(BF16)
    kw_ref[...] = tok(_TOK_KW).astype(BF16)
    u_ref[...] = tok(_TOK_U)
    zt = _dot_nt(wf_ref[...], h)
    qt_ref[...] = (zt[_FEAT_Q[0]:_FEAT_Q[1]] * (HEAD_DIM ** -0.5)).astype(BF16)
    gt_ref[...] = zt[_FEAT_GT[0]:_FEAT_GT[1]]
    for c in range(tm // LANES):
        cols = slice(c * LANES, (c + 1) * LANES)
        vst_ref[c] = zt[_FEAT_VS[0]:_FEAT_VS[1], cols].astype(BF16)
        vwt_ref[c] = zt[_FEAT_VW[0]:_FEAT_VW[1], cols].astype(BF16)


def _proj(x3, g, w_tok, w_feat, tm=512):
    b, t, d = x3.shape
    nck = tm // LANES
    tok = lambda wd: pl.BlockSpec((None, tm, wd), lambda i, j: (i, j, 0))
    chunked = pl.BlockSpec((None, nck, KV_WIDTH, LANES), lambda i, j: (i, j, 0, 0))
    return pl.pallas_call(
        functools.partial(_proj_kernel, tm=tm),
        grid=(b, t // tm),
        in_specs=[tok(d), _const_spec((1, d)), _const_spec(w_tok.shape), _const_spec(w_feat.shape)],
        out_specs=[tok(KV_WIDTH), tok(KV_WIDTH), tok(KV_WIDTH), tok(KV_WIDTH), tok(POOL_CH),
                   pl.BlockSpec((None, NSA_WIDTH, tm), lambda i, j: (i, 0, j)),
                   chunked, chunked,
                   pl.BlockSpec((None, GATE_ROWS, tm), lambda i, j: (i, 0, j))],
        out_shape=[jax.ShapeDtypeStruct((b, t, KV_WIDTH), F32),
                   jax.ShapeDtypeStruct((b, t, KV_WIDTH), F32),
                   jax.ShapeDtypeStruct((b, t, KV_WIDTH), BF16),
                   jax.ShapeDtypeStruct((b, t, KV_WIDTH), BF16),
                   jax.ShapeDtypeStruct((b, t, POOL_CH), F32),
                   jax.ShapeDtypeStruct((b, NSA_WIDTH, t), BF16),
                   jax.ShapeDtypeStruct((b, t // LANES, KV_WIDTH, LANES), BF16),
                   jax.ShapeDtypeStruct((b, t // LANES, KV_WIDTH, LANES), BF16),
                   jax.ShapeDtypeStruct((b, GATE_ROWS, t), F32)],
        compiler_params=pltpu.CompilerParams(
            dimension_semantics=("parallel", "parallel"), vmem_limit_bytes=VMEM_LIMIT),
        name="proj",
    )(x3, g, w_tok, w_feat)


def _split_w_in(w):
    q0, kv0 = 0, NSA_WIDTH
    col = lambda i: slice(kv0 + i * KV_WIDTH, kv0 + (i + 1) * KV_WIDTH)
    g0 = kv0 + 6 * KV_WIDTH
    w_tok = jnp.concatenate([w[:, col(0)], w[:, col(1)], w[:, col(2)], w[:, col(4)],
                             w[:, g0 + N_GATES:]], axis=1).astype(BF16)
    pad = jnp.zeros((w.shape[0], GATE_ROWS - N_GATES), w.dtype)
    w_feat = jnp.concatenate([w[:, q0:NSA_WIDTH], w[:, col(3)], w[:, col(5)],
                              w[:, g0:g0 + N_GATES], pad], axis=1).T.astype(BF16)
    return w_tok, w_feat


def _gelu_tanh(x):
    c = np.float32(np.sqrt(2.0 / np.pi))
    return 0.5 * x * (1.0 + jnp.tanh(c * (x + 0.044715 * (x * x * x))))


def _compress_kernel(kcin_ref, vcin_ref, pek_ref, wk1_ref, wk2_ref, pev_ref, wv1_ref, wv2_ref,
                     kc_ref, vct_ref, *, n_rows):
    nl = CMP_STRIDE
    hid2 = KV_GROUPS * CMP_HIDDEN

    def hidden(src_ref, pe_ref, w1_ref):
        acc = jnp.zeros((n_rows + 16, 2 * hid2), F32)
        for l in range(nl):
            xl = src_ref[pl.ds(l, n_rows, stride=nl), :]
            pa = jnp.broadcast_to(pe_ref[l:l + 1, :], (8, KV_WIDTH))
            pb = jnp.broadcast_to(pe_ref[nl + l:nl + l + 1, :], (8, KV_WIDTH))
            lhs = jnp.concatenate([xl, pa, pb], axis=0).astype(BF16)
            acc = acc + _dot(lhs, w1_ref[l])
        sa = acc[0:n_rows, 0:hid2]
        sb = acc[0:n_rows, hid2:2 * hid2]
        bias = acc[n_rows:n_rows + 1, 0:hid2] + acc[n_rows + 8:n_rows + 9, hid2:2 * hid2]
        hcur = sa + pltpu.roll(sb, n_rows - 1, axis=0) + bias
        return _gelu_tanh(hcur).astype(BF16)

    hk = hidden(kcin_ref, pek_ref, wk1_ref)
    kc = _dot(hk, wk2_ref[...])
    for g in range(KV_GROUPS):
        kc_ref[g] = kc[:, g * LANES:(g + 1) * LANES].astype(BF16)
    hv = hidden(vcin_ref, pev_ref, wv1_ref)
    for g in range(KV_GROUPS):
        vct_ref[g] = _dot_nt(wv2_ref[g], hv).astype(BF16)


def _compress(kcin, vcin, pek2, wk1b, wk2b, pev2, wv1b, wv2t):
    b, t, _ = kcin.shape
    n_rows = t // CMP_STRIDE
    seq_spec = pl.BlockSpec((None, t, KV_WIDTH), lambda i: (i, 0, 0))
    return pl.pallas_call(
        functools.partial(_compress_kernel, n_rows=n_rows),
        grid=(b,),
        in_specs=[seq_spec, seq_spec,
                  _const_spec(pek2.shape), _const_spec(wk1b.shape), _const_spec(wk2b.shape),
                  _const_spec(pev2.shape), _const_spec(wv1b.shape), _const_spec(wv2t.shape)],
        out_specs=[pl.BlockSpec((None, KV_GROUPS, n_rows, LANES), lambda i: (i, 0, 0, 0)),
                   pl.BlockSpec((None, KV_GROUPS, HEAD_DIM, n_rows), lambda i: (i, 0, 0, 0))],
        out_shape=[jax.ShapeDtypeStruct((b, KV_GROUPS, n_rows, LANES), BF16),
                   jax.ShapeDtypeStruct((b, KV_GROUPS, HEAD_DIM, n_rows), BF16)],
        compiler_params=pltpu.CompilerParams(
            dimension_semantics=("parallel",), vmem_limit_bytes=VMEM_LIMIT),
        name="compress",
    )(kcin, vcin, pek2, wk1b, wk2b, pev2, wv1b, wv2t)


def _compress_weights(pe, w1, w2, value_layout):
    nl, dh, hid = CMP_STRIDE, HEAD_DIM, CMP_HIDDEN
    w1r = w1.reshape(2, nl, dh, hid)
    z = jnp.zeros((nl, dh, hid), w1.dtype)
    row_g0 = jnp.concatenate([w1r[0], z, w1r[1], z], axis=-1)
    row_g1 = jnp.concatenate([z, w1r[0], z, w1r[1]], axis=-1)
    w1b = jnp.concatenate([row_g0, row_g1], axis=1).astype(BF16)
    pe2 = jnp.concatenate([pe, pe], axis=-1)
    zz = jnp.zeros((hid, dh), w2.dtype)
    if value_layout:
        w2b = jnp.stack([jnp.concatenate([w2, zz], axis=0).T,
                         jnp.concatenate([zz, w2], axis=0).T]).astype(BF16)
    else:
        top = jnp.concatenate([w2, zz, zz, zz], axis=-1)
        bot = jnp.concatenate([zz, zz, w2, zz], axis=-1)
        w2b = jnp.concatenate([top, bot], axis=0).astype(BF16)
    return pe2, w1b, w2b


def _nsa_tables(t):
    slopes = _slopes().reshape(KV_GROUPS, HEADS_PER_GROUP)
    n_rows = t // CMP_STRIDE
    n_blk = t // SEL_BLOCK
    i = np.arange(Q_BLOCK)
    c = np.arange(n_rows)
    d0c = (i[None, :] - (c[:, None] * CMP_STRIDE + CMP_BLOCK - 1)).astype(np.int32)
    tblc = -slopes[:, None, :, None] * d0c[None, :, None, :].astype(np.float32)
    tblc = tblc.reshape(KV_GROUPS, n_rows, GQ)
    d0c = np.tile(d0c, (1, HEADS_PER_GROUP))
    j = np.arange(WIN_KEYS)
    dist = i[None, :] + WINDOW - j[:, None]
    ok = (dist >= 0) & (dist < WINDOW)
    tblw = np.where(ok[None, :, None, :],
                    -slopes[:, None, :, None] * dist[None, :, None, :].astype(np.float32),
                    np.float32(NEG)).astype(np.float32).reshape(KV_GROUPS, WIN_KEYS, GQ)
    jj = np.arange(SEL_CHUNK)
    per = SEL_CHUNK // Q_BLOCK
    tailm = np.stack([np.where(jj[:, None] > i[None, :] + Q_BLOCK * r, np.float32(NEG), np.float32(0))
                      for r in range(per)]).astype(np.float32)
    ci = c[None, :] * CMP_STRIDE
    sj = np.arange(n_blk)[:, None] * SEL_BLOCK
    n_cmp = (t - CMP_BLOCK) // CMP_STRIDE + 1
    ov = ((ci <= sj + SEL_BLOCK - 1) & (ci + CMP_BLOCK - 1 >= sj) & (c[None, :] < n_cmp))
    ov = ov.astype(np.float32)
    pos = np.arange(t)
    ktag = np.zeros((t, 2 * LANES), np.float32)
    ktag[pos, HEAD_DIM + pos // SEL_BLOCK] = 1.0
    ktag[:, LANES] = pos % SEL_BLOCK
    qflag = np.zeros((HEAD_DIM, GQ), np.float32)
    qflag[0, :] = NEG
    qslope = np.zeros((KV_GROUPS, LANES, GQ), np.float32)
    qslope[:, 0, :] = np.repeat(slopes, Q_BLOCK, axis=1)
    return (jnp.asarray(tblc), jnp.asarray(d0c), jnp.asarray(tblw), jnp.asarray(tailm),
            jnp.asarray(ov, dtype=BF16), jnp.asarray(ktag, dtype=BF16),
            jnp.asarray(qflag, dtype=BF16), jnp.asarray(qslope, dtype=BF16))


def _nsa_kernel(qt_ref, gt_ref, ks_ref, kw_ref, vst_ref, vwt_ref, kc_ref, vct_ref,
                tblc_ref, d0c_ref, tblw_ref, tailm_ref, ov_ref, ktag_ref, qflag_ref, qslope_ref,
                o_ref,
                kaug_ref, kwaug_ref, m_ref, acc_ref, need_ref, *, t, n_blk):
    qi = pl.program_id(1)
    slopes = _slopes()
    hg = HEADS_PER_GROUP
    prep_rows = 512

    @pl.when(qi == 0)
    def _prep():
        low = lax.broadcasted_iota(jnp.int32, (prep_rows, LANES), 1) < HEAD_DIM
        flag = jnp.where(lax.broadcasted_iota(jnp.int32, (WINDOW, LANES), 1) == HEAD_DIM, 1.0, 0.0)
        for g in range(KV_GROUPS):
            kwaug_ref[g, 0:WINDOW, :] = flag.astype(BF16)

        def body(c, carry):
            r0 = pl.multiple_of(c * prep_rows, prep_rows)
            rows = pl.ds(r0, prep_rows)
            rows_w = pl.ds(r0 + WINDOW, prep_rows)
            tag = ktag_ref[rows, 0:LANES].astype(F32)
            ks = ks_ref[rows, :].astype(F32)
            kw = kw_ref[rows, :].astype(F32)
            ks_sw = pltpu.roll(ks, HEAD_DIM, axis=1)
            kw_sw = pltpu.roll(kw, HEAD_DIM, axis=1)
            for g in range(KV_GROUPS):
                kaug_ref[g, rows, 0:LANES] = jnp.where(low, ks if g == 0 else ks_sw, tag).astype(BF16)
                kaug_ref[g, rows, LANES:2 * LANES] = ktag_ref[rows, LANES:2 * LANES]
                kwaug_ref[g, rows_w, :] = jnp.where(low, kw if g == 0 else kw_sw, 0.0).astype(BF16)
            return carry

        lax.fori_loop(0, t // prep_rows, body, 0)

    gates = jax.nn.sigmoid(gt_ref[...])
    bi = lax.broadcasted_iota(jnp.int32, (n_blk, Q_BLOCK), 0)
    lq = lax.broadcasted_iota(jnp.int32, (n_blk, Q_BLOCK), 1)
    cur = 2 * qi + (lq >= SEL_BLOCK).astype(jnp.int32)
    valid = bi <= cur
    near = (bi == 0) | (bi >= cur - 1)
    rel_blk = (bi - cur).astype(F32)

    def with_ones(vt):
        first = lax.broadcasted_iota(jnp.int32, (ONES_ROWS, vt.shape[1]), 0) == 0
        return jnp.concatenate([vt, jnp.where(first, 1.0, 0.0).astype(BF16)], axis=0)

    def select(g):
        qt = jnp.concatenate([qt_ref[(g * hg + h) * HEAD_DIM:(g * hg + h + 1) * HEAD_DIM, :]
                              for h in range(hg)], axis=1)
        qw = jnp.concatenate([qt, qflag_ref[...]], axis=0)

        s = _dot(kc_ref[g], qw) + tblc_ref[g]
        s = jnp.where(d0c_ref[...] + Q_BLOCK * qi >= 0, s, NEG)
        m = jnp.max(s, axis=0, keepdims=True)
        m = jnp.where(m < 0.5 * NEG, 0.0, m)
        p = jnp.exp(s - m)
        l = jnp.sum(p, axis=0, keepdims=True)
        pn = p * (1.0 / jnp.maximum(l, 1e-30))
        o_cmp = _dot(vct_ref[g], pn.astype(BF16))
        psum = pn[:, 0:Q_BLOCK]
        for h in range(1, hg):
            psum = psum + pn[:, h * Q_BLOCK:(h + 1) * Q_BLOCK]

        p_hi = psum.astype(BF16)
        p_lo = (psum - p_hi.astype(F32)).astype(BF16)
        imp = _dot(ov_ref[...], p_hi) + _dot(ov_ref[...], p_lo)
        val = jnp.where(valid, jnp.where(near, 3e38, imp), -1.0)
        sel = jnp.zeros((n_blk, Q_BLOCK), F32)
        for _ in range(min(N_SEL, n_blk)):
            top = jnp.max(val, axis=0, keepdims=True)
            idx = jnp.min(jnp.where(val == top, bi, n_blk), axis=0, keepdims=True)
            pick = bi == idx
            sel = jnp.where(pick, 1.0, sel)
            val = jnp.where(pick, -2.0, val)
        chosen = jnp.where(valid, sel, 0.0) > 0.0

        bterm = jnp.concatenate(
            [jnp.where(chosen, float(slopes[g * hg + h]) * SEL_BLOCK * rel_blk, NEG) for h in range(hg)],
            axis=1).astype(BF16)
        parts = [qt, bterm]
        if n_blk < HEAD_DIM:
            parts.append(jnp.zeros((HEAD_DIM - n_blk, GQ), BF16))
        parts.append(qslope_ref[g])
        qaug = jnp.concatenate(parts, axis=0)
        return qw, o_cmp, qaug, jnp.where(chosen, 1.0, 0.0)

    def sel_scores(g, qaug, chunk, mask):
        rows = pl.ds(pl.multiple_of(chunk * SEL_CHUNK, SEL_CHUNK), SEL_CHUNK)
        s = _dot(kaug_ref[g, rows, :], qaug)
        return s if mask is None else s + mask

    def sel_update(g, s, chunk, carry):
        m, acc = carry
        m_new = jnp.maximum(m, jnp.max(s, axis=0, keepdims=True))
        alpha = jnp.exp(m - m_new)
        p = jnp.exp(s - m_new).astype(BF16)
        per_chunk = SEL_CHUNK // LANES
        vt = jnp.concatenate([vst_ref[chunk * per_chunk + k, g * HEAD_DIM:(g + 1) * HEAD_DIM, :]
                              for k in range(per_chunk)], axis=1)
        return m_new, acc * alpha + _dot(with_ones(vt), p)

    def sel_steps(chunk, mask, carries):
        scores = [sel_scores(g, qaug[g], chunk, mask) for g in groups]
        return tuple(sel_update(g, scores[g], chunk, carries[g]) for g in groups)

    def window(g, qw):
        rows = pl.ds(pl.multiple_of(qi * Q_BLOCK, Q_BLOCK), WIN_KEYS)
        s = _dot(kwaug_ref[g, rows, :], qw) + tblw_ref[g]
        m = jnp.max(s, axis=0, keepdims=True)
        p = jnp.exp(s - m).astype(BF16)
        back = WINDOW // LANES
        vt = jnp.concatenate([vwt_ref[jnp.maximum(qi - back + k, 0), g * HEAD_DIM:(g + 1) * HEAD_DIM, :]
                              for k in range(back + 1)], axis=1)
        acc = _dot(with_ones(vt), p)
        return acc[0:HEAD_DIM] * (1.0 / acc[HEAD_DIM:HEAD_DIM + 1])

    groups = tuple(range(KV_GROUPS))
    qw, o_cmp, qaug, picked = zip(*[select(g) for g in groups])

    per = SEL_CHUNK // Q_BLOCK
    blk_per_chunk = SEL_CHUNK // SEL_BLOCK
    n_full = qi // per
    wanted = picked[0]
    for g in groups[1:]:
        wanted = wanted + picked[g]
    for c in range(n_blk // blk_per_chunk - 1):
        hit = jnp.sum(wanted[c * blk_per_chunk:(c + 1) * blk_per_chunk, :])
        need_ref[c] = (hit > 0.0).astype(jnp.int32)
    for g in groups:
        m_ref[g] = jnp.full((8, GQ), NEG, F32)
        acc_ref[g] = jnp.zeros((HEAD_DIM + ONES_ROWS, GQ), F32)

    def load_carries():
        return tuple((m_ref[g, 0:1, :], acc_ref[g]) for g in groups)

    def full_chunk(c, carry):
        @pl.when(need_ref[c] > 0)
        def _():
            for g, (m, acc) in enumerate(sel_steps(c, None, load_carries())):
                m_ref[g, 0:1, :] = m
                acc_ref[g] = acc
        return carry

    lax.fori_loop(0, n_full, full_chunk, 0)
    tail_mask = jnp.concatenate([tailm_ref[qi % per]] * hg, axis=1)
    carries = sel_steps(n_full, tail_mask, load_carries())
    o_slc = [acc[0:HEAD_DIM] * (1.0 / acc[HEAD_DIM:HEAD_DIM + 1]) for _, acc in carries]

    o_win = [window(g, qw[g]) for g in groups]

    for g in groups:
        for pr in range(hg // 2):
            outs = []
            for h in (2 * pr, 2 * pr + 1):
                hh = g * hg + h
                lanes = slice(h * Q_BLOCK, (h + 1) * Q_BLOCK)
                out = gates[3 * hh:3 * hh + 1, :] * o_cmp[g][:, lanes]
                out = out + gates[3 * hh + 1:3 * hh + 2, :] * o_slc[g][:, lanes]
                out = out + gates[3 * hh + 2:3 * hh + 3, :] * o_win[g][:, lanes]
                outs.append(out)
            pair = jnp.concatenate(outs, axis=0)
            o_ref[:, (g * 2 + pr) * LANES:(g * 2 + pr + 1) * LANES] = pair.T.astype(BF16)


def _nsa(qt, gt, ks, kw, vst, vwt, kc, vct, tables):
    b, _, t = qt.shape
    n_blk = t // SEL_BLOCK
    n_rows = t // CMP_STRIDE
    assert t % SEL_CHUNK == 0 and N_SEL <= n_blk <= HEAD_DIM and n_blk % 16 == 0
    seq_spec = pl.BlockSpec((None, t, LANES), lambda i, j: (i, 0, 0))
    chunked = pl.BlockSpec((None, t // LANES, KV_WIDTH, LANES), lambda i, j: (i, 0, 0, 0))
    return pl.pallas_call(
        functools.partial(_nsa_kernel, t=t, n_blk=n_blk),
        grid=(b, t // Q_BLOCK),
        in_specs=[pl.BlockSpec((None, NSA_WIDTH, Q_BLOCK), lambda i, j: (i, 0, j)),
                  pl.BlockSpec((None, GATE_ROWS, Q_BLOCK), lambda i, j: (i, 0, j)),
                  seq_spec, seq_spec, chunked, chunked,
                  pl.BlockSpec((None, KV_GROUPS, n_rows, LANES), lambda i, j: (i, 0, 0, 0)),
                  pl.BlockSpec((None, KV_GROUPS, HEAD_DIM, n_rows), lambda i, j: (i, 0, 0, 0))]
                 + [_const_spec(a.shape) for a in tables],
        out_specs=pl.BlockSpec((None, Q_BLOCK, NSA_WIDTH), lambda i, j: (i, j, 0)),
        out_shape=jax.ShapeDtypeStruct((b, t, NSA_WIDTH), BF16),
        scratch_shapes=[pltpu.VMEM((KV_GROUPS, t, 2 * LANES), BF16),
                        pltpu.VMEM((KV_GROUPS, t + WINDOW, LANES), BF16),
                        pltpu.VMEM((KV_GROUPS, 8, GQ), F32),
                        pltpu.VMEM((KV_GROUPS, HEAD_DIM + ONES_ROWS, GQ), F32),
                        pltpu.SMEM((t // SEL_CHUNK,), jnp.int32)],
        compiler_params=pltpu.CompilerParams(
            dimension_semantics=("arbitrary", "arbitrary"), vmem_limit_bytes=VMEM_LIMIT),
        name="nsa",
    )(qt, gt, ks, kw, vst, vwt, kc, vct, *tables)


def _mix_out_kernel(x_ref, o_ref, u_ref, halo_ref, wp_ref, sc_ref, wo_ref, y_ref, mix_ref, *, tm):
    ti = pl.program_id(1)
    mix_ref[:, 0:NSA_WIDTH] = o_ref[...]
    halo = jnp.where(ti > 0, halo_ref[...], 0.0)
    pos = ti * tm + lax.broadcasted_iota(jnp.int32, (tm, 1), 0)
    for gi, w in enumerate(POOL_WIDTHS):
        lanes = slice(gi * POOL_GROUP, (gi + 1) * POOL_GROUP)
        u = u_ref[:, lanes]
        ext = jnp.concatenate([halo[:, lanes], u], axis=0)
        run = ext
        step = 1
        while step < w:
            run = run + pltpu.roll(run, step, axis=0)
            step *= 2
        cnt = jnp.minimum(pos + 1, w).astype(F32)
        y = run[POOL_HALO:, :] / cnt - u
        yo = _dot(y.astype(BF16), wp_ref[gi]) * sc_ref[:, lanes]
        mix_ref[:, NSA_WIDTH + gi * POOL_GROUP:NSA_WIDTH + (gi + 1) * POOL_GROUP] = yo.astype(BF16)
    y_ref[...] = x_ref[...] + _dot(mix_ref[...], wo_ref[...])


def _mix_out(x3, o_nsa, u, wp, sc, wo, tm=512):
    b, t, d = x3.shape
    per = tm // POOL_HALO
    return pl.pallas_call(
        functools.partial(_mix_out_kernel, tm=tm),
        grid=(b, t // tm),
        in_specs=[pl.BlockSpec((None, tm, d), lambda i, j: (i, j, 0)),
                  pl.BlockSpec((None, tm, NSA_WIDTH), lambda i, j: (i, j, 0)),
                  pl.BlockSpec((None, tm, POOL_CH), lambda i, j: (i, j, 0)),
                  pl.BlockSpec((None, POOL_HALO, POOL_CH), lambda i, j: (i, jnp.maximum(j * per - 1, 0), 0)),
                  _const_spec(wp.shape), _const_spec(sc.shape), _const_spec(wo.shape)],
        out_specs=pl.BlockSpec((None, tm, d), lambda i, j: (i, j, 0)),
        out_shape=jax.ShapeDtypeStruct((b, t, d), F32),
        scratch_shapes=[pltpu.VMEM((tm, NSA_WIDTH + POOL_CH), BF16)],
        compiler_params=pltpu.CompilerParams(
            dimension_semantics=("parallel", "parallel"), vmem_limit_bytes=VMEM_LIMIT),
        name="mix_out",
    )(x3, o_nsa, u, u, wp, sc, wo)


def _chunk_cols(w):
    d, f = w.shape
    return w.reshape(d, f // FF_CHUNK, FF_CHUNK).transpose(1, 0, 2).astype(BF16)


def _chunk_rows(w):
    f, d = w.shape
    return w.reshape(f // FF_CHUNK, FF_CHUNK, d).astype(BF16)


def kernel(x, ffn1_norm, ffn1_wg, ffn1_wu, ffn1_wd, mix_norm, w_in, cmp_pe_k, cmp_wk1, cmp_wk2,
           cmp_pe_v, cmp_wv1, cmp_wv2, pool_w, pool_scale, w_out, ffn2_norm, ffn2_wg, ffn2_wu,
           ffn2_wd, final_norm):
    b, t, d = x.shape
    depth = ffn1_norm.shape[0]
    n = b * t
    tables = _nsa_tables(t)
    x2 = x.reshape(n, d)
    for l in range(depth):
        x2 = _ffn(x2, ffn1_norm[l][None], _chunk_cols(ffn1_wg[l]), _chunk_cols(ffn1_wu[l]),
                  _chunk_rows(ffn1_wd[l]))
        x3 = x2.reshape(b, t, d)
        kcin, vcin, ks, kw, u, qt, vst, vwt, gt = _proj(x3, mix_norm[l][None], *_split_w_in(w_in[l]))
        kc, vct = _compress(kcin, vcin,
                            *_compress_weights(cmp_pe_k[l], cmp_wk1[l], cmp_wk2[l], False),
                            *_compress_weights(cmp_pe_v[l], cmp_wv1[l], cmp_wv2[l], True))
        o_nsa = _nsa(qt, gt, ks, kw, vst, vwt, kc, vct, tables)
        x2 = _mix_out(x3, o_nsa, u, pool_w[l].astype(BF16), pool_scale[l][None],
                      w_out[l].astype(BF16)).reshape(n, d)
        x2 = _ffn(x2, ffn2_norm[l][None], _chunk_cols(ffn2_wg[l]), _chunk_cols(ffn2_wu[l]),
                  _chunk_rows(ffn2_wd[l]), final_g=final_norm[None] if l == depth - 1 else None)
    return x2.reshape(b, t, d)
```

```python
import functools

import numpy as np
import jax
import jax.numpy as jnp
from jax import lax
from jax.experimental import pallas as pl
from jax.experimental.pallas import tpu as pltpu

HEAD_DIM = 64
NSA_HEADS = 8
KV_GROUPS = 2
HEADS_PER_GROUP = NSA_HEADS // KV_GROUPS
NSA_WIDTH = NSA_HEADS * HEAD_DIM
KV_WIDTH = KV_GROUPS * HEAD_DIM
CMP_BLOCK = 32
CMP_STRIDE = 16
CMP_HIDDEN = 128
SEL_BLOCK = 64
N_SEL = 8
WINDOW = 512
Q_BLOCK = 128
POOL_WIDTHS = (2, 4, 8, 16)
POOL_GROUP = 128
POOL_CH = POOL_GROUP * len(POOL_WIDTHS)
N_GATES = 3 * NSA_HEADS
GATE_ROWS = 32
EPS = 1e-6

LANES = 128
GQ = HEADS_PER_GROUP * Q_BLOCK
SEL_CHUNK = 256
WIN_KEYS = WINDOW + Q_BLOCK
ONES_ROWS = 16
POOL_HALO = 16
FF_CHUNK = 256
NEG = -1e30
VMEM_LIMIT = 56 * 1024 * 1024

F32 = jnp.float32
BF16 = jnp.bfloat16


def _slopes():
    n = NSA_HEADS
    return (2.0 ** (-8.0 * np.arange(1, n + 1) / n)).astype(np.float32)


def _dot(a, b):
    return jnp.dot(a, b, preferred_element_type=F32)


def _dot_nt(a, b):
    return lax.dot_general(a, b, (((1,), (1,)), ((), ())), preferred_element_type=F32)


def _rms(x, g):
    return x * lax.rsqrt(jnp.mean(x * x, axis=-1, keepdims=True) + EPS) * g


def _const_spec(shape):
    nd = len(shape)
    return pl.BlockSpec(shape, lambda *_: (0,) * nd)


def _ffn_kernel(x_ref, g_ref, wg_ref, wu_ref, wd_ref, *rest, n_chunks, final):
    if final:
        fn_ref, o_ref, h_ref, acc_ref = rest
    else:
        o_ref, h_ref, acc_ref = rest
    h_ref[...] = _rms(x_ref[...], g_ref[...]).astype(BF16)
    acc_ref[...] = jnp.zeros_like(acc_ref)

    def body(c, carry):
        h = h_ref[...]
        gate = _dot(h, wg_ref[c])
        up = _dot(h, wu_ref[c])
        a = (gate * jax.nn.sigmoid(gate) * up).astype(BF16)
        acc_ref[...] += _dot(a, wd_ref[c])
        return carry

    lax.fori_loop(0, n_chunks, body, 0, unroll=True)
    y = x_ref[...] + 0.5 * acc_ref[...]
    if final:
        y = _rms(y, fn_ref[...])
    o_ref[...] = y


def _ffn(x2, g, wg, wu, wd, final_g=None, tm=512):
    n, d = x2.shape
    n_chunks = wg.shape[0]
    final = final_g is not None
    in_specs = [
        pl.BlockSpec((tm, d), lambda i: (i, 0)),
        _const_spec((1, d)),
        _const_spec(wg.shape),
        _const_spec(wu.shape),
        _const_spec(wd.shape),
    ]
    args = [x2, g, wg, wu, wd]
    if final:
        in_specs.append(_const_spec((1, d)))
        args.append(final_g)
    return pl.pallas_call(
        functools.partial(_ffn_kernel, n_chunks=n_chunks, final=final),
        grid=(n // tm,),
        in_specs=in_specs,
        out_specs=pl.BlockSpec((tm, d), lambda i: (i, 0)),
        out_shape=jax.ShapeDtypeStruct((n, d), F32),
        scratch_shapes=[pltpu.VMEM((tm, d), BF16), pltpu.VMEM((tm, d), F32)],
        compiler_params=pltpu.CompilerParams(
            dimension_semantics=("parallel",), vmem_limit_bytes=VMEM_LIMIT),
        name="ffn_final" if final else "ffn",
    )(*args)


_TOK_KC = (0, 128)
_TOK_VC = (128, 256)
_TOK_KS = (256, 384)
_TOK_KW = (384, 512)
_TOK_U = (512, 1024)
_FEAT_Q = (0, 512)
_FEAT_VS = (512, 640)
_FEAT_VW = (640, 768)
_FEAT_GT = (768, 768 + GATE_ROWS)


def _proj_kernel(x_ref, g_ref, wt_ref, wf_ref, kcin_ref, vcin_ref, ks_ref, kw_ref, u_ref,
                 qt_ref, vst_ref, vwt_ref, gt_ref, *, tm):
    h = _rms(x_ref[...], g_ref[...]).astype(BF16)

    def tok(lo_hi):
        return _dot(h, wt_ref[:, lo_hi[0]:lo_hi[1]])

    kcin_ref[...] = tok(_TOK_KC)
    vcin_ref[...] = tok(_TOK_VC)
    ks_ref[...] = tok(_TOK_KS).astype(BF16)
    kw_ref[...] = tok(_TOK_KW).astype(BF16)
    u_ref[...] = tok(_TOK_U)
    zt = _dot_nt(wf_ref[...], h)
    qt_ref[...] = (zt[_FEAT_Q[0]:_FEAT_Q[1]] * (HEAD_DIM ** -0.5)).astype(BF16)
    gt_ref[...] = zt[_FEAT_GT[0]:_FEAT_GT[1]]
    for c in range(tm // LANES):
        cols = slice(c * LANES, (c + 1) * LANES)
        vst_ref[c] = zt[_FEAT_VS[0]:_FEAT_VS[1], cols].astype(BF16)
        vwt_ref[c] = zt[_FEAT_VW[0]:_FEAT_VW[1], cols].astype(BF16)


def _proj(x3, g, w_tok, w_feat, tm=512):
    b, t, d = x3.shape
    nck = tm // LANES
    tok = lambda wd: pl.BlockSpec((None, tm, wd), lambda i, j: (i, j, 0))
    chunked = pl.BlockSpec((None, nck, KV_WIDTH, LANES), lambda i, j: (i, j, 0, 0))
    return pl.pallas_call(
        functools.partial(_proj_kernel, tm=tm),
        grid=(b, t // tm),
        in_specs=[tok(d), _const_spec((1, d)), _const_spec(w_tok.shape), _const_spec(w_feat.shape)],
        out_specs=[tok(KV_WIDTH), tok(KV_WIDTH), tok(KV_WIDTH), tok(KV_WIDTH), tok(POOL_CH),
                   pl.BlockSpec((None, NSA_WIDTH, tm), lambda i, j: (i, 0, j)),
                   chunked, chunked,
                   pl.BlockSpec((None, GATE_ROWS, tm), lambda i, j: (i, 0, j))],
        out_shape=[jax.ShapeDtypeStruct((b, t, KV_WIDTH), F32),
                   jax.ShapeDtypeStruct((b, t, KV_WIDTH), F32),
                   jax.ShapeDtypeStruct((b, t, KV_WIDTH), BF16),
                   jax.ShapeDtypeStruct((b, t, KV_WIDTH), BF16),
                   jax.ShapeDtypeStruct((b, t, POOL_CH), F32),
                   jax.ShapeDtypeStruct((b, NSA_WIDTH, t), BF16),
                   jax.ShapeDtypeStruct((b, t // LANES, KV_WIDTH, LANES), BF16),
                   jax.ShapeDtypeStruct((b, t // LANES, KV_WIDTH, LANES), BF16),
                   jax.ShapeDtypeStruct((b, GATE_ROWS, t), F32)],
        compiler_params=pltpu.CompilerParams(
            dimension_semantics=("parallel", "parallel"), vmem_limit_bytes=VMEM_LIMIT),
        name="proj",
    )(x3, g, w_tok, w_feat)


def _split_w_in(w):
    q0, kv0 = 0, NSA_WIDTH
    col = lambda i: slice(kv0 + i * KV_WIDTH, kv0 + (i + 1) * KV_WIDTH)
    g0 = kv0 + 6 * KV_WIDTH
    w_tok = jnp.concatenate([w[:, col(0)], w[:, col(1)], w[:, col(2)], w[:, col(4)],
                             w[:, g0 + N_GATES:]], axis=1).astype(BF16)
    pad = jnp.zeros((w.shape[0], GATE_ROWS - N_GATES), w.dtype)
    w_feat = jnp.concatenate([w[:, q0:NSA_WIDTH], w[:, col(3)], w[:, col(5)],
                              w[:, g0:g0 + N_GATES], pad], axis=1).T.astype(BF16)
    return w_tok, w_feat


def _gelu_tanh(x):
    c = np.float32(np.sqrt(2.0 / np.pi))
    return 0.5 * x * (1.0 + jnp.tanh(c * (x + 0.044715 * (x * x * x))))


def _compress_kernel(kcin_ref, vcin_ref, pek_ref, wk1_ref, wk2_ref, pev_ref, wv1_ref, wv2_ref,
                     kc_ref, vct_ref, *, n_rows):
    nl = CMP_STRIDE
    hid2 = KV_GROUPS * CMP_HIDDEN

    def hidden(src_ref, pe_ref, w1_ref):
        acc = jnp.zeros((n_rows + 16, 2 * hid2), F32)
        for l in range(nl):
            xl = src_ref[pl.ds(l, n_rows, stride=nl), :]
            pa = jnp.broadcast_to(pe_ref[l:l + 1, :], (8, KV_WIDTH))
            pb = jnp.broadcast_to(pe_ref[nl + l:nl + l + 1, :], (8, KV_WIDTH))
            lhs = jnp.concatenate([xl, pa, pb], axis=0).astype(BF16)
            acc = acc + _dot(lhs, w1_ref[l])
        sa = acc[0:n_rows, 0:hid2]
        sb = acc[0:n_rows, hid2:2 * hid2]
        bias = acc[n_rows:n_rows + 1, 0:hid2] + acc[n_rows + 8:n_rows + 9, hid2:2 * hid2]
        hcur = sa + pltpu.roll(sb, n_rows - 1, axis=0) + bias
        return _gelu_tanh(hcur).astype(BF16)

    hk = hidden(kcin_ref, pek_ref, wk1_ref)
    kc = _dot(hk, wk2_ref[...])
    for g in range(KV_GROUPS):
        kc_ref[g] = kc[:, g * LANES:(g + 1) * LANES].astype(BF16)
    hv = hidden(vcin_ref, pev_ref, wv1_ref)
    for g in range(KV_GROUPS):
        vct_ref[g] = _dot_nt(wv2_ref[g], hv).astype(BF16)


def _compress(kcin, vcin, pek2, wk1b, wk2b, pev2, wv1b, wv2t):
    b, t, _ = kcin.shape
    n_rows = t // CMP_STRIDE
    seq_spec = pl.BlockSpec((None, t, KV_WIDTH), lambda i: (i, 0, 0))
    return pl.pallas_call(
        functools.partial(_compress_kernel, n_rows=n_rows),
        grid=(b,),
        in_specs=[seq_spec, seq_spec,
                  _const_spec(pek2.shape), _const_spec(wk1b.shape), _const_spec(wk2b.shape),
                  _const_spec(pev2.shape), _const_spec(wv1b.shape), _const_spec(wv2t.shape)],
        out_specs=[pl.BlockSpec((None, KV_GROUPS, n_rows, LANES), lambda i: (i, 0, 0, 0)),
                   pl.BlockSpec((None, KV_GROUPS, HEAD_DIM, n_rows), lambda i: (i, 0, 0, 0))],
        out_shape=[jax.ShapeDtypeStruct((b, KV_GROUPS, n_rows, LANES), BF16),
                   jax.ShapeDtypeStruct((b, KV_GROUPS, HEAD_DIM, n_rows), BF16)],
        compiler_params=pltpu.CompilerParams(
            dimension_semantics=("parallel",), vmem_limit_bytes=VMEM_LIMIT),
        name="compress",
    )(kcin, vcin, pek2, wk1b, wk2b, pev2, wv1b, wv2t)


def _compress_weights(pe, w1, w2, value_layout):
    nl, dh, hid = CMP_STRIDE, HEAD_DIM, CMP_HIDDEN
    w1r = w1.reshape(2, nl, dh, hid)
    z = jnp.zeros((nl, dh, hid), w1.dtype)
    row_g0 = jnp.concatenate([w1r[0], z, w1r[1], z], axis=-1)
    row_g1 = jnp.concatenate([z, w1r[0], z, w1r[1]], axis=-1)
    w1b = jnp.concatenate([row_g0, row_g1], axis=1).astype(BF16)
    pe2 = jnp.concatenate([pe, pe], axis=-1)
    zz = jnp.zeros((hid, dh), w2.dtype)
    if value_layout:
        w2b = jnp.stack([jnp.concatenate([w2, zz], axis=0).T,
                         jnp.concatenate([zz, w2], axis=0).T]).astype(BF16)
    else:
        top = jnp.concatenate([w2, zz, zz, zz], axis=-1)
        bot = jnp.concatenate([zz, zz, w2, zz], axis=-1)
        w2b = jnp.concatenate([top, bot], axis=0).astype(BF16)
    return pe2, w1b, w2b


def _nsa_tables(t):
    slopes = _slopes().reshape(KV_GROUPS, HEADS_PER_GROUP)
    n_rows = t // CMP_STRIDE
    n_blk = t // SEL_BLOCK
    i = np.arange(Q_BLOCK)
    c = np.arange(n_rows)
    d0c = (i[None, :] - (c[:, None] * CMP_STRIDE + CMP_BLOCK - 1)).astype(np.int32)
    tblc = -slopes[:, None, :, None] * d0c[None, :, None, :].astype(np.float32)
    tblc = tblc.reshape(KV_GROUPS, n_rows, GQ)
    d0c = np.tile(d0c, (1, HEADS_PER_GROUP))
    j = np.arange(WIN_KEYS)
    dist = i[None, :] + WINDOW - j[:, None]
    ok = (dist >= 0) & (dist < WINDOW)
    tblw = np.where(ok[None, :, None, :],
                    -slopes[:, None, :, None] * dist[None, :, None, :].astype(np.float32),
                    np.float32(NEG)).astype(np.float32).reshape(KV_GROUPS, WIN_KEYS, GQ)
    jj = np.arange(SEL_CHUNK)
    per = SEL_CHUNK // Q_BLOCK
    tailm = np.stack([np.where(jj[:, None] > i[None, :] + Q_BLOCK * r, np.float32(NEG), np.float32(0))
                      for r in range(per)]).astype(np.float32)
    ci = c[None, :] * CMP_STRIDE
    sj = np.arange(n_blk)[:, None] * SEL_BLOCK
    n_cmp = (t - CMP_BLOCK) // CMP_STRIDE + 1
    ov = ((ci <= sj + SEL_BLOCK - 1) & (ci + CMP_BLOCK - 1 >= sj) & (c[None, :] < n_cmp))
    ov = ov.astype(np.float32)
    pos = np.arange(t)
    ktag = np.zeros((t, 2 * LANES), np.float32)
    ktag[pos, HEAD_DIM + pos // SEL_BLOCK] = 1.0
    ktag[:, LANES] = pos % SEL_BLOCK
    qflag = np.zeros((HEAD_DIM, GQ), np.float32)
    qflag[0, :] = NEG
    qslope = np.zeros((KV_GROUPS, LANES, GQ), np.float32)
    qslope[:, 0, :] = np.repeat(slopes, Q_BLOCK, axis=1)
    return (jnp.asarray(tblc), jnp.asarray(d0c), jnp.asarray(tblw), jnp.asarray(tailm),
            jnp.asarray(ov, dtype=BF16), jnp.asarray(ktag, dtype=BF16),
            jnp.asarray(qflag, dtype=BF16), jnp.asarray(qslope, dtype=BF16))


def _nsa_kernel(qt_ref, gt_ref, ks_ref, kw_ref, vst_ref, vwt_ref, kc_ref, vct_ref,
                tblc_ref, d0c_ref, tblw_ref, tailm_ref, ov_ref, ktag_ref, qflag_ref, qslope_ref,
                o_ref,
                kaug_ref, kwaug_ref, m_ref, acc_ref, need_ref, *, t, n_blk):
    qi = pl.program_id(1)
    slopes = _slopes()
    hg = HEADS_PER_GROUP
    prep_rows = 512

    @pl.when(qi == 0)
    def _prep():
        low = lax.broadcasted_iota(jnp.int32, (prep_rows, LANES), 1) < HEAD_DIM
        flag = jnp.where(lax.broadcasted_iota(jnp.int32, (WINDOW, LANES), 1) == HEAD_DIM, 1.0, 0.0)
        for g in range(KV_GROUPS):
            kwaug_ref[g, 0:WINDOW, :] = flag.astype(BF16)

        def body(c, carry):
            r0 = pl.multiple_of(c * prep_rows, prep_rows)
            rows = pl.ds(r0, prep_rows)
            rows_w = pl.ds(r0 + WINDOW, prep_rows)
            tag = ktag_ref[rows, 0:LANES].astype(F32)
            ks = ks_ref[rows, :].astype(F32)
            kw = kw_ref[rows, :].astype(F32)
            ks_sw = pltpu.roll(ks, HEAD_DIM, axis=1)
            kw_sw = pltpu.roll(kw, HEAD_DIM, axis=1)
            for g in range(KV_GROUPS):
                kaug_ref[g, rows, 0:LANES] = jnp.where(low, ks if g == 0 else ks_sw, tag).astype(BF16)
                kaug_ref[g, rows, LANES:2 * LANES] = ktag_ref[rows, LANES:2 * LANES]
                kwaug_ref[g, rows_w, :] = jnp.where(low, kw if g == 0 else kw_sw, 0.0).astype(BF16)
            return carry

        lax.fori_loop(0, t // prep_rows, body, 0)

    gates = jax.nn.sigmoid(gt_ref[...])
    bi = lax.broadcasted_iota(jnp.int32, (n_blk, Q_BLOCK), 0)
    lq = lax.broadcasted_iota(jnp.int32, (n_blk, Q_BLOCK), 1)
    cur = 2 * qi + (lq >= SEL_BLOCK).astype(jnp.int32)
    valid = bi <= cur
    near = (bi == 0) | (bi >= cur - 1)
    rel_blk = (bi - cur).astype(F32)

    def with_ones(vt):
        first = lax.broadcasted_iota(jnp.int32, (ONES_ROWS, vt.shape[1]), 0) == 0
        return jnp.concatenate([vt, jnp.where(first, 1.0, 0.0).astype(BF16)], axis=0)

    def queries(g):
        qt = jnp.concatenate([qt_ref[(g * hg + h) * HEAD_DIM:(g * hg + h + 1) * HEAD_DIM, :]
                              for h in range(hg)], axis=1)
        return qt, jnp.concatenate([qt, qflag_ref[...]], axis=0)

    def select(g, qt, s):
        s = s + tblc_ref[g]
        s = jnp.where(d0c_ref[...] + Q_BLOCK * qi >= 0, s, NEG)
        m = jnp.max(s, axis=0, keepdims=True)
        m = jnp.where(m < 0.5 * NEG, 0.0, m)
        p = jnp.exp(s - m)
        l = jnp.sum(p, axis=0, keepdims=True)
        pn = p * (1.0 / jnp.maximum(l, 1e-30))
        o_cmp = _dot(vct_ref[g], pn.astype(BF16))
        psum = pn[:, 0:Q_BLOCK]
        for h in range(1, hg):
            psum = psum + pn[:, h * Q_BLOCK:(h + 1) * Q_BLOCK]

        p_hi = psum.astype(BF16)
        p_lo = (psum - p_hi.astype(F32)).astype(BF16)
        imp = _dot(ov_ref[...], p_hi) + _dot(ov_ref[...], p_lo)
        val = jnp.where(valid, jnp.where(near, 3e38, imp), -1.0)
        sel = jnp.zeros((n_blk, Q_BLOCK), F32)
        for _ in range(min(N_SEL, n_blk)):
            top = jnp.max(val, axis=0, keepdims=True)
            idx = jnp.min(jnp.where(val == top, bi, n_blk), axis=0, keepdims=True)
            pick = bi == idx
            sel = jnp.where(pick, 1.0, sel)
            val = jnp.where(pick, -2.0, val)
        chosen = jnp.where(valid, sel, 0.0) > 0.0

        bterm = jnp.concatenate(
            [jnp.where(chosen, float(slopes[g * hg + h]) * SEL_BLOCK * rel_blk, NEG) for h in range(hg)],
            axis=1).astype(BF16)
        parts = [qt, bterm]
        if n_blk < HEAD_DIM:
            parts.append(jnp.zeros((HEAD_DIM - n_blk, GQ), BF16))
        parts.append(qslope_ref[g])
        qaug = jnp.concatenate(parts, axis=0)
        return o_cmp, qaug, jnp.where(chosen, 1.0, 0.0)

    def sel_scores(g, qaug, chunk, mask):
        rows = pl.ds(pl.multiple_of(chunk * SEL_CHUNK, SEL_CHUNK), SEL_CHUNK)
        s = _dot(kaug_ref[g, rows, :], qaug)
        return s if mask is None else s + mask

    def sel_update(g, s, chunk, carry):
        m, acc = carry
        m_new = jnp.maximum(m, jnp.max(s, axis=0, keepdims=True))
        alpha = jnp.exp(m - m_new)
        p = jnp.exp(s - m_new).astype(BF16)
        per_chunk = SEL_CHUNK // LANES
        vt = jnp.concatenate([vst_ref[chunk * per_chunk + k, g * HEAD_DIM:(g + 1) * HEAD_DIM, :]
                              for k in range(per_chunk)], axis=1)
        return m_new, acc * alpha + _dot(with_ones(vt), p)

    def sel_steps(chunk, mask, carries):
        scores = [sel_scores(g, qaug[g], chunk, mask) for g in groups]
        return tuple(sel_update(g, scores[g], chunk, carries[g]) for g in groups)

    def window_scores(g, qw):
        rows = pl.ds(pl.multiple_of(qi * Q_BLOCK, Q_BLOCK), WIN_KEYS)
        return _dot(kwaug_ref[g, rows, :], qw)

    def window(g, s):
        s = s + tblw_ref[g]
        m = jnp.max(s, axis=0, keepdims=True)
        p = jnp.exp(s - m).astype(BF16)
        back = WINDOW // LANES
        vt = jnp.concatenate([vwt_ref[jnp.maximum(qi - back + k, 0), g * HEAD_DIM:(g + 1) * HEAD_DIM, :]
                              for k in range(back + 1)], axis=1)
        acc = _dot(with_ones(vt), p)
        return acc[0:HEAD_DIM] * (1.0 / acc[HEAD_DIM:HEAD_DIM + 1])

    groups = tuple(range(KV_GROUPS))
    qt, qw = zip(*[queries(g) for g in groups])
    s_cmp = [_dot(kc_ref[g], qw[g]) for g in groups]
    s_win = [window_scores(g, qw[g]) for g in groups]
    o_cmp, qaug, picked = zip(*[select(g, qt[g], s_cmp[g]) for g in groups])
    o_win = [window(g, s_win[g]) for g in groups]

    per = SEL_CHUNK // Q_BLOCK
    blk_per_chunk = SEL_CHUNK // SEL_BLOCK
    n_full = qi // per
    wanted = picked[0]
    for g in groups[1:]:
        wanted = wanted + picked[g]
    for c in range(n_blk // blk_per_chunk - 1):
        hit = jnp.sum(wanted[c * blk_per_chunk:(c + 1) * blk_per_chunk, :])
        need_ref[c] = (hit > 0.0).astype(jnp.int32)
    for g in groups:
        m_ref[g] = jnp.full((8, GQ), NEG, F32)
        acc_ref[g] = jnp.zeros((HEAD_DIM + ONES_ROWS, GQ), F32)

    def load_carries():
        return tuple((m_ref[g, 0:1, :], acc_ref[g]) for g in groups)

    def full_chunk(c, carry):
        @pl.when(need_ref[c] > 0)
        def _():
            for g, (m, acc) in enumerate(sel_steps(c, None, load_carries())):
                m_ref[g, 0:1, :] = m
                acc_ref[g] = acc
        return carry

    lax.fori_loop(0, n_full, full_chunk, 0)
    tail_mask = jnp.concatenate([tailm_ref[qi % per]] * hg, axis=1)
    carries = sel_steps(n_full, tail_mask, load_carries())
    o_slc = [acc[0:HEAD_DIM] * (1.0 / acc[HEAD_DIM:HEAD_DIM + 1]) for _, acc in carries]

    for g in groups:
        for pr in range(hg // 2):
            outs = []
            for h in (2 * pr, 2 * pr + 1):
                hh = g * hg + h
                lanes = slice(h * Q_BLOCK, (h + 1) * Q_BLOCK)
                out = gates[3 * hh:3 * hh + 1, :] * o_cmp[g][:, lanes]
                out = out + gates[3 * hh + 1:3 * hh + 2, :] * o_slc[g][:, lanes]
                out = out + gates[3 * hh + 2:3 * hh + 3, :] * o_win[g][:, lanes]
                outs.append(out)
            pair = jnp.concatenate(outs, axis=0)
            o_ref[:, (g * 2 + pr) * LANES:(g * 2 + pr + 1) * LANES] = pair.T.astype(BF16)


def _nsa(qt, gt, ks, kw, vst, vwt, kc, vct, tables):
    b, _, t = qt.shape
    n_blk = t // SEL_BLOCK
    n_rows = t // CMP_STRIDE
    assert t % SEL_CHUNK == 0 and N_SEL <= n_blk <= HEAD_DIM and n_blk % 16 == 0
    seq_spec = pl.BlockSpec((None, t, LANES), lambda i, j: (i, 0, 0))
    chunked = pl.BlockSpec((None, t // LANES, KV_WIDTH, LANES), lambda i, j: (i, 0, 0, 0))
    return pl.pallas_call(
        functools.partial(_nsa_kernel, t=t, n_blk=n_blk),
        grid=(b, t // Q_BLOCK),
        in_specs=[pl.BlockSpec((None, NSA_WIDTH, Q_BLOCK), lambda i, j: (i, 0, j)),
                  pl.BlockSpec((None, GATE_ROWS, Q_BLOCK), lambda i, j: (i, 0, j)),
                  seq_spec, seq_spec, chunked, chunked,
                  pl.BlockSpec((None, KV_GROUPS, n_rows, LANES), lambda i, j: (i, 0, 0, 0)),
                  pl.BlockSpec((None, KV_GROUPS, HEAD_DIM, n_rows), lambda i, j: (i, 0, 0, 0))]
                 + [_const_spec(a.shape) for a in tables],
        out_specs=pl.BlockSpec((None, Q_BLOCK, NSA_WIDTH), lambda i, j: (i, j, 0)),
        out_shape=jax.ShapeDtypeStruct((b, t, NSA_WIDTH), BF16),
        scratch_shapes=[pltpu.VMEM((KV_GROUPS, t, 2 * LANES), BF16),
                        pltpu.VMEM((KV_GROUPS, t + WINDOW, LANES), BF16),
                        pltpu.VMEM((KV_GROUPS, 8, GQ), F32),
                        pltpu.VMEM((KV_GROUPS, HEAD_DIM + ONES_ROWS, GQ), F32),
                        pltpu.SMEM((t // SEL_CHUNK,), jnp.int32)],
        compiler_params=pltpu.CompilerParams(
            dimension_semantics=("arbitrary", "arbitrary"), vmem_limit_bytes=VMEM_LIMIT),
        name="nsa",
    )(qt, gt, ks, kw, vst, vwt, kc, vct, *tables)


def _mix_out_kernel(x_ref, o_ref, u_ref, halo_ref, wp_ref, sc_ref, wo_ref, y_ref, mix_ref, *, tm):
    ti = pl.program_id(1)
    mix_ref[:, 0:NSA_WIDTH] = o_ref[...]
    halo = jnp.where(ti > 0, halo_ref[...], 0.0)
    pos = ti * tm + lax.broadcasted_iota(jnp.int32, (tm, 1), 0)
    for gi, w in enumerate(POOL_WIDTHS):
        lanes = slice(gi * POOL_GROUP, (gi + 1) * POOL_GROUP)
        u = u_ref[:, lanes]
        ext = jnp.concatenate([halo[:, lanes], u], axis=0)
        run = ext
        step = 1
        while step < w:
            run = run + pltpu.roll(run, step, axis=0)
            step *= 2
        cnt = jnp.minimum(pos + 1, w).astype(F32)
        y = run[POOL_HALO:, :] / cnt - u
        yo = _dot(y.astype(BF16), wp_ref[gi]) * sc_ref[:, lanes]
        mix_ref[:, NSA_WIDTH + gi * POOL_GROUP:NSA_WIDTH + (gi + 1) * POOL_GROUP] = yo.astype(BF16)
    y_ref[...] = x_ref[...] + _dot(mix_ref[...], wo_ref[...])


def _mix_out(x3, o_nsa, u, wp, sc, wo, tm=512):
    b, t, d = x3.shape
    per = tm // POOL_HALO
    return pl.pallas_call(
        functools.partial(_mix_out_kernel, tm=tm),
        grid=(b, t // tm),
        in_specs=[pl.BlockSpec((None, tm, d), lambda i, j: (i, j, 0)),
                  pl.BlockSpec((None, tm, NSA_WIDTH), lambda i, j: (i, j, 0)),
                  pl.BlockSpec((None, tm, POOL_CH), lambda i, j: (i, j, 0)),
                  pl.BlockSpec((None, POOL_HALO, POOL_CH), lambda i, j: (i, jnp.maximum(j * per - 1, 0), 0)),
                  _const_spec(wp.shape), _const_spec(sc.shape), _const_spec(wo.shape)],
        out_specs=pl.BlockSpec((None, tm, d), lambda i, j: (i, j, 0)),
        out_shape=jax.ShapeDtypeStruct((b, t, d), F32),
        scratch_shapes=[pltpu.VMEM((tm, NSA_WIDTH + POOL_CH), BF16)],
        compiler_params=pltpu.CompilerParams(
            dimension_semantics=("parallel", "parallel"), vmem_limit_bytes=VMEM_LIMIT),
        name="mix_out",
    )(x3, o_nsa, u, u, wp, sc, wo)


def _chunk_cols(w):
    d, f = w.shape
    return w.reshape(d, f // FF_CHUNK, FF_CHUNK).transpose(1, 0, 2).astype(BF16)


def _chunk_rows(w):
    f, d = w.shape
    return w.reshape(f // FF_CHUNK, FF_CHUNK, d).astype(BF16)


def kernel(x, ffn1_norm, ffn1_wg, ffn1_wu, ffn1_wd, mix_norm, w_in, cmp_pe_k, cmp_wk1, cmp_wk2,
           cmp_pe_v, cmp_wv1, cmp_wv2, pool_w, pool_scale, w_out, ffn2_norm, ffn2_wg, ffn2_wu,
           ffn2_wd, final_norm):
    b, t, d = x.shape
    depth = ffn1_norm.shape[0]
    n = b * t
    tables = _nsa_tables(t)
    x2 = x.reshape(n, d)
    for l in range(depth):
        x2 = _ffn(x2, ffn1_norm[l][None], _chunk_cols(ffn1_wg[l]), _chunk_cols(ffn1_wu[l]),
                  _chunk_rows(ffn1_wd[l]))
        x3 = x2.reshape(b, t, d)
        kcin, vcin, ks, kw, u, qt, vst, vwt, gt = _proj(x3, mix_norm[l][None], *_split_w_in(w_in[l]))
        kc, vct = _compress(kcin, vcin,
                            *_compress_weights(cmp_pe_k[l], cmp_wk1[l], cmp_wk2[l], False),
                            *_compress_weights(cmp_pe_v[l], cmp_wv1[l], cmp_wv2[l], True))
        o_nsa = _nsa(qt, gt, ks, kw, vst, vwt, kc, vct, tables)
        x2 = _mix_out(x3, o_nsa, u, pool_w[l].astype(BF16), pool_scale[l][None],
                      w_out[l].astype(BF16)).reshape(n, d)
        x2 = _ffn(x2, ffn2_norm[l][None], _chunk_cols(ffn2_wg[l]), _chunk_cols(ffn2_wu[l]),
                  _chunk_rows(ffn2_wd[l]), final_g=final_norm[None] if l == depth - 1 else None)
    return x2.reshape(b, t, d)
```

```python
import functools

import numpy as np
import jax
import jax.numpy as jnp
from jax import lax
from jax.experimental import pallas as pl
from jax.experimental.pallas import tpu as pltpu

HEAD_DIM = 64
NSA_HEADS = 8
KV_GROUPS = 2
HEADS_PER_GROUP = NSA_HEADS // KV_GROUPS
NSA_WIDTH = NSA_HEADS * HEAD_DIM
KV_WIDTH = KV_GROUPS * HEAD_DIM
CMP_BLOCK = 32
CMP_STRIDE = 16
CMP_HIDDEN = 128
SEL_BLOCK = 64
N_SEL = 8
WINDOW = 512
Q_BLOCK = 256
POOL_WIDTHS = (2, 4, 8, 16)
POOL_GROUP = 128
POOL_CH = POOL_GROUP * len(POOL_WIDTHS)
N_GATES = 3 * NSA_HEADS
GATE_ROWS = 32
EPS = 1e-6

LANES = 128
GQ = HEADS_PER_GROUP * Q_BLOCK
SEL_CHUNK = 256
WIN_KEYS = WINDOW + Q_BLOCK
ONES_ROWS = 16
POOL_HALO = 16
FF_CHUNK = 256
LOG2E = 1.4426950408889634
NEG = -1e30
VMEM_LIMIT = 56 * 1024 * 1024

F32 = jnp.float32
BF16 = jnp.bfloat16


def _slopes():
    n = NSA_HEADS
    return (2.0 ** (-8.0 * np.arange(1, n + 1) / n)).astype(np.float32)


def _dot(a, b):
    return jnp.dot(a, b, preferred_element_type=F32)


def _dot_nt(a, b):
    return lax.dot_general(a, b, (((1,), (1,)), ((), ())), preferred_element_type=F32)


def _rms(x, g):
    return x * lax.rsqrt(jnp.mean(x * x, axis=-1, keepdims=True) + EPS) * g


def _const_spec(shape):
    nd = len(shape)
    return pl.BlockSpec(shape, lambda *_: (0,) * nd)


def _ffn_kernel(x_ref, g_ref, wg_ref, wu_ref, wd_ref, *rest, n_chunks, final):
    if final:
        fn_ref, o_ref, h_ref, acc_ref = rest
    else:
        o_ref, h_ref, acc_ref = rest
    h_ref[...] = _rms(x_ref[...], g_ref[...]).astype(BF16)
    acc_ref[...] = jnp.zeros_like(acc_ref)

    def body(c, carry):
        h = h_ref[...]
        gate = _dot(h, wg_ref[c])
        up = _dot(h, wu_ref[c])
        a = (gate * jax.nn.sigmoid(gate) * up).astype(BF16)
        acc_ref[...] += _dot(a, wd_ref[c])
        return carry

    lax.fori_loop(0, n_chunks, body, 0, unroll=True)
    y = x_ref[...] + 0.5 * acc_ref[...]
    if final:
        y = _rms(y, fn_ref[...])
    o_ref[...] = y


def _ffn(x2, g, wg, wu, wd, final_g=None, tm=512):
    n, d = x2.shape
    n_chunks = wg.shape[0]
    final = final_g is not None
    in_specs = [
        pl.BlockSpec((tm, d), lambda i: (i, 0)),
        _const_spec((1, d)),
        _const_spec(wg.shape),
        _const_spec(wu.shape),
        _const_spec(wd.shape),
    ]
    args = [x2, g, wg, wu, wd]
    if final:
        in_specs.append(_const_spec((1, d)))
        args.append(final_g)
    return pl.pallas_call(
        functools.partial(_ffn_kernel, n_chunks=n_chunks, final=final),
        grid=(n // tm,),
        in_specs=in_specs,
        out_specs=pl.BlockSpec((tm, d), lambda i: (i, 0)),
        out_shape=jax.ShapeDtypeStruct((n, d), F32),
        scratch_shapes=[pltpu.VMEM((tm, d), BF16), pltpu.VMEM((tm, d), F32)],
        compiler_params=pltpu.CompilerParams(
            dimension_semantics=("parallel",), vmem_limit_bytes=VMEM_LIMIT),
        name="ffn_final" if final else "ffn",
    )(*args)


_TOK_KC = (0, 128)
_TOK_VC = (128, 256)
_TOK_KS = (256, 384)
_TOK_KW = (384, 512)
_TOK_U = (512, 1024)
_FEAT_Q = (0, 512)
_FEAT_VS = (512, 640)
_FEAT_VW = (640, 768)
_FEAT_GT = (768, 768 + GATE_ROWS)


def _proj_kernel(x_ref, g_ref, wt_ref, wf_ref, kcin_ref, vcin_ref, ks_ref, kw_ref, u_ref,
                 qt_ref, q2t_ref, vst_ref, vwt_ref, gt_ref, *, tm):
    h = _rms(x_ref[...], g_ref[...]).astype(BF16)

    def tok(lo_hi):
        return _dot(h, wt_ref[:, lo_hi[0]:lo_hi[1]])

    kcin_ref[...] = tok(_TOK_KC)
    vcin_ref[...] = tok(_TOK_VC)
    ks_ref[...] = tok(_TOK_KS).astype(BF16)
    kw_ref[...] = tok(_TOK_KW).astype(BF16)
    u_ref[...] = tok(_TOK_U)
    zt = _dot_nt(wf_ref[...], h)
    qt_ref[...] = (zt[_FEAT_Q[0]:_FEAT_Q[1]] * (HEAD_DIM ** -0.5)).astype(BF16)
    q2t_ref[...] = (zt[_FEAT_Q[0]:_FEAT_Q[1]] * (HEAD_DIM ** -0.5 * LOG2E)).astype(BF16)
    gt_ref[...] = zt[_FEAT_GT[0]:_FEAT_GT[1]]
    for c in range(tm // LANES):
        cols = slice(c * LANES, (c + 1) * LANES)
        vst_ref[c] = zt[_FEAT_VS[0]:_FEAT_VS[1], cols].astype(BF16)
        vwt_ref[c] = zt[_FEAT_VW[0]:_FEAT_VW[1], cols].astype(BF16)


def _proj(x3, g, w_tok, w_feat, tm=512):
    b, t, d = x3.shape
    nck = tm // LANES
    tok = lambda wd: pl.BlockSpec((None, tm, wd), lambda i, j: (i, j, 0))
    chunked = pl.BlockSpec((None, nck, KV_WIDTH, LANES), lambda i, j: (i, j, 0, 0))
    return pl.pallas_call(
        functools.partial(_proj_kernel, tm=tm),
        grid=(b, t // tm),
        in_specs=[tok(d), _const_spec((1, d)), _const_spec(w_tok.shape), _const_spec(w_feat.shape)],
        out_specs=[tok(KV_WIDTH), tok(KV_WIDTH), tok(KV_WIDTH), tok(KV_WIDTH), tok(POOL_CH),
                   pl.BlockSpec((None, NSA_WIDTH, tm), lambda i, j: (i, 0, j)),
                   pl.BlockSpec((None, NSA_WIDTH, tm), lambda i, j: (i, 0, j)),
                   chunked, chunked,
                   pl.BlockSpec((None, GATE_ROWS, tm), lambda i, j: (i, 0, j))],
        out_shape=[jax.ShapeDtypeStruct((b, t, KV_WIDTH), F32),
                   jax.ShapeDtypeStruct((b, t, KV_WIDTH), F32),
                   jax.ShapeDtypeStruct((b, t, KV_WIDTH), BF16),
                   jax.ShapeDtypeStruct((b, t, KV_WIDTH), BF16),
                   jax.ShapeDtypeStruct((b, t, POOL_CH), F32),
                   jax.ShapeDtypeStruct((b, NSA_WIDTH, t), BF16),
                   jax.ShapeDtypeStruct((b, NSA_WIDTH, t), BF16),
                   jax.ShapeDtypeStruct((b, t // LANES, KV_WIDTH, LANES), BF16),
                   jax.ShapeDtypeStruct((b, t // LANES, KV_WIDTH, LANES), BF16),
                   jax.ShapeDtypeStruct((b, GATE_ROWS, t), F32)],
        compiler_params=pltpu.CompilerParams(
            dimension_semantics=("parallel", "parallel"), vmem_limit_bytes=VMEM_LIMIT),
        name="proj",
    )(x3, g, w_tok, w_feat)


def _split_w_in(w):
    q0, kv0 = 0, NSA_WIDTH
    col = lambda i: slice(kv0 + i * KV_WIDTH, kv0 + (i + 1) * KV_WIDTH)
    g0 = kv0 + 6 * KV_WIDTH
    w_tok = jnp.concatenate([w[:, col(0)], w[:, col(1)], w[:, col(2)], w[:, col(4)],
                             w[:, g0 + N_GATES:]], axis=1).astype(BF16)
    pad = jnp.zeros((w.shape[0], GATE_ROWS - N_GATES), w.dtype)
    w_feat = jnp.concatenate([w[:, q0:NSA_WIDTH], w[:, col(3)], w[:, col(5)],
                              w[:, g0:g0 + N_GATES], pad], axis=1).T.astype(BF16)
    return w_tok, w_feat


def _gelu_tanh(x):
    c = np.float32(np.sqrt(2.0 / np.pi))
    return 0.5 * x * (1.0 + jnp.tanh(c * (x + 0.044715 * (x * x * x))))


def _compress_kernel(kcin_ref, vcin_ref, pek_ref, wk1_ref, wk2_ref, pev_ref, wv1_ref, wv2_ref,
                     kc_ref, vct_ref, *, n_rows):
    nl = CMP_STRIDE
    hid2 = KV_GROUPS * CMP_HIDDEN

    def hidden(src_ref, pe_ref, w1_ref):
        acc = jnp.zeros((n_rows + 16, 2 * hid2), F32)
        for l in range(nl):
            xl = src_ref[pl.ds(l, n_rows, stride=nl), :]
            pa = jnp.broadcast_to(pe_ref[l:l + 1, :], (8, KV_WIDTH))
            pb = jnp.broadcast_to(pe_ref[nl + l:nl + l + 1, :], (8, KV_WIDTH))
            lhs = jnp.concatenate([xl, pa, pb], axis=0).astype(BF16)
            acc = acc + _dot(lhs, w1_ref[l])
        sa = acc[0:n_rows, 0:hid2]
        sb = acc[0:n_rows, hid2:2 * hid2]
        bias = acc[n_rows:n_rows + 1, 0:hid2] + acc[n_rows + 8:n_rows + 9, hid2:2 * hid2]
        hcur = sa + pltpu.roll(sb, n_rows - 1, axis=0) + bias
        return _gelu_tanh(hcur).astype(BF16)

    hk = hidden(kcin_ref, pek_ref, wk1_ref)
    kc = _dot(hk, wk2_ref[...])
    for g in range(KV_GROUPS):
        kc_ref[g] = kc[:, g * LANES:(g + 1) * LANES].astype(BF16)
    hv = hidden(vcin_ref, pev_ref, wv1_ref)
    for g in range(KV_GROUPS):
        vct_ref[g] = _dot_nt(wv2_ref[g], hv).astype(BF16)


def _compress(kcin, vcin, pek2, wk1b, wk2b, pev2, wv1b, wv2t):
    b, t, _ = kcin.shape
    n_rows = t // CMP_STRIDE
    seq_spec = pl.BlockSpec((None, t, KV_WIDTH), lambda i: (i, 0, 0))
    return pl.pallas_call(
        functools.partial(_compress_kernel, n_rows=n_rows),
        grid=(b,),
        in_specs=[seq_spec, seq_spec,
                  _const_spec(pek2.shape), _const_spec(wk1b.shape), _const_spec(wk2b.shape),
                  _const_spec(pev2.shape), _const_spec(wv1b.shape), _const_spec(wv2t.shape)],
        out_specs=[pl.BlockSpec((None, KV_GROUPS, n_rows, LANES), lambda i: (i, 0, 0, 0)),
                   pl.BlockSpec((None, KV_GROUPS, HEAD_DIM, n_rows), lambda i: (i, 0, 0, 0))],
        out_shape=[jax.ShapeDtypeStruct((b, KV_GROUPS, n_rows, LANES), BF16),
                   jax.ShapeDtypeStruct((b, KV_GROUPS, HEAD_DIM, n_rows), BF16)],
        compiler_params=pltpu.CompilerParams(
            dimension_semantics=("parallel",), vmem_limit_bytes=VMEM_LIMIT),
        name="compress",
    )(kcin, vcin, pek2, wk1b, wk2b, pev2, wv1b, wv2t)


def _compress_weights(pe, w1, w2, value_layout):
    nl, dh, hid = CMP_STRIDE, HEAD_DIM, CMP_HIDDEN
    w1r = w1.reshape(2, nl, dh, hid)
    z = jnp.zeros((nl, dh, hid), w1.dtype)
    row_g0 = jnp.concatenate([w1r[0], z, w1r[1], z], axis=-1)
    row_g1 = jnp.concatenate([z, w1r[0], z, w1r[1]], axis=-1)
    w1b = jnp.concatenate([row_g0, row_g1], axis=1).astype(BF16)
    pe2 = jnp.concatenate([pe, pe], axis=-1)
    zz = jnp.zeros((hid, dh), w2.dtype)
    if value_layout:
        w2b = jnp.stack([jnp.concatenate([w2, zz], axis=0).T,
                         jnp.concatenate([zz, w2], axis=0).T]).astype(BF16)
    else:
        top = jnp.concatenate([w2, zz, zz, zz], axis=-1)
        bot = jnp.concatenate([zz, zz, w2, zz], axis=-1)
        w2b = jnp.concatenate([top, bot], axis=0).astype(BF16)
    return pe2, w1b, w2b


def _nsa_tables(t):
    slopes = _slopes().reshape(KV_GROUPS, HEADS_PER_GROUP)
    n_rows = t // CMP_STRIDE
    n_blk = t // SEL_BLOCK
    i = np.arange(Q_BLOCK)
    c = np.arange(n_rows)
    d0c = (i[None, :] - (c[:, None] * CMP_STRIDE + CMP_BLOCK - 1)).astype(np.int32)
    tblc = -LOG2E * slopes[:, None, :, None].astype(np.float64) * d0c[None, :, None, :]
    tblc = tblc.astype(np.float32)
    tblc = tblc.reshape(KV_GROUPS, n_rows, GQ)
    d0c = np.tile(d0c, (1, HEADS_PER_GROUP))
    j = np.arange(WIN_KEYS)
    dist = i[None, :] + WINDOW - j[:, None]
    ok = (dist >= 0) & (dist < WINDOW)
    tblw = np.where(ok[None, :, None, :],
                    -LOG2E * slopes[:, None, :, None].astype(np.float64) * dist[None, :, None, :],
                    NEG).astype(np.float32).reshape(KV_GROUPS, WIN_KEYS, GQ)
    jj = np.arange(SEL_CHUNK)
    per = SEL_CHUNK // Q_BLOCK
    tailm = np.stack([np.where(jj[:, None] > i[None, :] + Q_BLOCK * r, np.float32(NEG), np.float32(0))
                      for r in range(per)]).astype(np.float32)
    ci = c[None, :] * CMP_STRIDE
    sj = np.arange(n_blk)[:, None] * SEL_BLOCK
    n_cmp = (t - CMP_BLOCK) // CMP_STRIDE + 1
    ov = ((ci <= sj + SEL_BLOCK - 1) & (ci + CMP_BLOCK - 1 >= sj) & (c[None, :] < n_cmp))
    ov = ov.astype(np.float32)
    pos = np.arange(t)
    ktag = np.zeros((t, 2 * LANES), np.float32)
    ktag[pos, HEAD_DIM + pos // SEL_BLOCK] = 1.0
    ktag[:, LANES] = pos % SEL_BLOCK
    qflag = np.zeros((HEAD_DIM, GQ), np.float32)
    qflag[0, :] = NEG
    qslope = np.zeros((KV_GROUPS, LANES, GQ), np.float32)
    qslope[:, 0, :] = np.repeat(slopes, Q_BLOCK, axis=1)
    return (jnp.asarray(tblc), jnp.asarray(d0c), jnp.asarray(tblw), jnp.asarray(tailm),
            jnp.asarray(ov, dtype=BF16), jnp.asarray(ktag, dtype=BF16),
            jnp.asarray(qflag, dtype=BF16), jnp.asarray(qslope, dtype=BF16))


def _nsa_kernel(qt_ref, q2t_ref, gt_ref, ks_ref, kw_ref, vst_ref, vwt_ref, kc_ref, vct_ref,
                tblc_ref, d0c_ref, tblw_ref, tailm_ref, ov_ref, ktag_ref, qflag_ref, qslope_ref,
                o_ref,
                kaug_ref, kwaug_ref, m_ref, acc_ref, need_ref, *, t, n_blk):
    qi = pl.program_id(1)
    slopes = _slopes()
    hg = HEADS_PER_GROUP
    prep_rows = 512

    @pl.when(qi == 0)
    def _prep():
        low = lax.broadcasted_iota(jnp.int32, (prep_rows, LANES), 1) < HEAD_DIM
        flag = jnp.where(lax.broadcasted_iota(jnp.int32, (WINDOW, LANES), 1) == HEAD_DIM, 1.0, 0.0)
        for g in range(KV_GROUPS):
            kwaug_ref[g, 0:WINDOW, :] = flag.astype(BF16)

        def body(c, carry):
            r0 = pl.multiple_of(c * prep_rows, prep_rows)
            rows = pl.ds(r0, prep_rows)
            rows_w = pl.ds(r0 + WINDOW, prep_rows)
            tag = ktag_ref[rows, 0:LANES].astype(F32)
            ks = ks_ref[rows, :].astype(F32)
            kw = kw_ref[rows, :].astype(F32)
            ks_sw = pltpu.roll(ks, HEAD_DIM, axis=1)
            kw_sw = pltpu.roll(kw, HEAD_DIM, axis=1)
            for g in range(KV_GROUPS):
                kaug_ref[g, rows, 0:LANES] = jnp.where(low, ks if g == 0 else ks_sw, tag).astype(BF16)
                kaug_ref[g, rows, LANES:2 * LANES] = ktag_ref[rows, LANES:2 * LANES]
                kwaug_ref[g, rows_w, :] = jnp.where(low, kw if g == 0 else kw_sw, 0.0).astype(BF16)
            return carry

        lax.fori_loop(0, t // prep_rows, body, 0)

    gates = jax.nn.sigmoid(gt_ref[...])
    bi = lax.broadcasted_iota(jnp.int32, (n_blk, Q_BLOCK), 0)
    lq = lax.broadcasted_iota(jnp.int32, (n_blk, Q_BLOCK), 1)
    cur = (Q_BLOCK // SEL_BLOCK) * qi + lax.shift_right_logical(lq, SEL_BLOCK.bit_length() - 1)
    valid = bi <= cur
    near = (bi == 0) | (bi >= cur - 1)
    always = valid & (near | (cur < N_SEL))
    free = valid & jnp.logical_not(near)
    rel_blk = (bi - cur).astype(F32)

    def with_ones(vt):
        first = lax.broadcasted_iota(jnp.int32, (ONES_ROWS, vt.shape[1]), 0) == 0
        return jnp.concatenate([vt, jnp.where(first, 1.0, 0.0).astype(BF16)], axis=0)

    def queries(g):
        heads = lambda ref: jnp.concatenate([ref[(g * hg + h) * HEAD_DIM:(g * hg + h + 1) * HEAD_DIM, :]
                                             for h in range(hg)], axis=1)
        return heads(qt_ref), jnp.concatenate([heads(q2t_ref), qflag_ref[...]], axis=0)

    def select(g, qt, s):
        s = s + tblc_ref[g]
        s = jnp.where(d0c_ref[...] + Q_BLOCK * qi >= 0, s, NEG)
        m = jnp.max(s, axis=0, keepdims=True)
        m = jnp.where(m < 0.5 * NEG, 0.0, m)
        p = jnp.exp2(s - m)
        l = jnp.sum(p, axis=0, keepdims=True)
        pn = p * (1.0 / jnp.maximum(l, 1e-30))
        o_cmp = _dot(vct_ref[g], pn.astype(BF16))
        psum = pn[:, 0:Q_BLOCK]
        for h in range(1, hg):
            psum = psum + pn[:, h * Q_BLOCK:(h + 1) * Q_BLOCK]

        p_hi = psum.astype(BF16)
        p_lo = (psum - p_hi.astype(F32)).astype(BF16)
        imp = _dot(ov_ref[...], p_hi) + _dot(ov_ref[...], p_lo)
        val = jnp.where(free, imp, -1.0)
        sel = jnp.zeros((n_blk, Q_BLOCK), F32)
        for _ in range(N_SEL - 3):
            top = jnp.max(val, axis=0, keepdims=True)
            idx = jnp.min(jnp.where(val == top, bi, n_blk), axis=0, keepdims=True)
            pick = bi == idx
            sel = jnp.where(pick, 1.0, sel)
            val = jnp.where(pick, -2.0, val)
        chosen = jnp.where(always, 1.0, jnp.where(free, sel, 0.0)) > 0.0

        bterm = jnp.concatenate(
            [jnp.where(chosen, float(slopes[g * hg + h]) * SEL_BLOCK * rel_blk, NEG) for h in range(hg)],
            axis=1).astype(BF16)
        parts = [qt, bterm]
        if n_blk < HEAD_DIM:
            parts.append(jnp.zeros((HEAD_DIM - n_blk, GQ), BF16))
        parts.append(qslope_ref[g])
        qaug = jnp.concatenate(parts, axis=0)
        return o_cmp, qaug, jnp.where(chosen, 1.0, 0.0)

    def sel_scores(g, qaug, chunk, mask):
        rows = pl.ds(pl.multiple_of(chunk * SEL_CHUNK, SEL_CHUNK), SEL_CHUNK)
        s = _dot(kaug_ref[g, rows, :], qaug)
        return s if mask is None else s + mask

    def sel_update(g, s, chunk, carry):
        m, acc = carry
        m_new = jnp.maximum(m, jnp.max(s, axis=0, keepdims=True))
        alpha = jnp.exp(m - m_new)
        p = jnp.exp(s - m_new).astype(BF16)
        per_chunk = SEL_CHUNK // LANES
        vt = jnp.concatenate([vst_ref[chunk * per_chunk + k, g * HEAD_DIM:(g + 1) * HEAD_DIM, :]
                              for k in range(per_chunk)], axis=1)
        return m_new, acc * alpha + _dot(with_ones(vt), p)

    def sel_steps(chunk, mask, carries):
        scores = [sel_scores(g, qaug[g], chunk, mask) for g in groups]
        return tuple(sel_update(g, scores[g], chunk, carries[g]) for g in groups)

    def window_scores(g, qw):
        rows = pl.ds(pl.multiple_of(qi * Q_BLOCK, Q_BLOCK), WIN_KEYS)
        return _dot(kwaug_ref[g, rows, :], qw)

    def window(g, s):
        s = s + tblw_ref[g]
        m = jnp.max(s, axis=0, keepdims=True)
        p = jnp.exp2(s - m).astype(BF16)
        back = WINDOW // LANES
        ahead = Q_BLOCK // LANES
        vt = jnp.concatenate([vwt_ref[jnp.maximum(qi * ahead - back + k, 0), g * HEAD_DIM:(g + 1) * HEAD_DIM, :]
                              for k in range(back + ahead)], axis=1)
        acc = _dot(with_ones(vt), p)
        return acc[0:HEAD_DIM] * (1.0 / acc[HEAD_DIM:HEAD_DIM + 1])

    groups = tuple(range(KV_GROUPS))
    qt, qw = zip(*[queries(g) for g in groups])
    s_cmp = [_dot(kc_ref[g], qw[g]) for g in groups]
    s_win = [window_scores(g, qw[g]) for g in groups]
    o_cmp, qaug, picked = zip(*[select(g, qt[g], s_cmp[g]) for g in groups])
    o_win = [window(g, s_win[g]) for g in groups]

    per = SEL_CHUNK // Q_BLOCK
    blk_per_chunk = SEL_CHUNK // SEL_BLOCK
    n_full = qi // per
    wanted = picked[0]
    for g in groups[1:]:
        wanted = wanted + picked[g]
    for c in range(n_blk // blk_per_chunk - 1):
        hit = jnp.sum(wanted[c * blk_per_chunk:(c + 1) * blk_per_chunk, :])
        need_ref[c] = (hit > 0.0).astype(jnp.int32)
    for g in groups:
        m_ref[g] = jnp.full((8, GQ), NEG, F32)
        acc_ref[g] = jnp.zeros((HEAD_DIM + ONES_ROWS, GQ), F32)

    def load_carries():
        return tuple((m_ref[g, 0:1, :], acc_ref[g]) for g in groups)

    def full_chunk(c, carry):
        @pl.when(need_ref[c] > 0)
        def _():
            for g, (m, acc) in enumerate(sel_steps(c, None, load_carries())):
                m_ref[g, 0:1, :] = m
                acc_ref[g] = acc
        return carry

    lax.fori_loop(0, n_full, full_chunk, 0)
    tail_mask = jnp.concatenate([tailm_ref[qi % per]] * hg, axis=1)
    carries = sel_steps(n_full, tail_mask, load_carries())
    o_slc = [acc[0:HEAD_DIM] * (1.0 / acc[HEAD_DIM:HEAD_DIM + 1]) for _, acc in carries]

    for g in groups:
        for pr in range(hg // 2):
            outs = []
            for h in (2 * pr, 2 * pr + 1):
                hh = g * hg + h
                lanes = slice(h * Q_BLOCK, (h + 1) * Q_BLOCK)
                out = gates[3 * hh:3 * hh + 1, :] * o_cmp[g][:, lanes]
                out = out + gates[3 * hh + 1:3 * hh + 2, :] * o_slc[g][:, lanes]
                out = out + gates[3 * hh + 2:3 * hh + 3, :] * o_win[g][:, lanes]
                outs.append(out)
            pair = jnp.concatenate(outs, axis=0)
            o_ref[:, (g * 2 + pr) * LANES:(g * 2 + pr + 1) * LANES] = pair.T.astype(BF16)


def _nsa(qt, q2t, gt, ks, kw, vst, vwt, kc, vct, tables):
    b, _, t = qt.shape
    n_blk = t // SEL_BLOCK
    n_rows = t // CMP_STRIDE
    assert t % SEL_CHUNK == 0 and SEL_CHUNK % Q_BLOCK == 0
    assert N_SEL <= n_blk <= HEAD_DIM and n_blk % 16 == 0
    seq_spec = pl.BlockSpec((None, t, LANES), lambda i, j: (i, 0, 0))
    chunked = pl.BlockSpec((None, t // LANES, KV_WIDTH, LANES), lambda i, j: (i, 0, 0, 0))
    return pl.pallas_call(
        functools.partial(_nsa_kernel, t=t, n_blk=n_blk),
        grid=(b, t // Q_BLOCK),
        in_specs=[pl.BlockSpec((None, NSA_WIDTH, Q_BLOCK), lambda i, j: (i, 0, j)),
                  pl.BlockSpec((None, NSA_WIDTH, Q_BLOCK), lambda i, j: (i, 0, j)),
                  pl.BlockSpec((None, GATE_ROWS, Q_BLOCK), lambda i, j: (i, 0, j)),
                  seq_spec, seq_spec, chunked, chunked,
                  pl.BlockSpec((None, KV_GROUPS, n_rows, LANES), lambda i, j: (i, 0, 0, 0)),
                  pl.BlockSpec((None, KV_GROUPS, HEAD_DIM, n_rows), lambda i, j: (i, 0, 0, 0))]
                 + [_const_spec(a.shape) for a in tables],
        out_specs=pl.BlockSpec((None, Q_BLOCK, NSA_WIDTH), lambda i, j: (i, j, 0)),
        out_shape=jax.ShapeDtypeStruct((b, t, NSA_WIDTH), BF16),
        scratch_shapes=[pltpu.VMEM((KV_GROUPS, t, 2 * LANES), BF16),
                        pltpu.VMEM((KV_GROUPS, t + WINDOW, LANES), BF16),
                        pltpu.VMEM((KV_GROUPS, 8, GQ), F32),
                        pltpu.VMEM((KV_GROUPS, HEAD_DIM + ONES_ROWS, GQ), F32),
                        pltpu.SMEM((t // SEL_CHUNK,), jnp.int32)],
        compiler_params=pltpu.CompilerParams(
            dimension_semantics=("arbitrary", "arbitrary"), vmem_limit_bytes=VMEM_LIMIT),
        name="nsa",
    )(qt, q2t, gt, ks, kw, vst, vwt, kc, vct, *tables)


def _mix_out_kernel(x_ref, o_ref, u_ref, halo_ref, wp_ref, sc_ref, wo_ref, y_ref, mix_ref, *, tm):
    ti = pl.program_id(1)
    mix_ref[:, 0:NSA_WIDTH] = o_ref[...]
    halo = jnp.where(ti > 0, halo_ref[...], 0.0)
    pos = ti * tm + lax.broadcasted_iota(jnp.int32, (tm, 1), 0)
    for gi, w in enumerate(POOL_WIDTHS):
        lanes = slice(gi * POOL_GROUP, (gi + 1) * POOL_GROUP)
        u = u_ref[:, lanes]
        ext = jnp.concatenate([halo[:, lanes], u], axis=0)
        run = ext
        step = 1
        while step < w:
            run = run + pltpu.roll(run, step, axis=0)
            step *= 2
        cnt = jnp.minimum(pos + 1, w).astype(F32)
        y = run[POOL_HALO:, :] / cnt - u
        yo = _dot(y.astype(BF16), wp_ref[gi]) * sc_ref[:, lanes]
        mix_ref[:, NSA_WIDTH + gi * POOL_GROUP:NSA_WIDTH + (gi + 1) * POOL_GROUP] = yo.astype(BF16)
    y_ref[...] = x_ref[...] + _dot(mix_ref[...], wo_ref[...])


def _mix_out(x3, o_nsa, u, wp, sc, wo, tm=512):
    b, t, d = x3.shape
    per = tm // POOL_HALO
    return pl.pallas_call(
        functools.partial(_mix_out_kernel, tm=tm),
        grid=(b, t // tm),
        in_specs=[pl.BlockSpec((None, tm, d), lambda i, j: (i, j, 0)),
                  pl.BlockSpec((None, tm, NSA_WIDTH), lambda i, j: (i, j, 0)),
                  pl.BlockSpec((None, tm, POOL_CH), lambda i, j: (i, j, 0)),
                  pl.BlockSpec((None, POOL_HALO, POOL_CH), lambda i, j: (i, jnp.maximum(j * per - 1, 0), 0)),
                  _const_spec(wp.shape), _const_spec(sc.shape), _const_spec(wo.shape)],
        out_specs=pl.BlockSpec((None, tm, d), lambda i, j: (i, j, 0)),
        out_shape=jax.ShapeDtypeStruct((b, t, d), F32),
        scratch_shapes=[pltpu.VMEM((tm, NSA_WIDTH + POOL_CH), BF16)],
        compiler_params=pltpu.CompilerParams(
            dimension_semantics=("parallel", "parallel"), vmem_limit_bytes=VMEM_LIMIT),
        name="mix_out",
    )(x3, o_nsa, u, u, wp, sc, wo)


def _chunk_cols(w):
    d, f = w.shape
    return w.reshape(d, f // FF_CHUNK, FF_CHUNK).transpose(1, 0, 2).astype(BF16)


def _chunk_rows(w):
    f, d = w.shape
    return w.reshape(f // FF_CHUNK, FF_CHUNK, d).astype(BF16)


def kernel(x, ffn1_norm, ffn1_wg, ffn1_wu, ffn1_wd, mix_norm, w_in, cmp_pe_k, cmp_wk1, cmp_wk2,
           cmp_pe_v, cmp_wv1, cmp_wv2, pool_w, pool_scale, w_out, ffn2_norm, ffn2_wg, ffn2_wu,
           ffn2_wd, final_norm):
    b, t, d = x.shape
    depth = ffn1_norm.shape[0]
    n = b * t
    tables = _nsa_tables(t)
    x2 = x.reshape(n, d)
    for l in range(depth):
        x2 = _ffn(x2, ffn1_norm[l][None], _chunk_cols(ffn1_wg[l]), _chunk_cols(ffn1_wu[l]),
                  _chunk_rows(ffn1_wd[l]))
        x3 = x2.reshape(b, t, d)
        kcin, vcin, ks, kw, u, qt, q2t, vst, vwt, gt = _proj(x3, mix_norm[l][None], *_split_w_in(w_in[l]))
        kc, vct = _compress(kcin, vcin,
                            *_compress_weights(cmp_pe_k[l], cmp_wk1[l], cmp_wk2[l], False),
                            *_compress_weights(cmp_pe_v[l], cmp_wv1[l], cmp_wv2[l], True))
        o_nsa = _nsa(qt, q2t, gt, ks, kw, vst, vwt, kc, vct, tables)
        x2 = _mix_out(x3, o_nsa, u, pool_w[l].astype(BF16), pool_scale[l][None],
                      w_out[l].astype(BF16)).reshape(n, d)
        x2 = _ffn(x2, ffn2_norm[l][None], _chunk_cols(ffn2_wg[l]), _chunk_cols(ffn2_wu[l]),
                  _chunk_rows(ffn2_wd[l]), final_g=final_norm[None] if l == depth - 1 else None)
    return x2.reshape(b, t, d)
```

```python
import functools

import numpy as np
import jax
import jax.numpy as jnp
from jax import lax
from jax.experimental import pallas as pl
from jax.experimental.pallas import tpu as pltpu

HEAD_DIM = 64
NSA_HEADS = 8
KV_GROUPS = 2
HEADS_PER_GROUP = NSA_HEADS // KV_GROUPS
NSA_WIDTH = NSA_HEADS * HEAD_DIM
KV_WIDTH = KV_GROUPS * HEAD_DIM
CMP_BLOCK = 32
CMP_STRIDE = 16
CMP_HIDDEN = 128
SEL_BLOCK = 64
N_SEL = 8
WINDOW = 512
Q_BLOCK = 256
POOL_WIDTHS = (2, 4, 8, 16)
POOL_GROUP = 128
POOL_CH = POOL_GROUP * len(POOL_WIDTHS)
N_GATES = 3 * NSA_HEADS
GATE_ROWS = 32
EPS = 1e-6

LANES = 128
GQ = HEADS_PER_GROUP * Q_BLOCK
SEL_CHUNK = 256
WIN_KEYS = WINDOW + Q_BLOCK
ONES_ROWS = 16
POOL_HALO = 16
FF_CHUNK = 256
LOG2E = 1.4426950408889634
NEG = -1e30
VMEM_LIMIT = 56 * 1024 * 1024

F32 = jnp.float32
BF16 = jnp.bfloat16


def _slopes():
    n = NSA_HEADS
    return (2.0 ** (-8.0 * np.arange(1, n + 1) / n)).astype(np.float32)


def _dot(a, b):
    return jnp.dot(a, b, preferred_element_type=F32)


def _dot_nt(a, b):
    return lax.dot_general(a, b, (((1,), (1,)), ((), ())), preferred_element_type=F32)


def _rms(x, g):
    return x * lax.rsqrt(jnp.mean(x * x, axis=-1, keepdims=True) + EPS) * g


def _const_spec(shape):
    nd = len(shape)
    return pl.BlockSpec(shape, lambda *_: (0,) * nd)


def _swiglu_residual(x, g_ref, wg_ref, wu_ref, wd_ref, h_ref, acc_ref):
    h_ref[...] = _rms(x, g_ref[...]).astype(BF16)
    for c in range(wg_ref.shape[1] // FF_CHUNK):
        cols = slice(c * FF_CHUNK, (c + 1) * FF_CHUNK)
        h = h_ref[...]
        gate = _dot(h, wg_ref[:, cols])
        up = _dot(h, wu_ref[:, cols])
        a = (gate * jax.nn.sigmoid(gate) * up).astype(BF16)
        part = _dot(a, wd_ref[cols, :])
        if c == 0:
            acc_ref[...] = part
        else:
            acc_ref[...] += part
    return x + 0.5 * acc_ref[...]


def _resident(shape):
    nd = len(shape)
    return pl.BlockSpec(shape, lambda *_: (0,) * nd, pipeline_mode=pl.Buffered(1))


_TOK_KC = (0, 128)
_TOK_VC = (128, 256)
_TOK_KS = (256, 384)
_TOK_KW = (384, 512)
_TOK_U = (512, 1024)
_FEAT_Q = (0, 512)
_FEAT_VS = (512, 640)
_FEAT_VW = (640, 768)
_FEAT_GT = (768, 768 + GATE_ROWS)


def _ffn_proj_kernel(x_ref, g1_ref, wg_ref, wu_ref, wd_ref, g_ref, wt_ref, wf_ref,
                     x1_ref, kcin_ref, vcin_ref, ks_ref, kw_ref, u_ref,
                     qt_ref, q2t_ref, vst_ref, vwt_ref, gt_ref, h_ref, acc_ref, *, tm):
    x1 = _swiglu_residual(x_ref[...], g1_ref, wg_ref, wu_ref, wd_ref, h_ref, acc_ref)
    x1_ref[...] = x1
    h = _rms(x1, g_ref[...]).astype(BF16)

    def tok(lo_hi):
        return _dot(h, wt_ref[:, lo_hi[0]:lo_hi[1]])

    kcin_ref[...] = tok(_TOK_KC)
    vcin_ref[...] = tok(_TOK_VC)
    ks_ref[...] = tok(_TOK_KS).astype(BF16)
    kw_ref[...] = tok(_TOK_KW).astype(BF16)
    u_ref[...] = tok(_TOK_U)
    zt = _dot_nt(wf_ref[...], h)
    qt_ref[...] = (zt[_FEAT_Q[0]:_FEAT_Q[1]] * (HEAD_DIM ** -0.5)).astype(BF16)
    q2t_ref[...] = (zt[_FEAT_Q[0]:_FEAT_Q[1]] * (HEAD_DIM ** -0.5 * LOG2E)).astype(BF16)
    gt_ref[...] = zt[_FEAT_GT[0]:_FEAT_GT[1]]
    for c in range(tm // LANES):
        cols = slice(c * LANES, (c + 1) * LANES)
        vst_ref[c] = zt[_FEAT_VS[0]:_FEAT_VS[1], cols].astype(BF16)
        vwt_ref[c] = zt[_FEAT_VW[0]:_FEAT_VW[1], cols].astype(BF16)


def _ffn_proj(x3, g1, wg, wu, wd, g, w_tok, w_feat, tm=512):
    b, t, d = x3.shape
    nck = tm // LANES
    tok = lambda wd: pl.BlockSpec((None, tm, wd), lambda i, j: (i, j, 0))
    chunked = pl.BlockSpec((None, nck, KV_WIDTH, LANES), lambda i, j: (i, j, 0, 0))
    return pl.pallas_call(
        functools.partial(_ffn_proj_kernel, tm=tm),
        grid=(b, t // tm),
        in_specs=[tok(d), _resident((1, d)), _resident(wg.shape), _resident(wu.shape), _resident(wd.shape),
                  _resident((1, d)), _resident(w_tok.shape), _resident(w_feat.shape)],
        out_specs=[tok(d), tok(KV_WIDTH), tok(KV_WIDTH), tok(KV_WIDTH), tok(KV_WIDTH), tok(POOL_CH),
                   pl.BlockSpec((None, NSA_WIDTH, tm), lambda i, j: (i, 0, j)),
                   pl.BlockSpec((None, NSA_WIDTH, tm), lambda i, j: (i, 0, j)),
                   chunked, chunked,
                   pl.BlockSpec((None, GATE_ROWS, tm), lambda i, j: (i, 0, j))],
        out_shape=[jax.ShapeDtypeStruct((b, t, d), F32),
                   jax.ShapeDtypeStruct((b, t, KV_WIDTH), F32),
                   jax.ShapeDtypeStruct((b, t, KV_WIDTH), F32),
                   jax.ShapeDtypeStruct((b, t, KV_WIDTH), BF16),
                   jax.ShapeDtypeStruct((b, t, KV_WIDTH), BF16),
                   jax.ShapeDtypeStruct((b, t, POOL_CH), F32),
                   jax.ShapeDtypeStruct((b, NSA_WIDTH, t), BF16),
                   jax.ShapeDtypeStruct((b, NSA_WIDTH, t), BF16),
                   jax.ShapeDtypeStruct((b, t // LANES, KV_WIDTH, LANES), BF16),
                   jax.ShapeDtypeStruct((b, t // LANES, KV_WIDTH, LANES), BF16),
                   jax.ShapeDtypeStruct((b, GATE_ROWS, t), F32)],
        scratch_shapes=[pltpu.VMEM((tm, d), BF16), pltpu.VMEM((tm, d), F32)],
        compiler_params=pltpu.CompilerParams(
            dimension_semantics=("parallel", "parallel"), vmem_limit_bytes=VMEM_LIMIT),
        name="ffn_proj",
    )(x3, g1, wg, wu, wd, g, w_tok, w_feat)


def _split_w_in(w):
    q0, kv0 = 0, NSA_WIDTH
    col = lambda i: slice(kv0 + i * KV_WIDTH, kv0 + (i + 1) * KV_WIDTH)
    g0 = kv0 + 6 * KV_WIDTH
    w_tok = jnp.concatenate([w[:, col(0)], w[:, col(1)], w[:, col(2)], w[:, col(4)],
                             w[:, g0 + N_GATES:]], axis=1).astype(BF16)
    pad = jnp.zeros((w.shape[0], GATE_ROWS - N_GATES), w.dtype)
    w_feat = jnp.concatenate([w[:, q0:NSA_WIDTH], w[:, col(3)], w[:, col(5)],
                              w[:, g0:g0 + N_GATES], pad], axis=1).T.astype(BF16)
    return w_tok, w_feat


def _gelu_tanh(x):
    c = np.float32(np.sqrt(2.0 / np.pi))
    return 0.5 * x * (1.0 + jnp.tanh(c * (x + 0.044715 * (x * x * x))))


def _compress_kernel(kcin_ref, vcin_ref, pek_ref, wk1_ref, wk2_ref, pev_ref, wv1_ref, wv2_ref,
                     kc_ref, vct_ref, *, n_rows):
    nl = CMP_STRIDE
    hid2 = KV_GROUPS * CMP_HIDDEN

    def hidden(src_ref, pe_ref, w1_ref):
        acc = jnp.zeros((n_rows + 16, 2 * hid2), F32)
        for l in range(nl):
            xl = src_ref[pl.ds(l, n_rows, stride=nl), :]
            pa = jnp.broadcast_to(pe_ref[l:l + 1, :], (8, KV_WIDTH))
            pb = jnp.broadcast_to(pe_ref[nl + l:nl + l + 1, :], (8, KV_WIDTH))
            lhs = jnp.concatenate([xl, pa, pb], axis=0).astype(BF16)
            acc = acc + _dot(lhs, w1_ref[l])
        sa = acc[0:n_rows, 0:hid2]
        sb = acc[0:n_rows, hid2:2 * hid2]
        bias = acc[n_rows:n_rows + 1, 0:hid2] + acc[n_rows + 8:n_rows + 9, hid2:2 * hid2]
        hcur = sa + pltpu.roll(sb, n_rows - 1, axis=0) + bias
        return _gelu_tanh(hcur).astype(BF16)

    hk = hidden(kcin_ref, pek_ref, wk1_ref)
    kc = _dot(hk, wk2_ref[...])
    for g in range(KV_GROUPS):
        kc_ref[g] = kc[:, g * LANES:(g + 1) * LANES].astype(BF16)
    hv = hidden(vcin_ref, pev_ref, wv1_ref)
    for g in range(KV_GROUPS):
        vct_ref[g] = _dot_nt(wv2_ref[g], hv).astype(BF16)


def _compress(kcin, vcin, pek2, wk1b, wk2b, pev2, wv1b, wv2t):
    b, t, _ = kcin.shape
    n_rows = t // CMP_STRIDE
    seq_spec = pl.BlockSpec((None, t, KV_WIDTH), lambda i: (i, 0, 0))
    return pl.pallas_call(
        functools.partial(_compress_kernel, n_rows=n_rows),
        grid=(b,),
        in_specs=[seq_spec, seq_spec,
                  _const_spec(pek2.shape), _const_spec(wk1b.shape), _const_spec(wk2b.shape),
                  _const_spec(pev2.shape), _const_spec(wv1b.shape), _const_spec(wv2t.shape)],
        out_specs=[pl.BlockSpec((None, KV_GROUPS, n_rows, LANES), lambda i: (i, 0, 0, 0)),
                   pl.BlockSpec((None, KV_GROUPS, HEAD_DIM, n_rows), lambda i: (i, 0, 0, 0))],
        out_shape=[jax.ShapeDtypeStruct((b, KV_GROUPS, n_rows, LANES), BF16),
                   jax.ShapeDtypeStruct((b, KV_GROUPS, HEAD_DIM, n_rows), BF16)],
        compiler_params=pltpu.CompilerParams(
            dimension_semantics=("parallel",), vmem_limit_bytes=VMEM_LIMIT),
        name="compress",
    )(kcin, vcin, pek2, wk1b, wk2b, pev2, wv1b, wv2t)


def _compress_weights(pe, w1, w2, value_layout):
    nl, dh, hid = CMP_STRIDE, HEAD_DIM, CMP_HIDDEN
    w1r = w1.reshape(2, nl, dh, hid)
    z = jnp.zeros((nl, dh, hid), w1.dtype)
    row_g0 = jnp.concatenate([w1r[0], z, w1r[1], z], axis=-1)
    row_g1 = jnp.concatenate([z, w1r[0], z, w1r[1]], axis=-1)
    w1b = jnp.concatenate([row_g0, row_g1], axis=1).astype(BF16)
    pe2 = jnp.concatenate([pe, pe], axis=-1)
    zz = jnp.zeros((hid, dh), w2.dtype)
    if value_layout:
        w2b = jnp.stack([jnp.concatenate([w2, zz], axis=0).T,
                         jnp.concatenate([zz, w2], axis=0).T]).astype(BF16)
    else:
        top = jnp.concatenate([w2, zz, zz, zz], axis=-1)
        bot = jnp.concatenate([zz, zz, w2, zz], axis=-1)
        w2b = jnp.concatenate([top, bot], axis=0).astype(BF16)
    return pe2, w1b, w2b


def _nsa_tables(t):
    slopes = _slopes().reshape(KV_GROUPS, HEADS_PER_GROUP)
    n_rows = t // CMP_STRIDE
    n_blk = t // SEL_BLOCK
    i = np.arange(Q_BLOCK)
    c = np.arange(n_rows)
    d0c = (i[None, :] - (c[:, None] * CMP_STRIDE + CMP_BLOCK - 1)).astype(np.int32)
    tblc = -LOG2E * slopes[:, None, :, None].astype(np.float64) * d0c[None, :, None, :]
    tblc = tblc.astype(np.float32)
    tblc = tblc.reshape(KV_GROUPS, n_rows, GQ)
    d0c = np.tile(d0c, (1, HEADS_PER_GROUP))
    j = np.arange(WIN_KEYS)
    dist = i[None, :] + WINDOW - j[:, None]
    ok = (dist >= 0) & (dist < WINDOW)
    tblw = np.where(ok[None, :, None, :],
                    -LOG2E * slopes[:, None, :, None].astype(np.float64) * dist[None, :, None, :],
                    NEG).astype(np.float32).reshape(KV_GROUPS, WIN_KEYS, GQ)
    jj = np.arange(SEL_CHUNK)
    per = SEL_CHUNK // Q_BLOCK
    tailm = np.stack([np.where(jj[:, None] > i[None, :] + Q_BLOCK * r, np.float32(NEG), np.float32(0))
                      for r in range(per)]).astype(np.float32)
    ci = c[None, :] * CMP_STRIDE
    sj = np.arange(n_blk)[:, None] * SEL_BLOCK
    n_cmp = (t - CMP_BLOCK) // CMP_STRIDE + 1
    ov = ((ci <= sj + SEL_BLOCK - 1) & (ci + CMP_BLOCK - 1 >= sj) & (c[None, :] < n_cmp))
    ov = ov.astype(np.float32)
    pos = np.arange(t)
    ktag = np.zeros((t, 2 * LANES), np.float32)
    ktag[pos, HEAD_DIM + pos // SEL_BLOCK] = 1.0
    ktag[:, LANES] = pos % SEL_BLOCK
    qflag = np.zeros((HEAD_DIM, GQ), np.float32)
    qflag[0, :] = NEG
    qslope = np.zeros((KV_GROUPS, LANES, GQ), np.float32)
    qslope[:, 0, :] = np.repeat(slopes, Q_BLOCK, axis=1)
    return (jnp.asarray(tblc), jnp.asarray(d0c), jnp.asarray(tblw), jnp.asarray(tailm),
            jnp.asarray(ov, dtype=BF16), jnp.asarray(ktag, dtype=BF16),
            jnp.asarray(qflag, dtype=BF16), jnp.asarray(qslope, dtype=BF16))


def _nsa_kernel(qt_ref, q2t_ref, gt_ref, ks_ref, kw_ref, vst_ref, vwt_ref, kc_ref, vct_ref,
                tblc_ref, d0c_ref, tblw_ref, tailm_ref, ov_ref, ktag_ref, qflag_ref, qslope_ref,
                o_ref,
                kaug_ref, kwaug_ref, m_ref, acc_ref, need_ref, *, t, n_blk):
    qi = pl.program_id(1)
    slopes = _slopes()
    hg = HEADS_PER_GROUP
    prep_rows = 512

    @pl.when(qi == 0)
    def _prep():
        low = lax.broadcasted_iota(jnp.int32, (prep_rows, LANES), 1) < HEAD_DIM
        flag = jnp.where(lax.broadcasted_iota(jnp.int32, (WINDOW, LANES), 1) == HEAD_DIM, 1.0, 0.0)
        for g in range(KV_GROUPS):
            kwaug_ref[g, 0:WINDOW, :] = flag.astype(BF16)

        def body(c, carry):
            r0 = pl.multiple_of(c * prep_rows, prep_rows)
            rows = pl.ds(r0, prep_rows)
            rows_w = pl.ds(r0 + WINDOW, prep_rows)
            tag = ktag_ref[rows, 0:LANES].astype(F32)
            ks = ks_ref[rows, :].astype(F32)
            kw = kw_ref[rows, :].astype(F32)
            ks_sw = pltpu.roll(ks, HEAD_DIM, axis=1)
            kw_sw = pltpu.roll(kw, HEAD_DIM, axis=1)
            for g in range(KV_GROUPS):
                kaug_ref[g, rows, 0:LANES] = jnp.where(low, ks if g == 0 else ks_sw, tag).astype(BF16)
                kaug_ref[g, rows, LANES:2 * LANES] = ktag_ref[rows, LANES:2 * LANES]
                kwaug_ref[g, rows_w, :] = jnp.where(low, kw if g == 0 else kw_sw, 0.0).astype(BF16)
            return carry

        lax.fori_loop(0, t // prep_rows, body, 0)

    gates = jax.nn.sigmoid(gt_ref[...])
    bi = lax.broadcasted_iota(jnp.int32, (n_blk, Q_BLOCK), 0)
    lq = lax.broadcasted_iota(jnp.int32, (n_blk, Q_BLOCK), 1)
    cur = (Q_BLOCK // SEL_BLOCK) * qi + lax.shift_right_logical(lq, SEL_BLOCK.bit_length() - 1)
    valid = bi <= cur
    near = (bi == 0) | (bi >= cur - 1)
    always = valid & (near | (cur < N_SEL))
    free = valid & jnp.logical_not(near)
    rel_blk = (bi - cur).astype(F32)

    def with_ones(vt):
        first = lax.broadcasted_iota(jnp.int32, (ONES_ROWS, vt.shape[1]), 0) == 0
        return jnp.concatenate([vt, jnp.where(first, 1.0, 0.0).astype(BF16)], axis=0)

    def queries(g):
        heads = lambda ref: jnp.concatenate([ref[(g * hg + h) * HEAD_DIM:(g * hg + h + 1) * HEAD_DIM, :]
                                             for h in range(hg)], axis=1)
        return heads(qt_ref), jnp.concatenate([heads(q2t_ref), qflag_ref[...]], axis=0)

    def select(g, qt, s):
        s = s + tblc_ref[g]
        s = jnp.where(d0c_ref[...] + Q_BLOCK * qi >= 0, s, NEG)
        m = jnp.max(s, axis=0, keepdims=True)
        m = jnp.where(m < 0.5 * NEG, 0.0, m)
        p = jnp.exp2(s - m)
        l = jnp.sum(p, axis=0, keepdims=True)
        pn = p * (1.0 / jnp.maximum(l, 1e-30))
        o_cmp = _dot(vct_ref[g], pn.astype(BF16))
        psum = pn[:, 0:Q_BLOCK]
        for h in range(1, hg):
            psum = psum + pn[:, h * Q_BLOCK:(h + 1) * Q_BLOCK]

        p_hi = psum.astype(BF16)
        p_lo = (psum - p_hi.astype(F32)).astype(BF16)
        imp = _dot(ov_ref[...], p_hi) + _dot(ov_ref[...], p_lo)
        val = jnp.where(free, imp, -1.0)
        sel = jnp.zeros((n_blk, Q_BLOCK), F32)
        for _ in range(N_SEL - 3):
            top = jnp.max(val, axis=0, keepdims=True)
            idx = jnp.min(jnp.where(val == top, bi, n_blk), axis=0, keepdims=True)
            pick = bi == idx
            sel = jnp.where(pick, 1.0, sel)
            val = jnp.where(pick, -2.0, val)
        chosen = jnp.where(always, 1.0, jnp.where(free, sel, 0.0)) > 0.0

        bterm = jnp.concatenate(
            [jnp.where(chosen, float(slopes[g * hg + h]) * SEL_BLOCK * rel_blk, NEG) for h in range(hg)],
            axis=1).astype(BF16)
        parts = [qt, bterm]
        if n_blk < HEAD_DIM:
            parts.append(jnp.zeros((HEAD_DIM - n_blk, GQ), BF16))
        parts.append(qslope_ref[g])
        qaug = jnp.concatenate(parts, axis=0)
        return o_cmp, qaug, jnp.where(chosen, 1.0, 0.0)

    def sel_scores(g, qaug, chunk, mask):
        rows = pl.ds(pl.multiple_of(chunk * SEL_CHUNK, SEL_CHUNK), SEL_CHUNK)
        s = _dot(kaug_ref[g, rows, :], qaug)
        return s if mask is None else s + mask

    def sel_update(g, s, chunk, carry):
        m, acc = carry
        m_new = jnp.maximum(m, jnp.max(s, axis=0, keepdims=True))
        alpha = jnp.exp(m - m_new)
        p = jnp.exp(s - m_new).astype(BF16)
        per_chunk = SEL_CHUNK // LANES
        vt = jnp.concatenate([vst_ref[chunk * per_chunk + k, g * HEAD_DIM:(g + 1) * HEAD_DIM, :]
                              for k in range(per_chunk)], axis=1)
        return m_new, acc * alpha + _dot(with_ones(vt), p)

    def sel_steps(chunk, mask, carries):
        scores = [sel_scores(g, qaug[g], chunk, mask) for g in groups]
        return tuple(sel_update(g, scores[g], chunk, carries[g]) for g in groups)

    def window_scores(g, qw):
        rows = pl.ds(pl.multiple_of(qi * Q_BLOCK, Q_BLOCK), WIN_KEYS)
        return _dot(kwaug_ref[g, rows, :], qw)

    def window(g, s):
        s = s + tblw_ref[g]
        m = jnp.max(s, axis=0, keepdims=True)
        p = jnp.exp2(s - m).astype(BF16)
        back = WINDOW // LANES
        ahead = Q_BLOCK // LANES
        vt = jnp.concatenate([vwt_ref[jnp.maximum(qi * ahead - back + k, 0), g * HEAD_DIM:(g + 1) * HEAD_DIM, :]
                              for k in range(back + ahead)], axis=1)
        acc = _dot(with_ones(vt), p)
        return acc[0:HEAD_DIM] * (1.0 / acc[HEAD_DIM:HEAD_DIM + 1])

    groups = tuple(range(KV_GROUPS))
    qt, qw = zip(*[queries(g) for g in groups])
    s_cmp = [_dot(kc_ref[g], qw[g]) for g in groups]
    s_win = [window_scores(g, qw[g]) for g in groups]
    o_cmp, qaug, picked = zip(*[select(g, qt[g], s_cmp[g]) for g in groups])
    o_win = [window(g, s_win[g]) for g in groups]

    per = SEL_CHUNK // Q_BLOCK
    blk_per_chunk = SEL_CHUNK // SEL_BLOCK
    n_full = qi // per
    wanted = picked[0]
    for g in groups[1:]:
        wanted = wanted + picked[g]
    for c in range(n_blk // blk_per_chunk - 1):
        hit = jnp.sum(wanted[c * blk_per_chunk:(c + 1) * blk_per_chunk, :])
        need_ref[c] = (hit > 0.0).astype(jnp.int32)
    for g in groups:
        m_ref[g] = jnp.full((8, GQ), NEG, F32)
        acc_ref[g] = jnp.zeros((HEAD_DIM + ONES_ROWS, GQ), F32)

    def load_carries():
        return tuple((m_ref[g, 0:1, :], acc_ref[g]) for g in groups)

    def full_chunk(c, carry):
        @pl.when(need_ref[c] > 0)
        def _():
            for g, (m, acc) in enumerate(sel_steps(c, None, load_carries())):
                m_ref[g, 0:1, :] = m
                acc_ref[g] = acc
        return carry

    lax.fori_loop(0, n_full, full_chunk, 0)
    tail_mask = jnp.concatenate([tailm_ref[qi % per]] * hg, axis=1)
    carries = sel_steps(n_full, tail_mask, load_carries())
    o_slc = [acc[0:HEAD_DIM] * (1.0 / acc[HEAD_DIM:HEAD_DIM + 1]) for _, acc in carries]

    for g in groups:
        for pr in range(hg // 2):
            outs = []
            for h in (2 * pr, 2 * pr + 1):
                hh = g * hg + h
                lanes = slice(h * Q_BLOCK, (h + 1) * Q_BLOCK)
                out = gates[3 * hh:3 * hh + 1, :] * o_cmp[g][:, lanes]
                out = out + gates[3 * hh + 1:3 * hh + 2, :] * o_slc[g][:, lanes]
                out = out + gates[3 * hh + 2:3 * hh + 3, :] * o_win[g][:, lanes]
                outs.append(out)
            pair = jnp.concatenate(outs, axis=0)
            o_ref[:, (g * 2 + pr) * LANES:(g * 2 + pr + 1) * LANES] = pair.T.astype(BF16)


def _nsa(qt, q2t, gt, ks, kw, vst, vwt, kc, vct, tables):
    b, _, t = qt.shape
    n_blk = t // SEL_BLOCK
    n_rows = t // CMP_STRIDE
    assert t % SEL_CHUNK == 0 and SEL_CHUNK % Q_BLOCK == 0
    assert N_SEL <= n_blk <= HEAD_DIM and n_blk % 16 == 0
    seq_spec = pl.BlockSpec((None, t, LANES), lambda i, j: (i, 0, 0))
    chunked = pl.BlockSpec((None, t // LANES, KV_WIDTH, LANES), lambda i, j: (i, 0, 0, 0))
    return pl.pallas_call(
        functools.partial(_nsa_kernel, t=t, n_blk=n_blk),
        grid=(b, t // Q_BLOCK),
        in_specs=[pl.BlockSpec((None, NSA_WIDTH, Q_BLOCK), lambda i, j: (i, 0, j)),
                  pl.BlockSpec((None, NSA_WIDTH, Q_BLOCK), lambda i, j: (i, 0, j)),
                  pl.BlockSpec((None, GATE_ROWS, Q_BLOCK), lambda i, j: (i, 0, j)),
                  seq_spec, seq_spec, chunked, chunked,
                  pl.BlockSpec((None, KV_GROUPS, n_rows, LANES), lambda i, j: (i, 0, 0, 0)),
                  pl.BlockSpec((None, KV_GROUPS, HEAD_DIM, n_rows), lambda i, j: (i, 0, 0, 0))]
                 + [_const_spec(a.shape) for a in tables],
        out_specs=pl.BlockSpec((None, Q_BLOCK, NSA_WIDTH), lambda i, j: (i, j, 0)),
        out_shape=jax.ShapeDtypeStruct((b, t, NSA_WIDTH), BF16),
        scratch_shapes=[pltpu.VMEM((KV_GROUPS, t, 2 * LANES), BF16),
                        pltpu.VMEM((KV_GROUPS, t + WINDOW, LANES), BF16),
                        pltpu.VMEM((KV_GROUPS, 8, GQ), F32),
                        pltpu.VMEM((KV_GROUPS, HEAD_DIM + ONES_ROWS, GQ), F32),
                        pltpu.SMEM((t // SEL_CHUNK,), jnp.int32)],
        compiler_params=pltpu.CompilerParams(
            dimension_semantics=("arbitrary", "arbitrary"), vmem_limit_bytes=VMEM_LIMIT),
        name="nsa",
    )(qt, q2t, gt, ks, kw, vst, vwt, kc, vct, *tables)


def _mix_ffn_kernel(x_ref, o_ref, u_ref, halo_ref, wp_ref, sc_ref, wo_ref, g2_ref, wg_ref, wu_ref, wd_ref,
                    *rest, tm, final):
    if final:
        fn_ref, y_ref, mix_ref, h_ref, acc_ref = rest
    else:
        y_ref, mix_ref, h_ref, acc_ref = rest
    ti = pl.program_id(1)
    mix_ref[:, 0:NSA_WIDTH] = o_ref[...]
    halo = jnp.where(ti > 0, halo_ref[...], 0.0)
    pos = ti * tm + lax.broadcasted_iota(jnp.int32, (tm, 1), 0)
    for gi, w in enumerate(POOL_WIDTHS):
        lanes = slice(gi * POOL_GROUP, (gi + 1) * POOL_GROUP)
        u = u_ref[:, lanes]
        ext = jnp.concatenate([halo[:, lanes], u], axis=0)
        run = ext
        step = 1
        while step < w:
            run = run + pltpu.roll(run, step, axis=0)
            step *= 2
        cnt = jnp.minimum(pos + 1, w).astype(F32)
        y = run[POOL_HALO:, :] / cnt - u
        yo = _dot(y.astype(BF16), wp_ref[gi]) * sc_ref[:, lanes]
        mix_ref[:, NSA_WIDTH + gi * POOL_GROUP:NSA_WIDTH + (gi + 1) * POOL_GROUP] = yo.astype(BF16)
    x2 = x_ref[...] + _dot(mix_ref[...], wo_ref[...])
    y = _swiglu_residual(x2, g2_ref, wg_ref, wu_ref, wd_ref, h_ref, acc_ref)
    if final:
        y = _rms(y, fn_ref[...])
    y_ref[...] = y


def _mix_ffn(x3, o_nsa, u, wp, sc, wo, g2, wg, wu, wd, final_g=None, tm=512):
    b, t, d = x3.shape
    per = tm // POOL_HALO
    final = final_g is not None
    tail_specs = [_resident((1, d))] if final else []
    tail_args = [final_g] if final else []
    return pl.pallas_call(
        functools.partial(_mix_ffn_kernel, tm=tm, final=final),
        grid=(b, t // tm),
        in_specs=[pl.BlockSpec((None, tm, d), lambda i, j: (i, j, 0)),
                  pl.BlockSpec((None, tm, NSA_WIDTH), lambda i, j: (i, j, 0)),
                  pl.BlockSpec((None, tm, POOL_CH), lambda i, j: (i, j, 0)),
                  pl.BlockSpec((None, POOL_HALO, POOL_CH), lambda i, j: (i, jnp.maximum(j * per - 1, 0), 0)),
                  _resident(wp.shape), _resident(sc.shape), _resident(wo.shape),
                  _resident((1, d)), _resident(wg.shape), _resident(wu.shape), _resident(wd.shape)] + tail_specs,
        out_specs=pl.BlockSpec((None, tm, d), lambda i, j: (i, j, 0)),
        out_shape=jax.ShapeDtypeStruct((b, t, d), F32),
        scratch_shapes=[pltpu.VMEM((tm, NSA_WIDTH + POOL_CH), BF16),
                        pltpu.VMEM((tm, d), BF16), pltpu.VMEM((tm, d), F32)],
        compiler_params=pltpu.CompilerParams(
            dimension_semantics=("parallel", "parallel"), vmem_limit_bytes=VMEM_LIMIT),
        name="mix_ffn_final" if final else "mix_ffn",
    )(x3, o_nsa, u, u, wp, sc, wo, g2, wg, wu, wd, *tail_args)


def kernel(x, ffn1_norm, ffn1_wg, ffn1_wu, ffn1_wd, mix_norm, w_in, cmp_pe_k, cmp_wk1, cmp_wk2,
           cmp_pe_v, cmp_wv1, cmp_wv2, pool_w, pool_scale, w_out, ffn2_norm, ffn2_wg, ffn2_wu,
           ffn2_wd, final_norm):
    b, t, d = x.shape
    depth = ffn1_norm.shape[0]
    tables = _nsa_tables(t)
    bf = lambda a: a.astype(BF16)
    for l in range(depth):
        x, kcin, vcin, ks, kw, u, qt, q2t, vst, vwt, gt = _ffn_proj(
            x, ffn1_norm[l][None], bf(ffn1_wg[l]), bf(ffn1_wu[l]), bf(ffn1_wd[l]),
            mix_norm[l][None], *_split_w_in(w_in[l]))
        kc, vct = _compress(kcin, vcin,
                            *_compress_weights(cmp_pe_k[l], cmp_wk1[l], cmp_wk2[l], False),
                            *_compress_weights(cmp_pe_v[l], cmp_wv1[l], cmp_wv2[l], True))
        o_nsa = _nsa(qt, q2t, gt, ks, kw, vst, vwt, kc, vct, tables)
        x = _mix_ffn(x, o_nsa, u, bf(pool_w[l]), pool_scale[l][None], bf(w_out[l]),
                     ffn2_norm[l][None], bf(ffn2_wg[l]), bf(ffn2_wu[l]), bf(ffn2_wd[l]),
                     final_g=final_norm[None] if l == depth - 1 else None)
    return x
```

```python
import functools

import numpy as np
import jax
import jax.numpy as jnp
from jax import lax
from jax.experimental import pallas as pl
from jax.experimental.pallas import tpu as pltpu

HEAD_DIM = 64
NSA_HEADS = 8
KV_GROUPS = 2
HEADS_PER_GROUP = NSA_HEADS // KV_GROUPS
NSA_WIDTH = NSA_HEADS * HEAD_DIM
KV_WIDTH = KV_GROUPS * HEAD_DIM
CMP_BLOCK = 32
CMP_STRIDE = 16
CMP_HIDDEN = 128
SEL_BLOCK = 64
N_SEL = 8
WINDOW = 512
Q_BLOCK = 256
POOL_WIDTHS = (2, 4, 8, 16)
POOL_GROUP = 128
POOL_CH = POOL_GROUP * len(POOL_WIDTHS)
N_GATES = 3 * NSA_HEADS
GATE_ROWS = 32
EPS = 1e-6

LANES = 128
GQ = HEADS_PER_GROUP * Q_BLOCK
SEL_CHUNK = 256
FIRST_KEYS = 128
WIN_KEYS = WINDOW + Q_BLOCK
ONES_ROWS = 16
POOL_HALO = 16
FF_CHUNK = 256
LOG2E = 1.4426950408889634
NEG = -1e30
VMEM_LIMIT = 56 * 1024 * 1024

F32 = jnp.float32
BF16 = jnp.bfloat16


def _slopes():
    n = NSA_HEADS
    return (2.0 ** (-8.0 * np.arange(1, n + 1) / n)).astype(np.float32)


def _dot(a, b):
    return jnp.dot(a, b, preferred_element_type=F32)


def _dot_nt(a, b):
    return lax.dot_general(a, b, (((1,), (1,)), ((), ())), preferred_element_type=F32)


def _rms(x, g):
    return x * lax.rsqrt(jnp.mean(x * x, axis=-1, keepdims=True) + EPS) * g


def _const_spec(shape):
    nd = len(shape)
    return pl.BlockSpec(shape, lambda *_: (0,) * nd)


def _swiglu_residual(x, g_ref, wg_ref, wu_ref, wd_ref, h_ref, acc_ref):
    h_ref[...] = _rms(x, g_ref[...]).astype(BF16)
    for c in range(wg_ref.shape[1] // FF_CHUNK):
        cols = slice(c * FF_CHUNK, (c + 1) * FF_CHUNK)
        h = h_ref[...]
        gate = _dot(h, wg_ref[:, cols])
        up = _dot(h, wu_ref[:, cols])
        a = (gate * jax.nn.sigmoid(gate) * up).astype(BF16)
        part = _dot(a, wd_ref[cols, :])
        if c == 0:
            acc_ref[...] = part
        else:
            acc_ref[...] += part
    return x + 0.5 * acc_ref[...]


def _resident(shape):
    nd = len(shape)
    return pl.BlockSpec(shape, lambda *_: (0,) * nd, pipeline_mode=pl.Buffered(1))


_TOK_KC = (0, 128)
_TOK_VC = (128, 256)
_TOK_KS = (256, 384)
_TOK_KW = (384, 512)
_TOK_U = (512, 1024)
_FEAT_Q = (0, 512)
_FEAT_VS = (512, 640)
_FEAT_VW = (640, 768)
_FEAT_GT = (768, 768 + GATE_ROWS)


def _ffn_proj_kernel(x_ref, g1_ref, wg_ref, wu_ref, wd_ref, g_ref, wt_ref, wf_ref,
                     x1_ref, kcin_ref, vcin_ref, ks_ref, kw_ref, u_ref,
                     qt_ref, q2t_ref, vst_ref, vwt_ref, gt_ref, h_ref, acc_ref, *, tm):
    x1 = _swiglu_residual(x_ref[...], g1_ref, wg_ref, wu_ref, wd_ref, h_ref, acc_ref)
    x1_ref[...] = x1
    h = _rms(x1, g_ref[...]).astype(BF16)

    def tok(lo_hi):
        return _dot(h, wt_ref[:, lo_hi[0]:lo_hi[1]])

    kcin_ref[...] = tok(_TOK_KC)
    vcin_ref[...] = tok(_TOK_VC)
    ks_ref[...] = tok(_TOK_KS).astype(BF16)
    kw_ref[...] = tok(_TOK_KW).astype(BF16)
    u_ref[...] = tok(_TOK_U)
    zt = _dot_nt(wf_ref[...], h)
    qt_ref[...] = (zt[_FEAT_Q[0]:_FEAT_Q[1]] * (HEAD_DIM ** -0.5)).astype(BF16)
    q2t_ref[...] = (zt[_FEAT_Q[0]:_FEAT_Q[1]] * (HEAD_DIM ** -0.5 * LOG2E)).astype(BF16)
    gt_ref[...] = zt[_FEAT_GT[0]:_FEAT_GT[1]]
    for c in range(tm // LANES):
        cols = slice(c * LANES, (c + 1) * LANES)
        vst_ref[c] = zt[_FEAT_VS[0]:_FEAT_VS[1], cols].astype(BF16)
        vwt_ref[c] = zt[_FEAT_VW[0]:_FEAT_VW[1], cols].astype(BF16)


def _ffn_proj(x3, g1, wg, wu, wd, g, w_tok, w_feat, tm=512):
    b, t, d = x3.shape
    nck = tm // LANES
    tok = lambda wd: pl.BlockSpec((None, tm, wd), lambda i, j: (i, j, 0))
    chunked = pl.BlockSpec((None, nck, KV_WIDTH, LANES), lambda i, j: (i, j, 0, 0))
    return pl.pallas_call(
        functools.partial(_ffn_proj_kernel, tm=tm),
        grid=(b, t // tm),
        in_specs=[tok(d), _resident((1, d)), _resident(wg.shape), _resident(wu.shape), _resident(wd.shape),
                  _resident((1, d)), _resident(w_tok.shape), _resident(w_feat.shape)],
        out_specs=[tok(d), tok(KV_WIDTH), tok(KV_WIDTH), tok(KV_WIDTH), tok(KV_WIDTH), tok(POOL_CH),
                   pl.BlockSpec((None, NSA_WIDTH, tm), lambda i, j: (i, 0, j)),
                   pl.BlockSpec((None, NSA_WIDTH, tm), lambda i, j: (i, 0, j)),
                   chunked, chunked,
                   pl.BlockSpec((None, GATE_ROWS, tm), lambda i, j: (i, 0, j))],
        out_shape=[jax.ShapeDtypeStruct((b, t, d), F32),
                   jax.ShapeDtypeStruct((b, t, KV_WIDTH), F32),
                   jax.ShapeDtypeStruct((b, t, KV_WIDTH), F32),
                   jax.ShapeDtypeStruct((b, t, KV_WIDTH), BF16),
                   jax.ShapeDtypeStruct((b, t, KV_WIDTH), BF16),
                   jax.ShapeDtypeStruct((b, t, POOL_CH), F32),
                   jax.ShapeDtypeStruct((b, NSA_WIDTH, t), BF16),
                   jax.ShapeDtypeStruct((b, NSA_WIDTH, t), BF16),
                   jax.ShapeDtypeStruct((b, t // LANES, KV_WIDTH, LANES), BF16),
                   jax.ShapeDtypeStruct((b, t // LANES, KV_WIDTH, LANES), BF16),
                   jax.ShapeDtypeStruct((b, GATE_ROWS, t), F32)],
        scratch_shapes=[pltpu.VMEM((tm, d), BF16), pltpu.VMEM((tm, d), F32)],
        compiler_params=pltpu.CompilerParams(
            dimension_semantics=("parallel", "parallel"), vmem_limit_bytes=VMEM_LIMIT),
        name="ffn_proj",
    )(x3, g1, wg, wu, wd, g, w_tok, w_feat)


def _split_w_in(w):
    q0, kv0 = 0, NSA_WIDTH
    col = lambda i: slice(kv0 + i * KV_WIDTH, kv0 + (i + 1) * KV_WIDTH)
    g0 = kv0 + 6 * KV_WIDTH
    w_tok = jnp.concatenate([w[:, col(0)], w[:, col(1)], w[:, col(2)], w[:, col(4)],
                             w[:, g0 + N_GATES:]], axis=1).astype(BF16)
    pad = jnp.zeros((w.shape[0], GATE_ROWS - N_GATES), w.dtype)
    w_feat = jnp.concatenate([w[:, q0:NSA_WIDTH], w[:, col(3)], w[:, col(5)],
                              w[:, g0:g0 + N_GATES], pad], axis=1).T.astype(BF16)
    return w_tok, w_feat


def _gelu_tanh(x):
    c = np.float32(np.sqrt(2.0 / np.pi))
    return 0.5 * x * (1.0 + jnp.tanh(c * (x + 0.044715 * (x * x * x))))


def _compress_kernel(kcin_ref, vcin_ref, pek_ref, wk1_ref, wk2_ref, pev_ref, wv1_ref, wv2_ref,
                     kc_ref, vct_ref, *, n_rows):
    nl = CMP_STRIDE
    hid2 = KV_GROUPS * CMP_HIDDEN

    def hidden(src_ref, pe_ref, w1_ref):
        acc = jnp.zeros((n_rows + 16, 2 * hid2), F32)
        for l in range(nl):
            xl = src_ref[pl.ds(l, n_rows, stride=nl), :]
            pa = jnp.broadcast_to(pe_ref[l:l + 1, :], (8, KV_WIDTH))
            pb = jnp.broadcast_to(pe_ref[nl + l:nl + l + 1, :], (8, KV_WIDTH))
            lhs = jnp.concatenate([xl, pa, pb], axis=0).astype(BF16)
            acc = acc + _dot(lhs, w1_ref[l])
        sa = acc[0:n_rows, 0:hid2]
        sb = acc[0:n_rows, hid2:2 * hid2]
        bias = acc[n_rows:n_rows + 1, 0:hid2] + acc[n_rows + 8:n_rows + 9, hid2:2 * hid2]
        hcur = sa + pltpu.roll(sb, n_rows - 1, axis=0) + bias
        return _gelu_tanh(hcur).astype(BF16)

    hk = hidden(kcin_ref, pek_ref, wk1_ref)
    kc = _dot(hk, wk2_ref[...])
    for g in range(KV_GROUPS):
        kc_ref[g] = kc[:, g * LANES:(g + 1) * LANES].astype(BF16)
    hv = hidden(vcin_ref, pev_ref, wv1_ref)
    for g in range(KV_GROUPS):
        vct_ref[g] = _dot_nt(wv2_ref[g], hv).astype(BF16)


def _compress(kcin, vcin, pek2, wk1b, wk2b, pev2, wv1b, wv2t):
    b, t, _ = kcin.shape
    n_rows = t // CMP_STRIDE
    seq_spec = pl.BlockSpec((None, t, KV_WIDTH), lambda i: (i, 0, 0))
    return pl.pallas_call(
        functools.partial(_compress_kernel, n_rows=n_rows),
        grid=(b,),
        in_specs=[seq_spec, seq_spec,
                  _const_spec(pek2.shape), _const_spec(wk1b.shape), _const_spec(wk2b.shape),
                  _const_spec(pev2.shape), _const_spec(wv1b.shape), _const_spec(wv2t.shape)],
        out_specs=[pl.BlockSpec((None, KV_GROUPS, n_rows, LANES), lambda i: (i, 0, 0, 0)),
                   pl.BlockSpec((None, KV_GROUPS, HEAD_DIM, n_rows), lambda i: (i, 0, 0, 0))],
        out_shape=[jax.ShapeDtypeStruct((b, KV_GROUPS, n_rows, LANES), BF16),
                   jax.ShapeDtypeStruct((b, KV_GROUPS, HEAD_DIM, n_rows), BF16)],
        compiler_params=pltpu.CompilerParams(
            dimension_semantics=("parallel",), vmem_limit_bytes=VMEM_LIMIT),
        name="compress",
    )(kcin, vcin, pek2, wk1b, wk2b, pev2, wv1b, wv2t)


def _compress_weights(pe, w1, w2, value_layout):
    nl, dh, hid = CMP_STRIDE, HEAD_DIM, CMP_HIDDEN
    w1r = w1.reshape(2, nl, dh, hid)
    z = jnp.zeros((nl, dh, hid), w1.dtype)
    row_g0 = jnp.concatenate([w1r[0], z, w1r[1], z], axis=-1)
    row_g1 = jnp.concatenate([z, w1r[0], z, w1r[1]], axis=-1)
    w1b = jnp.concatenate([row_g0, row_g1], axis=1).astype(BF16)
    pe2 = jnp.concatenate([pe, pe], axis=-1)
    zz = jnp.zeros((hid, dh), w2.dtype)
    if value_layout:
        w2b = jnp.stack([jnp.concatenate([w2, zz], axis=0).T,
                         jnp.concatenate([zz, w2], axis=0).T]).astype(BF16)
    else:
        top = jnp.concatenate([w2, zz, zz, zz], axis=-1)
        bot = jnp.concatenate([zz, zz, w2, zz], axis=-1)
        w2b = jnp.concatenate([top, bot], axis=0).astype(BF16)
    return pe2, w1b, w2b


def _nsa_tables(t):
    slopes = _slopes().reshape(KV_GROUPS, HEADS_PER_GROUP)
    n_rows = t // CMP_STRIDE
    n_blk = t // SEL_BLOCK
    i = np.arange(Q_BLOCK)
    c = np.arange(n_rows)
    d0c = (i[None, :] - (c[:, None] * CMP_STRIDE + CMP_BLOCK - 1)).astype(np.int32)
    tblc = -LOG2E * slopes[:, None, :, None].astype(np.float64) * d0c[None, :, None, :]
    tblc = tblc.astype(np.float32)
    tblc = tblc.reshape(KV_GROUPS, n_rows, GQ)
    d0c = np.tile(d0c, (1, HEADS_PER_GROUP))
    j = np.arange(WIN_KEYS)
    dist = i[None, :] + WINDOW - j[:, None]
    ok = (dist >= 0) & (dist < WINDOW)
    tblw = np.where(ok[None, :, None, :],
                    -LOG2E * slopes[:, None, :, None].astype(np.float64) * dist[None, :, None, :],
                    NEG).astype(np.float32).reshape(KV_GROUPS, WIN_KEYS, GQ)
    jj = np.arange(SEL_CHUNK)
    per = SEL_CHUNK // Q_BLOCK
    tailm = np.stack([np.where(jj[:, None] > i[None, :] + Q_BLOCK * r, np.float32(NEG), np.float32(0))
                      for r in range(per)]).astype(np.float32)
    ci = c[None, :] * CMP_STRIDE
    sj = np.arange(n_blk)[:, None] * SEL_BLOCK
    n_cmp = (t - CMP_BLOCK) // CMP_STRIDE + 1
    ov = ((ci <= sj + SEL_BLOCK - 1) & (ci + CMP_BLOCK - 1 >= sj) & (c[None, :] < n_cmp))
    ov = ov.astype(np.float32)
    pos = np.arange(t)
    ktag = np.zeros((t, 2 * LANES), np.float32)
    ktag[pos, HEAD_DIM + pos // SEL_BLOCK] = 1.0
    ktag[:, LANES] = pos % SEL_BLOCK
    qflag = np.zeros((HEAD_DIM, GQ), np.float32)
    qflag[0, :] = NEG
    qslope = np.zeros((KV_GROUPS, LANES, GQ), np.float32)
    qslope[:, 0, :] = np.repeat(slopes, Q_BLOCK, axis=1)
    return (jnp.asarray(tblc), jnp.asarray(d0c), jnp.asarray(tblw), jnp.asarray(tailm),
            jnp.asarray(ov, dtype=BF16), jnp.asarray(ktag, dtype=BF16),
            jnp.asarray(qflag, dtype=BF16), jnp.asarray(qslope, dtype=BF16))


def _nsa_kernel(qt_ref, q2t_ref, gt_ref, ks_ref, kw_ref, vst_ref, vwt_ref, kc_ref, vct_ref,
                tblc_ref, d0c_ref, tblw_ref, tailm_ref, ov_ref, ktag_ref, qflag_ref, qslope_ref,
                o_ref,
                kaug_ref, kwaug_ref, m_ref, acc_ref, need_ref, *, t, n_blk):
    qi = pl.program_id(1)
    slopes = _slopes()
    hg = HEADS_PER_GROUP
    prep_rows = 512

    @pl.when(qi == 0)
    def _prep():
        low = lax.broadcasted_iota(jnp.int32, (prep_rows, LANES), 1) < HEAD_DIM
        flag = jnp.where(lax.broadcasted_iota(jnp.int32, (WINDOW, LANES), 1) == HEAD_DIM, 1.0, 0.0)
        for g in range(KV_GROUPS):
            kwaug_ref[g, 0:WINDOW, :] = flag.astype(BF16)

        def body(c, carry):
            r0 = pl.multiple_of(c * prep_rows, prep_rows)
            rows = pl.ds(r0, prep_rows)
            rows_w = pl.ds(r0 + WINDOW, prep_rows)
            tag = ktag_ref[rows, 0:LANES].astype(F32)
            ks = ks_ref[rows, :].astype(F32)
            kw = kw_ref[rows, :].astype(F32)
            ks_sw = pltpu.roll(ks, HEAD_DIM, axis=1)
            kw_sw = pltpu.roll(kw, HEAD_DIM, axis=1)
            for g in range(KV_GROUPS):
                kaug_ref[g, rows, 0:LANES] = jnp.where(low, ks if g == 0 else ks_sw, tag).astype(BF16)
                kaug_ref[g, rows, LANES:2 * LANES] = ktag_ref[rows, LANES:2 * LANES]
                kwaug_ref[g, rows_w, :] = jnp.where(low, kw if g == 0 else kw_sw, 0.0).astype(BF16)
            return carry

        lax.fori_loop(0, t // prep_rows, body, 0)

    gates = jax.nn.sigmoid(gt_ref[...])
    bi = lax.broadcasted_iota(jnp.int32, (n_blk, Q_BLOCK), 0)
    lq = lax.broadcasted_iota(jnp.int32, (n_blk, Q_BLOCK), 1)
    cur = (Q_BLOCK // SEL_BLOCK) * qi + lax.shift_right_logical(lq, SEL_BLOCK.bit_length() - 1)
    valid = bi <= cur
    near = (bi == 0) | (bi >= cur - 1)
    always = valid & (near | (cur < N_SEL))
    free = valid & jnp.logical_not(near)
    later = bi >= FIRST_KEYS // SEL_BLOCK
    rel_blk = (bi - cur).astype(F32)

    def with_ones(vt):
        first = lax.broadcasted_iota(jnp.int32, (ONES_ROWS, vt.shape[1]), 0) == 0
        return jnp.concatenate([vt, jnp.where(first, 1.0, 0.0).astype(BF16)], axis=0)

    def queries(g):
        heads = lambda ref: jnp.concatenate([ref[(g * hg + h) * HEAD_DIM:(g * hg + h + 1) * HEAD_DIM, :]
                                             for h in range(hg)], axis=1)
        return heads(qt_ref), jnp.concatenate([heads(q2t_ref), qflag_ref[...]], axis=0)

    def select(g, qt, s):
        s = s + tblc_ref[g]
        s = jnp.where(d0c_ref[...] + Q_BLOCK * qi >= 0, s, NEG)
        m = jnp.max(s, axis=0, keepdims=True)
        m = jnp.where(m < 0.5 * NEG, 0.0, m)
        p = jnp.exp2(s - m)
        l = jnp.sum(p, axis=0, keepdims=True)
        pn = p * (1.0 / jnp.maximum(l, 1e-30))
        o_cmp = _dot(vct_ref[g], pn.astype(BF16))
        psum = pn[:, 0:Q_BLOCK]
        for h in range(1, hg):
            psum = psum + pn[:, h * Q_BLOCK:(h + 1) * Q_BLOCK]

        p_hi = psum.astype(BF16)
        p_lo = (psum - p_hi.astype(F32)).astype(BF16)
        imp = _dot(ov_ref[...], p_hi) + _dot(ov_ref[...], p_lo)
        val = jnp.where(free, imp, -1.0)
        sel = jnp.zeros((n_blk, Q_BLOCK), F32)
        for _ in range(N_SEL - 3):
            top = jnp.max(val, axis=0, keepdims=True)
            idx = jnp.min(jnp.where(val == top, bi, n_blk), axis=0, keepdims=True)
            pick = bi == idx
            sel = jnp.where(pick, 1.0, sel)
            val = jnp.where(pick, -2.0, val)
        chosen = jnp.where(always, 1.0, jnp.where(free, sel, 0.0)) > 0.0

        def tagged(blocks):
            bterm = jnp.concatenate(
                [jnp.where(blocks, float(slopes[g * hg + h]) * SEL_BLOCK * rel_blk, NEG) for h in range(hg)],
                axis=1).astype(BF16)
            parts = [qt, bterm]
            if n_blk < HEAD_DIM:
                parts.append(jnp.zeros((HEAD_DIM - n_blk, GQ), BF16))
            parts.append(qslope_ref[g])
            return jnp.concatenate(parts, axis=0)

        return o_cmp, tagged(chosen), tagged(chosen & later), jnp.where(chosen & later, 1.0, 0.0)

    def sel_scores(g, q, chunk, mask):
        keys = kaug_ref[g, pl.ds(pl.multiple_of(chunk * SEL_CHUNK, SEL_CHUNK), SEL_CHUNK), :]
        if mask is None:
            return _dot(keys, q)
        keys = jnp.concatenate([keys, kaug_ref[g, 0:FIRST_KEYS, :]], axis=0)
        return _dot(keys, q) + mask

    def sel_update(g, s, chunk, carry, with_first):
        m, acc = carry
        m_new = jnp.maximum(m, jnp.max(s, axis=0, keepdims=True))
        alpha = jnp.exp(m - m_new)
        p = jnp.exp(s - m_new).astype(BF16)
        per_chunk = SEL_CHUNK // LANES
        slabs = [chunk * per_chunk + k for k in range(per_chunk)]
        if with_first:
            slabs += list(range(FIRST_KEYS // LANES))
        vt = jnp.concatenate([vst_ref[k, g * HEAD_DIM:(g + 1) * HEAD_DIM, :] for k in slabs], axis=1)
        return m_new, acc * alpha + _dot(with_ones(vt), p)

    def sel_steps(chunk, mask, carries):
        q = qaug if mask is not None else qfar
        scores = [sel_scores(g, q[g], chunk, mask) for g in groups]
        return tuple(sel_update(g, scores[g], chunk, carries[g], mask is not None) for g in groups)

    def window_scores(g, qw):
        rows = pl.ds(pl.multiple_of(qi * Q_BLOCK, Q_BLOCK), WIN_KEYS)
        return _dot(kwaug_ref[g, rows, :], qw)

    def window(g, s):
        s = s + tblw_ref[g]
        m = jnp.max(s, axis=0, keepdims=True)
        p = jnp.exp2(s - m).astype(BF16)
        back = WINDOW // LANES
        ahead = Q_BLOCK // LANES
        vt = jnp.concatenate([vwt_ref[jnp.maximum(qi * ahead - back + k, 0), g * HEAD_DIM:(g + 1) * HEAD_DIM, :]
                              for k in range(back + ahead)], axis=1)
        acc = _dot(with_ones(vt), p)
        return acc[0:HEAD_DIM] * (1.0 / acc[HEAD_DIM:HEAD_DIM + 1])

    groups = tuple(range(KV_GROUPS))
    qt, qw = zip(*[queries(g) for g in groups])
    s_cmp = [_dot(kc_ref[g], qw[g]) for g in groups]
    s_win = [window_scores(g, qw[g]) for g in groups]
    sel_out, o_win = [], []
    for g in groups:
        sel_out.append(select(g, qt[g], s_cmp[g]))
        o_win.append(window(g, s_win[g]))
    o_cmp, qaug, qfar, picked = zip(*sel_out)

    per = SEL_CHUNK // Q_BLOCK
    blk_per_chunk = SEL_CHUNK // SEL_BLOCK
    n_full = qi // per
    wanted = picked[0]
    for g in groups[1:]:
        wanted = wanted + picked[g]
    for c in range(n_blk // blk_per_chunk - 1):
        hit = jnp.sum(wanted[c * blk_per_chunk:(c + 1) * blk_per_chunk, :])
        need_ref[c] = (hit > 0.0).astype(jnp.int32)
    for g in groups:
        m_ref[g] = jnp.full((8, GQ), NEG, F32)
        acc_ref[g] = jnp.zeros((HEAD_DIM + ONES_ROWS, GQ), F32)

    def load_carries():
        return tuple((m_ref[g, 0:1, :], acc_ref[g]) for g in groups)

    def full_chunk(c, carry):
        @pl.when(need_ref[c] > 0)
        def _():
            for g, (m, acc) in enumerate(sel_steps(c, None, load_carries())):
                m_ref[g, 0:1, :] = m
                acc_ref[g] = acc
        return carry

    lax.fori_loop(0, n_full, full_chunk, 0)
    first_mask = jnp.where(n_full == 0, NEG, 0.0) + jnp.zeros((FIRST_KEYS, GQ), F32)
    tail_mask = jnp.concatenate([jnp.concatenate([tailm_ref[qi % per]] * hg, axis=1), first_mask], axis=0)
    carries = sel_steps(n_full, tail_mask, load_carries())
    o_slc = [acc[0:HEAD_DIM] * (1.0 / acc[HEAD_DIM:HEAD_DIM + 1]) for _, acc in carries]

    for g in groups:
        for pr in range(hg // 2):
            outs = []
            for h in (2 * pr, 2 * pr + 1):
                hh = g * hg + h
                lanes = slice(h * Q_BLOCK, (h + 1) * Q_BLOCK)
                out = gates[3 * hh:3 * hh + 1, :] * o_cmp[g][:, lanes]
                out = out + gates[3 * hh + 1:3 * hh + 2, :] * o_slc[g][:, lanes]
                out = out + gates[3 * hh + 2:3 * hh + 3, :] * o_win[g][:, lanes]
                outs.append(out)
            pair = jnp.concatenate(outs, axis=0)
            o_ref[:, (g * 2 + pr) * LANES:(g * 2 + pr + 1) * LANES] = pair.T.astype(BF16)


def _nsa(qt, q2t, gt, ks, kw, vst, vwt, kc, vct, tables):
    b, _, t = qt.shape
    n_blk = t // SEL_BLOCK
    n_rows = t // CMP_STRIDE
    assert t % SEL_CHUNK == 0 and SEL_CHUNK % Q_BLOCK == 0
    assert N_SEL <= n_blk <= HEAD_DIM and n_blk % 16 == 0
    seq_spec = pl.BlockSpec((None, t, LANES), lambda i, j: (i, 0, 0))
    chunked = pl.BlockSpec((None, t // LANES, KV_WIDTH, LANES), lambda i, j: (i, 0, 0, 0))
    return pl.pallas_call(
        functools.partial(_nsa_kernel, t=t, n_blk=n_blk),
        grid=(b, t // Q_BLOCK),
        in_specs=[pl.BlockSpec((None, NSA_WIDTH, Q_BLOCK), lambda i, j: (i, 0, j)),
                  pl.BlockSpec((None, NSA_WIDTH, Q_BLOCK), lambda i, j: (i, 0, j)),
                  pl.BlockSpec((None, GATE_ROWS, Q_BLOCK), lambda i, j: (i, 0, j)),
                  seq_spec, seq_spec, chunked, chunked,
                  pl.BlockSpec((None, KV_GROUPS, n_rows, LANES), lambda i, j: (i, 0, 0, 0)),
                  pl.BlockSpec((None, KV_GROUPS, HEAD_DIM, n_rows), lambda i, j: (i, 0, 0, 0))]
                 + [_const_spec(a.shape) for a in tables],
        out_specs=pl.BlockSpec((None, Q_BLOCK, NSA_WIDTH), lambda i, j: (i, j, 0)),
        out_shape=jax.ShapeDtypeStruct((b, t, NSA_WIDTH), BF16),
        scratch_shapes=[pltpu.VMEM((KV_GROUPS, t, 2 * LANES), BF16),
                        pltpu.VMEM((KV_GROUPS, t + WINDOW, LANES), BF16),
                        pltpu.VMEM((KV_GROUPS, 8, GQ), F32),
                        pltpu.VMEM((KV_GROUPS, HEAD_DIM + ONES_ROWS, GQ), F32),
                        pltpu.SMEM((t // SEL_CHUNK,), jnp.int32)],
        compiler_params=pltpu.CompilerParams(
            dimension_semantics=("arbitrary", "arbitrary"), vmem_limit_bytes=VMEM_LIMIT),
        name="nsa",
    )(qt, q2t, gt, ks, kw, vst, vwt, kc, vct, *tables)


def _mix_ffn_kernel(x_ref, o_ref, u_ref, halo_ref, wp_ref, sc_ref, wo_ref, g2_ref, wg_ref, wu_ref, wd_ref,
                    *rest, tm, final):
    if final:
        fn_ref, y_ref, mix_ref, h_ref, acc_ref = rest
    else:
        y_ref, mix_ref, h_ref, acc_ref = rest
    ti = pl.program_id(1)
    mix_ref[:, 0:NSA_WIDTH] = o_ref[...]
    halo = jnp.where(ti > 0, halo_ref[...], 0.0)
    pos = ti * tm + lax.broadcasted_iota(jnp.int32, (tm, 1), 0)
    for gi, w in enumerate(POOL_WIDTHS):
        lanes = slice(gi * POOL_GROUP, (gi + 1) * POOL_GROUP)
        u = u_ref[:, lanes]
        ext = jnp.concatenate([halo[:, lanes], u], axis=0)
        run = ext
        step = 1
        while step < w:
            run = run + pltpu.roll(run, step, axis=0)
            step *= 2
        cnt = jnp.minimum(pos + 1, w).astype(F32)
        y = run[POOL_HALO:, :] / cnt - u
        yo = _dot(y.astype(BF16), wp_ref[gi]) * sc_ref[:, lanes]
        mix_ref[:, NSA_WIDTH + gi * POOL_GROUP:NSA_WIDTH + (gi + 1) * POOL_GROUP] = yo.astype(BF16)
    x2 = x_ref[...] + _dot(mix_ref[...], wo_ref[...])
    y = _swiglu_residual(x2, g2_ref, wg_ref, wu_ref, wd_ref, h_ref, acc_ref)
    if final:
        y = _rms(y, fn_ref[...])
    y_ref[...] = y


def _mix_ffn(x3, o_nsa, u, wp, sc, wo, g2, wg, wu, wd, final_g=None, tm=512):
    b, t, d = x3.shape
    per = tm // POOL_HALO
    final = final_g is not None
    tail_specs = [_resident((1, d))] if final else []
    tail_args = [final_g] if final else []
    return pl.pallas_call(
        functools.partial(_mix_ffn_kernel, tm=tm, final=final),
        grid=(b, t // tm),
        in_specs=[pl.BlockSpec((None, tm, d), lambda i, j: (i, j, 0)),
                  pl.BlockSpec((None, tm, NSA_WIDTH), lambda i, j: (i, j, 0)),
                  pl.BlockSpec((None, tm, POOL_CH), lambda i, j: (i, j, 0)),
                  pl.BlockSpec((None, POOL_HALO, POOL_CH), lambda i, j: (i, jnp.maximum(j * per - 1, 0), 0)),
                  _resident(wp.shape), _resident(sc.shape), _resident(wo.shape),
                  _resident((1, d)), _resident(wg.shape), _resident(wu.shape), _resident(wd.shape)] + tail_specs,
        out_specs=pl.BlockSpec((None, tm, d), lambda i, j: (i, j, 0)),
        out_shape=jax.ShapeDtypeStruct((b, t, d), F32),
        scratch_shapes=[pltpu.VMEM((tm, NSA_WIDTH + POOL_CH), BF16),
                        pltpu.VMEM((tm, d), BF16), pltpu.VMEM((tm, d), F32)],
        compiler_params=pltpu.CompilerParams(
            dimension_semantics=("parallel", "parallel"), vmem_limit_bytes=VMEM_LIMIT),
        name="mix_ffn_final" if final else "mix_ffn",
    )(x3, o_nsa, u, u, wp, sc, wo, g2, wg, wu, wd, *tail_args)


def kernel(x, ffn1_norm, ffn1_wg, ffn1_wu, ffn1_wd, mix_norm, w_in, cmp_pe_k, cmp_wk1, cmp_wk2,
           cmp_pe_v, cmp_wv1, cmp_wv2, pool_w, pool_scale, w_out, ffn2_norm, ffn2_wg, ffn2_wu,
           ffn2_wd, final_norm):
    b, t, d = x.shape
    depth = ffn1_norm.shape[0]
    tables = _nsa_tables(t)
    bf = lambda a: a.astype(BF16)
    for l in range(depth):
        x, kcin, vcin, ks, kw, u, qt, q2t, vst, vwt, gt = _ffn_proj(
            x, ffn1_norm[l][None], bf(ffn1_wg[l]), bf(ffn1_wu[l]), bf(ffn1_wd[l]),
            mix_norm[l][None], *_split_w_in(w_in[l]))
        kc, vct = _compress(kcin, vcin,
                            *_compress_weights(cmp_pe_k[l], cmp_wk1[l], cmp_wk2[l], False),
                            *_compress_weights(cmp_pe_v[l], cmp_wv1[l], cmp_wv2[l], True))
        o_nsa = _nsa(qt, q2t, gt, ks, kw, vst, vwt, kc, vct, tables)
        x = _mix_ffn(x, o_nsa, u, bf(pool_w[l]), pool_scale[l][None], bf(w_out[l]),
                     ffn2_norm[l][None], bf(ffn2_wg[l]), bf(ffn2_wu[l]), bf(ffn2_wd[l]),
                     final_g=final_norm[None] if l == depth - 1 else None)
    return x
```

```python
import functools

import numpy as np
import jax
import jax.numpy as jnp
from jax import lax
from jax.experimental import pallas as pl
from jax.experimental.pallas import tpu as pltpu

HEAD_DIM = 64
NSA_HEADS = 8
KV_GROUPS = 2
HEADS_PER_GROUP = NSA_HEADS // KV_GROUPS
NSA_WIDTH = NSA_HEADS * HEAD_DIM
KV_WIDTH = KV_GROUPS * HEAD_DIM
CMP_BLOCK = 32
CMP_STRIDE = 16
CMP_HIDDEN = 128
SEL_BLOCK = 64
N_SEL = 8
WINDOW = 512
Q_BLOCK = 256
POOL_WIDTHS = (2, 4, 8, 16)
POOL_GROUP = 128
POOL_CH = POOL_GROUP * len(POOL_WIDTHS)
N_GATES = 3 * NSA_HEADS
GATE_ROWS = 32
EPS = 1e-6

LANES = 128
GQ = HEADS_PER_GROUP * Q_BLOCK
SEL_CHUNK = 256
SLAB_HEADS = 2
FIRST_KEYS = 128
WIN_KEYS = WINDOW + Q_BLOCK
ONES_ROWS = 16
POOL_HALO = 16
FF_CHUNK = 256
LOG2E = 1.4426950408889634
NEG = -1e30
VMEM_LIMIT = 56 * 1024 * 1024

F32 = jnp.float32
BF16 = jnp.bfloat16


def _slopes():
    n = NSA_HEADS
    return (2.0 ** (-8.0 * np.arange(1, n + 1) / n)).astype(np.float32)


def _dot(a, b):
    return jnp.dot(a, b, preferred_element_type=F32)


def _dot_nt(a, b):
    return lax.dot_general(a, b, (((1,), (1,)), ((), ())), preferred_element_type=F32)


def _rms(x, g):
    return x * lax.rsqrt(jnp.mean(x * x, axis=-1, keepdims=True) + EPS) * g


def _const_spec(shape):
    nd = len(shape)
    return pl.BlockSpec(shape, lambda *_: (0,) * nd)


def _swiglu_residual(x, g_ref, wg_ref, wu_ref, wd_ref, h_ref, acc_ref):
    h_ref[...] = _rms(x, g_ref[...]).astype(BF16)
    for c in range(wg_ref.shape[1] // FF_CHUNK):
        cols = slice(c * FF_CHUNK, (c + 1) * FF_CHUNK)
        h = h_ref[...]
        gate = _dot(h, wg_ref[:, cols])
        up = _dot(h, wu_ref[:, cols])
        a = (gate * jax.nn.sigmoid(gate) * up).astype(BF16)
        part = _dot(a, wd_ref[cols, :])
        if c == 0:
            acc_ref[...] = part
        else:
            acc_ref[...] += part
    return x + 0.5 * acc_ref[...]


def _resident(shape):
    nd = len(shape)
    return pl.BlockSpec(shape, lambda *_: (0,) * nd, pipeline_mode=pl.Buffered(1))


_TOK_KC = (0, 128)
_TOK_VC = (128, 256)
_TOK_KS = (256, 384)
_TOK_KW = (384, 512)
_TOK_U = (512, 1024)
_FEAT_Q = (0, 512)
_FEAT_VS = (512, 640)
_FEAT_VW = (640, 768)
_FEAT_GT = (768, 768 + GATE_ROWS)


def _ffn_proj_kernel(x_ref, g1_ref, wg_ref, wu_ref, wd_ref, g_ref, wt_ref, wf_ref,
                     x1_ref, kcin_ref, vcin_ref, ks_ref, kw_ref, u_ref,
                     qt_ref, q2t_ref, vst_ref, vwt_ref, gt_ref, h_ref, acc_ref, *, tm):
    x1 = _swiglu_residual(x_ref[...], g1_ref, wg_ref, wu_ref, wd_ref, h_ref, acc_ref)
    x1_ref[...] = x1
    h = _rms(x1, g_ref[...]).astype(BF16)

    def tok(lo_hi):
        return _dot(h, wt_ref[:, lo_hi[0]:lo_hi[1]])

    kcin_ref[...] = tok(_TOK_KC)
    vcin_ref[...] = tok(_TOK_VC)
    ks_ref[...] = tok(_TOK_KS).astype(BF16)
    kw_ref[...] = tok(_TOK_KW).astype(BF16)
    u_ref[...] = tok(_TOK_U)
    zt = _dot_nt(wf_ref[...], h)
    qt_ref[...] = (zt[_FEAT_Q[0]:_FEAT_Q[1]] * (HEAD_DIM ** -0.5)).astype(BF16)
    q2t_ref[...] = (zt[_FEAT_Q[0]:_FEAT_Q[1]] * (HEAD_DIM ** -0.5 * LOG2E)).astype(BF16)
    gt_ref[...] = zt[_FEAT_GT[0]:_FEAT_GT[1]]
    for c in range(tm // LANES):
        cols = slice(c * LANES, (c + 1) * LANES)
        vst_ref[c] = zt[_FEAT_VS[0]:_FEAT_VS[1], cols].astype(BF16)
        vwt_ref[c] = zt[_FEAT_VW[0]:_FEAT_VW[1], cols].astype(BF16)


def _ffn_proj(x3, g1, wg, wu, wd, g, w_tok, w_feat, tm=512):
    b, t, d = x3.shape
    nck = tm // LANES
    tok = lambda wd: pl.BlockSpec((None, tm, wd), lambda i, j: (i, j, 0))
    chunked = pl.BlockSpec((None, nck, KV_WIDTH, LANES), lambda i, j: (i, j, 0, 0))
    return pl.pallas_call(
        functools.partial(_ffn_proj_kernel, tm=tm),
        grid=(b, t // tm),
        in_specs=[tok(d), _resident((1, d)), _resident(wg.shape), _resident(wu.shape), _resident(wd.shape),
                  _resident((1, d)), _resident(w_tok.shape), _resident(w_feat.shape)],
        out_specs=[tok(d), tok(KV_WIDTH), tok(KV_WIDTH), tok(KV_WIDTH), tok(KV_WIDTH), tok(POOL_CH),
                   pl.BlockSpec((None, NSA_WIDTH, tm), lambda i, j: (i, 0, j)),
                   pl.BlockSpec((None, NSA_WIDTH, tm), lambda i, j: (i, 0, j)),
                   chunked, chunked,
                   pl.BlockSpec((None, GATE_ROWS, tm), lambda i, j: (i, 0, j))],
        out_shape=[jax.ShapeDtypeStruct((b, t, d), F32),
                   jax.ShapeDtypeStruct((b, t, KV_WIDTH), F32),
                   jax.ShapeDtypeStruct((b, t, KV_WIDTH), F32),
                   jax.ShapeDtypeStruct((b, t, KV_WIDTH), BF16),
                   jax.ShapeDtypeStruct((b, t, KV_WIDTH), BF16),
                   jax.ShapeDtypeStruct((b, t, POOL_CH), F32),
                   jax.ShapeDtypeStruct((b, NSA_WIDTH, t), BF16),
                   jax.ShapeDtypeStruct((b, NSA_WIDTH, t), BF16),
                   jax.ShapeDtypeStruct((b, t // LANES, KV_WIDTH, LANES), BF16),
                   jax.ShapeDtypeStruct((b, t // LANES, KV_WIDTH, LANES), BF16),
                   jax.ShapeDtypeStruct((b, GATE_ROWS, t), F32)],
        scratch_shapes=[pltpu.VMEM((tm, d), BF16), pltpu.VMEM((tm, d), F32)],
        compiler_params=pltpu.CompilerParams(
            dimension_semantics=("parallel", "parallel"), vmem_limit_bytes=VMEM_LIMIT),
        name="ffn_proj",
    )(x3, g1, wg, wu, wd, g, w_tok, w_feat)


def _split_w_in(w):
    q0, kv0 = 0, NSA_WIDTH
    col = lambda i: slice(kv0 + i * KV_WIDTH, kv0 + (i + 1) * KV_WIDTH)
    g0 = kv0 + 6 * KV_WIDTH
    w_tok = jnp.concatenate([w[:, col(0)], w[:, col(1)], w[:, col(2)], w[:, col(4)],
                             w[:, g0 + N_GATES:]], axis=1).astype(BF16)
    pad = jnp.zeros((w.shape[0], GATE_ROWS - N_GATES), w.dtype)
    w_feat = jnp.concatenate([w[:, q0:NSA_WIDTH], w[:, col(3)], w[:, col(5)],
                              w[:, g0:g0 + N_GATES], pad], axis=1).T.astype(BF16)
    return w_tok, w_feat


def _gelu_tanh(x):
    c = np.float32(np.sqrt(2.0 / np.pi))
    return 0.5 * x * (1.0 + jnp.tanh(c * (x + 0.044715 * (x * x * x))))


def _compress_kernel(kcin_ref, vcin_ref, pek_ref, wk1_ref, wk2_ref, pev_ref, wv1_ref, wv2_ref,
                     kc_ref, vct_ref, *, n_rows):
    nl = CMP_STRIDE
    hid2 = KV_GROUPS * CMP_HIDDEN

    def hidden(src_ref, pe_ref, w1_ref):
        acc = jnp.zeros((n_rows + 16, 2 * hid2), F32)
        for l in range(nl):
            xl = src_ref[pl.ds(l, n_rows, stride=nl), :]
            pa = jnp.broadcast_to(pe_ref[l:l + 1, :], (8, KV_WIDTH))
            pb = jnp.broadcast_to(pe_ref[nl + l:nl + l + 1, :], (8, KV_WIDTH))
            lhs = jnp.concatenate([xl, pa, pb], axis=0).astype(BF16)
            acc = acc + _dot(lhs, w1_ref[l])
        sa = acc[0:n_rows, 0:hid2]
        sb = acc[0:n_rows, hid2:2 * hid2]
        bias = acc[n_rows:n_rows + 1, 0:hid2] + acc[n_rows + 8:n_rows + 9, hid2:2 * hid2]
        hcur = sa + pltpu.roll(sb, n_rows - 1, axis=0) + bias
        return _gelu_tanh(hcur).astype(BF16)

    hk = hidden(kcin_ref, pek_ref, wk1_ref)
    kc = _dot(hk, wk2_ref[...])
    for g in range(KV_GROUPS):
        kc_ref[g] = kc[:, g * LANES:(g + 1) * LANES].astype(BF16)
    hv = hidden(vcin_ref, pev_ref, wv1_ref)
    for g in range(KV_GROUPS):
        vct_ref[g] = _dot_nt(wv2_ref[g], hv).astype(BF16)


def _compress(kcin, vcin, pek2, wk1b, wk2b, pev2, wv1b, wv2t):
    b, t, _ = kcin.shape
    n_rows = t // CMP_STRIDE
    seq_spec = pl.BlockSpec((None, t, KV_WIDTH), lambda i: (i, 0, 0))
    return pl.pallas_call(
        functools.partial(_compress_kernel, n_rows=n_rows),
        grid=(b,),
        in_specs=[seq_spec, seq_spec,
                  _const_spec(pek2.shape), _const_spec(wk1b.shape), _const_spec(wk2b.shape),
                  _const_spec(pev2.shape), _const_spec(wv1b.shape), _const_spec(wv2t.shape)],
        out_specs=[pl.BlockSpec((None, KV_GROUPS, n_rows, LANES), lambda i: (i, 0, 0, 0)),
                   pl.BlockSpec((None, KV_GROUPS, HEAD_DIM, n_rows), lambda i: (i, 0, 0, 0))],
        out_shape=[jax.ShapeDtypeStruct((b, KV_GROUPS, n_rows, LANES), BF16),
                   jax.ShapeDtypeStruct((b, KV_GROUPS, HEAD_DIM, n_rows), BF16)],
        compiler_params=pltpu.CompilerParams(
            dimension_semantics=("parallel",), vmem_limit_bytes=VMEM_LIMIT),
        name="compress",
    )(kcin, vcin, pek2, wk1b, wk2b, pev2, wv1b, wv2t)


def _compress_weights(pe, w1, w2, value_layout):
    nl, dh, hid = CMP_STRIDE, HEAD_DIM, CMP_HIDDEN
    w1r = w1.reshape(2, nl, dh, hid)
    z = jnp.zeros((nl, dh, hid), w1.dtype)
    row_g0 = jnp.concatenate([w1r[0], z, w1r[1], z], axis=-1)
    row_g1 = jnp.concatenate([z, w1r[0], z, w1r[1]], axis=-1)
    w1b = jnp.concatenate([row_g0, row_g1], axis=1).astype(BF16)
    pe2 = jnp.concatenate([pe, pe], axis=-1)
    zz = jnp.zeros((hid, dh), w2.dtype)
    if value_layout:
        w2b = jnp.stack([jnp.concatenate([w2, zz], axis=0).T,
                         jnp.concatenate([zz, w2], axis=0).T]).astype(BF16)
    else:
        top = jnp.concatenate([w2, zz, zz, zz], axis=-1)
        bot = jnp.concatenate([zz, zz, w2, zz], axis=-1)
        w2b = jnp.concatenate([top, bot], axis=0).astype(BF16)
    return pe2, w1b, w2b


def _nsa_tables(t):
    slopes = _slopes().reshape(KV_GROUPS, HEADS_PER_GROUP)
    n_rows = t // CMP_STRIDE
    n_blk = t // SEL_BLOCK
    i = np.arange(Q_BLOCK)
    c = np.arange(n_rows)
    d0c = (i[None, :] - (c[:, None] * CMP_STRIDE + CMP_BLOCK - 1)).astype(np.int32)
    tblc = -LOG2E * slopes[:, None, :, None].astype(np.float64) * d0c[None, :, None, :]
    tblc = tblc.astype(np.float32)
    tblc = tblc.reshape(KV_GROUPS, n_rows, GQ)
    d0c = np.tile(d0c, (1, HEADS_PER_GROUP))
    j = np.arange(WIN_KEYS)
    dist = i[None, :] + WINDOW - j[:, None]
    ok = (dist >= 0) & (dist < WINDOW)
    tblw = np.where(ok[None, :, None, :],
                    -LOG2E * slopes[:, None, :, None].astype(np.float64) * dist[None, :, None, :],
                    NEG).astype(np.float32).reshape(KV_GROUPS, WIN_KEYS, GQ)
    jj = np.arange(SEL_CHUNK)
    per = SEL_CHUNK // Q_BLOCK
    tailm = np.stack([np.where(jj[:, None] > i[None, :] + Q_BLOCK * r, np.float32(NEG), np.float32(0))
                      for r in range(per)]).astype(np.float32)
    ci = c[None, :] * CMP_STRIDE
    sj = np.arange(n_blk)[:, None] * SEL_BLOCK
    n_cmp = (t - CMP_BLOCK) // CMP_STRIDE + 1
    ov = ((ci <= sj + SEL_BLOCK - 1) & (ci + CMP_BLOCK - 1 >= sj) & (c[None, :] < n_cmp))
    ov = ov.astype(np.float32)
    pos = np.arange(t)
    ktag = np.zeros((t, 2 * LANES), np.float32)
    ktag[pos, HEAD_DIM + pos // SEL_BLOCK] = 1.0
    ktag[:, LANES] = pos % SEL_BLOCK
    qflag = np.zeros((HEAD_DIM, GQ), np.float32)
    qflag[0, :] = NEG
    qslope = np.zeros((KV_GROUPS, LANES, GQ), np.float32)
    qslope[:, 0, :] = np.repeat(slopes, Q_BLOCK, axis=1)
    return (jnp.asarray(tblc), jnp.asarray(d0c), jnp.asarray(tblw), jnp.asarray(tailm),
            jnp.asarray(ov, dtype=BF16), jnp.asarray(ktag, dtype=BF16),
            jnp.asarray(qflag, dtype=BF16), jnp.asarray(qslope, dtype=BF16))


def _nsa_kernel(qt_ref, q2t_ref, gt_ref, ks_ref, kw_ref, vst_ref, vwt_ref, kc_ref, vct_ref,
                tblc_ref, d0c_ref, tblw_ref, tailm_ref, ov_ref, ktag_ref, qflag_ref, qslope_ref,
                o_ref,
                kaug_ref, kwaug_ref, m_ref, acc_ref, need_ref, *, t, n_blk):
    qi = pl.program_id(1)
    slopes = _slopes()
    hg = HEADS_PER_GROUP
    prep_rows = 512

    @pl.when(qi == 0)
    def _prep():
        low = lax.broadcasted_iota(jnp.int32, (prep_rows, LANES), 1) < HEAD_DIM
        flag = jnp.where(lax.broadcasted_iota(jnp.int32, (WINDOW, LANES), 1) == HEAD_DIM, 1.0, 0.0)
        for g in range(KV_GROUPS):
            kwaug_ref[g, 0:WINDOW, :] = flag.astype(BF16)

        def body(c, carry):
            r0 = pl.multiple_of(c * prep_rows, prep_rows)
            rows = pl.ds(r0, prep_rows)
            rows_w = pl.ds(r0 + WINDOW, prep_rows)
            tag = ktag_ref[rows, 0:LANES].astype(F32)
            ks = ks_ref[rows, :].astype(F32)
            kw = kw_ref[rows, :].astype(F32)
            ks_sw = pltpu.roll(ks, HEAD_DIM, axis=1)
            kw_sw = pltpu.roll(kw, HEAD_DIM, axis=1)
            for g in range(KV_GROUPS):
                kaug_ref[g, rows, 0:LANES] = jnp.where(low, ks if g == 0 else ks_sw, tag).astype(BF16)
                kaug_ref[g, rows, LANES:2 * LANES] = ktag_ref[rows, LANES:2 * LANES]
                kwaug_ref[g, rows_w, :] = jnp.where(low, kw if g == 0 else kw_sw, 0.0).astype(BF16)
            return carry

        lax.fori_loop(0, t // prep_rows, body, 0)

    gates = jax.nn.sigmoid(gt_ref[...])
    bi = lax.broadcasted_iota(jnp.int32, (n_blk, Q_BLOCK), 0)
    lq = lax.broadcasted_iota(jnp.int32, (n_blk, Q_BLOCK), 1)
    cur = (Q_BLOCK // SEL_BLOCK) * qi + lax.shift_right_logical(lq, SEL_BLOCK.bit_length() - 1)
    valid = bi <= cur
    near = (bi == 0) | (bi >= cur - 1)
    always = valid & (near | (cur < N_SEL))
    free = valid & jnp.logical_not(near)
    later = bi >= FIRST_KEYS // SEL_BLOCK
    rel_blk = (bi - cur).astype(F32)

    head_slabs = [slice(h * Q_BLOCK, (h + SLAB_HEADS) * Q_BLOCK) for h in range(0, hg, SLAB_HEADS)]

    def with_ones(vt):
        first = lax.broadcasted_iota(jnp.int32, (ONES_ROWS, vt.shape[1]), 0) == 0
        return jnp.concatenate([vt, jnp.where(first, 1.0, 0.0).astype(BF16)], axis=0)

    def queries(g):
        heads = lambda ref: jnp.concatenate([ref[(g * hg + h) * HEAD_DIM:(g * hg + h + 1) * HEAD_DIM, :]
                                             for h in range(hg)], axis=1)
        return heads(qt_ref), jnp.concatenate([heads(q2t_ref), qflag_ref[...]], axis=0)

    def select(g, qt, s):
        o_parts, psum = [], None
        for lanes in head_slabs:
            sh = s[:, lanes] + tblc_ref[g, :, lanes]
            sh = jnp.where(d0c_ref[:, lanes] + Q_BLOCK * qi >= 0, sh, NEG)
            m = jnp.max(sh, axis=0, keepdims=True)
            m = jnp.where(m < 0.5 * NEG, 0.0, m)
            p = jnp.exp2(sh - m)
            l = jnp.sum(p, axis=0, keepdims=True)
            pn = p * (1.0 / jnp.maximum(l, 1e-30))
            o_parts.append(_dot(vct_ref[g], pn.astype(BF16)))
            for h in range(SLAB_HEADS):
                ph = pn[:, h * Q_BLOCK:(h + 1) * Q_BLOCK]
                psum = ph if psum is None else psum + ph
        o_cmp = jnp.concatenate(o_parts, axis=1)

        p_hi = psum.astype(BF16)
        p_lo = (psum - p_hi.astype(F32)).astype(BF16)
        imp = _dot(ov_ref[...], p_hi) + _dot(ov_ref[...], p_lo)
        val = jnp.where(free, imp, -1.0)
        sel = jnp.zeros((n_blk, Q_BLOCK), F32)
        for _ in range(N_SEL - 3):
            top = jnp.max(val, axis=0, keepdims=True)
            idx = jnp.min(jnp.where(val == top, bi, n_blk), axis=0, keepdims=True)
            pick = bi == idx
            sel = jnp.where(pick, 1.0, sel)
            val = jnp.where(pick, -2.0, val)
        chosen = jnp.where(always, 1.0, jnp.where(free, sel, 0.0)) > 0.0

        def tagged(blocks):
            bterm = jnp.concatenate(
                [jnp.where(blocks, float(slopes[g * hg + h]) * SEL_BLOCK * rel_blk, NEG) for h in range(hg)],
                axis=1).astype(BF16)
            parts = [qt, bterm]
            if n_blk < HEAD_DIM:
                parts.append(jnp.zeros((HEAD_DIM - n_blk, GQ), BF16))
            parts.append(qslope_ref[g])
            return jnp.concatenate(parts, axis=0)

        return o_cmp, tagged(chosen), tagged(chosen & later), jnp.where(chosen & later, 1.0, 0.0)

    def sel_scores(g, q, chunk, mask):
        keys = kaug_ref[g, pl.ds(pl.multiple_of(chunk * SEL_CHUNK, SEL_CHUNK), SEL_CHUNK), :]
        if mask is None:
            return _dot(keys, q)
        keys = jnp.concatenate([keys, kaug_ref[g, 0:FIRST_KEYS, :]], axis=0)
        return _dot(keys, q) + mask

    def sel_update(g, s, chunk, carry, with_first):
        m, acc = carry
        per_chunk = SEL_CHUNK // LANES
        slabs = [chunk * per_chunk + k for k in range(per_chunk)]
        if with_first:
            slabs += list(range(FIRST_KEYS // LANES))
        vt = with_ones(jnp.concatenate([vst_ref[k, g * HEAD_DIM:(g + 1) * HEAD_DIM, :] for k in slabs], axis=1))
        ms, accs = [], []
        for lanes in head_slabs:
            sh, mh = s[:, lanes], m[:, lanes]
            m_new = jnp.maximum(mh, jnp.max(sh, axis=0, keepdims=True))
            alpha = jnp.exp(mh - m_new)
            p = jnp.exp(sh - m_new).astype(BF16)
            ms.append(m_new)
            accs.append(acc[:, lanes] * alpha + _dot(vt, p))
        return jnp.concatenate(ms, axis=1), jnp.concatenate(accs, axis=1)

    def sel_steps(chunk, mask, carries):
        q = qaug if mask is not None else qfar
        scores = [sel_scores(g, q[g], chunk, mask) for g in groups]
        return tuple(sel_update(g, scores[g], chunk, carries[g], mask is not None) for g in groups)

    def window_scores(g, qw):
        rows = pl.ds(pl.multiple_of(qi * Q_BLOCK, Q_BLOCK), WIN_KEYS)
        return _dot(kwaug_ref[g, rows, :], qw)

    def window(g, s):
        back = WINDOW // LANES
        ahead = Q_BLOCK // LANES
        vt = jnp.concatenate([vwt_ref[jnp.maximum(qi * ahead - back + k, 0), g * HEAD_DIM:(g + 1) * HEAD_DIM, :]
                              for k in range(back + ahead)], axis=1)
        vt = with_ones(vt)
        outs = []
        for lanes in head_slabs:
            sh = s[:, lanes] + tblw_ref[g, :, lanes]
            m = jnp.max(sh, axis=0, keepdims=True)
            p = jnp.exp2(sh - m).astype(BF16)
            acc = _dot(vt, p)
            outs.append(acc[0:HEAD_DIM] * (1.0 / acc[HEAD_DIM:HEAD_DIM + 1]))
        return jnp.concatenate(outs, axis=1)

    groups = tuple(range(KV_GROUPS))
    qt, qw = zip(*[queries(g) for g in groups])
    s_cmp = [_dot(kc_ref[g], qw[g]) for g in groups]
    s_win = [window_scores(g, qw[g]) for g in groups]
    sel_out, o_win = [], []
    for g in groups:
        sel_out.append(select(g, qt[g], s_cmp[g]))
        o_win.append(window(g, s_win[g]))
    o_cmp, qaug, qfar, picked = zip(*sel_out)

    per = SEL_CHUNK // Q_BLOCK
    blk_per_chunk = SEL_CHUNK // SEL_BLOCK
    n_full = qi // per
    wanted = picked[0]
    for g in groups[1:]:
        wanted = wanted + picked[g]
    for c in range(n_blk // blk_per_chunk - 1):
        hit = jnp.sum(wanted[c * blk_per_chunk:(c + 1) * blk_per_chunk, :])
        need_ref[c] = (hit > 0.0).astype(jnp.int32)
    def load_carries():
        return tuple((m_ref[g, 0:1, :], acc_ref[g]) for g in groups)

    def store_carries(carries):
        for g, (m, acc) in enumerate(carries):
            m_ref[g, 0:1, :] = m
            acc_ref[g] = acc

    first_mask = jnp.where(n_full == 0, NEG, 0.0) + jnp.zeros((FIRST_KEYS, GQ), F32)
    tail_mask = jnp.concatenate([jnp.concatenate([tailm_ref[qi % per]] * hg, axis=1), first_mask], axis=0)
    init = (jnp.full((1, GQ), NEG, F32), jnp.zeros((HEAD_DIM + ONES_ROWS, GQ), F32))
    store_carries(sel_steps(n_full, tail_mask, tuple(init for _ in groups)))

    def full_chunk(c, carry):
        @pl.when(need_ref[c] > 0)
        def _():
            store_carries(sel_steps(c, None, load_carries()))
        return carry

    lax.fori_loop(0, n_full, full_chunk, 0)
    o_slc = [acc[0:HEAD_DIM] * (1.0 / acc[HEAD_DIM:HEAD_DIM + 1]) for _, acc in load_carries()]

    for g in groups:
        for pr in range(hg // 2):
            outs = []
            for h in (2 * pr, 2 * pr + 1):
                hh = g * hg + h
                lanes = slice(h * Q_BLOCK, (h + 1) * Q_BLOCK)
                out = gates[3 * hh:3 * hh + 1, :] * o_cmp[g][:, lanes]
                out = out + gates[3 * hh + 1:3 * hh + 2, :] * o_slc[g][:, lanes]
                out = out + gates[3 * hh + 2:3 * hh + 3, :] * o_win[g][:, lanes]
                outs.append(out)
            pair = jnp.concatenate(outs, axis=0)
            o_ref[:, (g * 2 + pr) * LANES:(g * 2 + pr + 1) * LANES] = pair.T.astype(BF16)


def _nsa(qt, q2t, gt, ks, kw, vst, vwt, kc, vct, tables):
    b, _, t = qt.shape
    n_blk = t // SEL_BLOCK
    n_rows = t // CMP_STRIDE
    assert t % SEL_CHUNK == 0 and SEL_CHUNK % Q_BLOCK == 0
    assert N_SEL <= n_blk <= HEAD_DIM and n_blk % 16 == 0
    seq_spec = pl.BlockSpec((None, t, LANES), lambda i, j: (i, 0, 0))
    chunked = pl.BlockSpec((None, t // LANES, KV_WIDTH, LANES), lambda i, j: (i, 0, 0, 0))
    return pl.pallas_call(
        functools.partial(_nsa_kernel, t=t, n_blk=n_blk),
        grid=(b, t // Q_BLOCK),
        in_specs=[pl.BlockSpec((None, NSA_WIDTH, Q_BLOCK), lambda i, j: (i, 0, j)),
                  pl.BlockSpec((None, NSA_WIDTH, Q_BLOCK), lambda i, j: (i, 0, j)),
                  pl.BlockSpec((None, GATE_ROWS, Q_BLOCK), lambda i, j: (i, 0, j)),
                  seq_spec, seq_spec, chunked, chunked,
                  pl.BlockSpec((None, KV_GROUPS, n_rows, LANES), lambda i, j: (i, 0, 0, 0)),
                  pl.BlockSpec((None, KV_GROUPS, HEAD_DIM, n_rows), lambda i, j: (i, 0, 0, 0))]
                 + [_const_spec(a.shape) for a in tables],
        out_specs=pl.BlockSpec((None, Q_BLOCK, NSA_WIDTH), lambda i, j: (i, j, 0)),
        out_shape=jax.ShapeDtypeStruct((b, t, NSA_WIDTH), BF16),
        scratch_shapes=[pltpu.VMEM((KV_GROUPS, t, 2 * LANES), BF16),
                        pltpu.VMEM((KV_GROUPS, t + WINDOW, LANES), BF16),
                        pltpu.VMEM((KV_GROUPS, 8, GQ), F32),
                        pltpu.VMEM((KV_GROUPS, HEAD_DIM + ONES_ROWS, GQ), F32),
                        pltpu.SMEM((t // SEL_CHUNK,), jnp.int32)],
        compiler_params=pltpu.CompilerParams(
            dimension_semantics=("arbitrary", "arbitrary"), vmem_limit_bytes=VMEM_LIMIT),
        name="nsa",
    )(qt, q2t, gt, ks, kw, vst, vwt, kc, vct, *tables)


def _mix_ffn_kernel(x_ref, o_ref, u_ref, halo_ref, wp_ref, sc_ref, wo_ref, g2_ref, wg_ref, wu_ref, wd_ref,
                    *rest, tm, final):
    if final:
        fn_ref, y_ref, mix_ref, h_ref, acc_ref = rest
    else:
        y_ref, mix_ref, h_ref, acc_ref = rest
    ti = pl.program_id(1)
    mix_ref[:, 0:NSA_WIDTH] = o_ref[...]
    halo = jnp.where(ti > 0, halo_ref[...], 0.0)
    pos = ti * tm + lax.broadcasted_iota(jnp.int32, (tm, 1), 0)
    for gi, w in enumerate(POOL_WIDTHS):
        lanes = slice(gi * POOL_GROUP, (gi + 1) * POOL_GROUP)
        u = u_ref[:, lanes]
        ext = jnp.concatenate([halo[:, lanes], u], axis=0)
        run = ext
        step = 1
        while step < w:
            run = run + pltpu.roll(run, step, axis=0)
            step *= 2
        cnt = jnp.minimum(pos + 1, w).astype(F32)
        y = run[POOL_HALO:, :] / cnt - u
        yo = _dot(y.astype(BF16), wp_ref[gi]) * sc_ref[:, lanes]
        mix_ref[:, NSA_WIDTH + gi * POOL_GROUP:NSA_WIDTH + (gi + 1) * POOL_GROUP] = yo.astype(BF16)
    x2 = x_ref[...] + _dot(mix_ref[...], wo_ref[...])
    y = _swiglu_residual(x2, g2_ref, wg_ref, wu_ref, wd_ref, h_ref, acc_ref)
    if final:
        y = _rms(y, fn_ref[...])
    y_ref[...] = y


def _mix_ffn(x3, o_nsa, u, wp, sc, wo, g2, wg, wu, wd, final_g=None, tm=512):
    b, t, d = x3.shape
    per = tm // POOL_HALO
    final = final_g is not None
    tail_specs = [_resident((1, d))] if final else []
    tail_args = [final_g] if final else []
    return pl.pallas_call(
        functools.partial(_mix_ffn_kernel, tm=tm, final=final),
        grid=(b, t // tm),
        in_specs=[pl.BlockSpec((None, tm, d), lambda i, j: (i, j, 0)),
                  pl.BlockSpec((None, tm, NSA_WIDTH), lambda i, j: (i, j, 0)),
                  pl.BlockSpec((None, tm, POOL_CH), lambda i, j: (i, j, 0)),
                  pl.BlockSpec((None, POOL_HALO, POOL_CH), lambda i, j: (i, jnp.maximum(j * per - 1, 0), 0)),
                  _resident(wp.shape), _resident(sc.shape), _resident(wo.shape),
                  _resident((1, d)), _resident(wg.shape), _resident(wu.shape), _resident(wd.shape)] + tail_specs,
        out_specs=pl.BlockSpec((None, tm, d), lambda i, j: (i, j, 0)),
        out_shape=jax.ShapeDtypeStruct((b, t, d), F32),
        scratch_shapes=[pltpu.VMEM((tm, NSA_WIDTH + POOL_CH), BF16),
                        pltpu.VMEM((tm, d), BF16), pltpu.VMEM((tm, d), F32)],
        compiler_params=pltpu.CompilerParams(
            dimension_semantics=("parallel", "parallel"), vmem_limit_bytes=VMEM_LIMIT),
        name="mix_ffn_final" if final else "mix_ffn",
    )(x3, o_nsa, u, u, wp, sc, wo, g2, wg, wu, wd, *tail_args)


def kernel(x, ffn1_norm, ffn1_wg, ffn1_wu, ffn1_wd, mix_norm, w_in, cmp_pe_k, cmp_wk1, cmp_wk2,
           cmp_pe_v, cmp_wv1, cmp_wv2, pool_w, pool_scale, w_out, ffn2_norm, ffn2_wg, ffn2_wu,
           ffn2_wd, final_norm):
    b, t, d = x.shape
    depth = ffn1_norm.shape[0]
    tables = _nsa_tables(t)
    bf = lambda a: a.astype(BF16)
    for l in range(depth):
        x, kcin, vcin, ks, kw, u, qt, q2t, vst, vwt, gt = _ffn_proj(
            x, ffn1_norm[l][None], bf(ffn1_wg[l]), bf(ffn1_wu[l]), bf(ffn1_wd[l]),
            mix_norm[l][None], *_split_w_in(w_in[l]))
        kc, vct = _compress(kcin, vcin,
                            *_compress_weights(cmp_pe_k[l], cmp_wk1[l], cmp_wk2[l], False),
                            *_compress_weights(cmp_pe_v[l], cmp_wv1[l], cmp_wv2[l], True))
        o_nsa = _nsa(qt, q2t, gt, ks, kw, vst, vwt, kc, vct, tables)
        x = _mix_ffn(x, o_nsa, u, bf(pool_w[l]), pool_scale[l][None], bf(w_out[l]),
                     ffn2_norm[l][None], bf(ffn2_wg[l]), bf(ffn2_wu[l]), bf(ffn2_wd[l]),
                     final_g=final_norm[None] if l == depth - 1 else None)
    return x
```

```python
import functools

import numpy as np
import jax
import jax.numpy as jnp
from jax import lax
from jax.experimental import pallas as pl
from jax.experimental.pallas import tpu as pltpu

HEAD_DIM = 64
NSA_HEADS = 8
KV_GROUPS = 2
HEADS_PER_GROUP = NSA_HEADS // KV_GROUPS
NSA_WIDTH = NSA_HEADS * HEAD_DIM
KV_WIDTH = KV_GROUPS * HEAD_DIM
CMP_BLOCK = 32
CMP_STRIDE = 16
CMP_HIDDEN = 128
SEL_BLOCK = 64
N_SEL = 8
WINDOW = 512
Q_BLOCK = 256
POOL_WIDTHS = (2, 4, 8, 16)
POOL_GROUP = 128
POOL_CH = POOL_GROUP * len(POOL_WIDTHS)
N_GATES = 3 * NSA_HEADS
GATE_ROWS = 32
EPS = 1e-6

LANES = 128
GQ = HEADS_PER_GROUP * Q_BLOCK
SEL_CHUNK = 256
SLAB_HEADS = 2
FIRST_KEYS = 128
WIN_KEYS = WINDOW + Q_BLOCK
ONES_ROWS = 16
POOL_HALO = 16
FF_CHUNK = 256
LOG2E = 1.4426950408889634
NEG = -1e30
VMEM_LIMIT = 56 * 1024 * 1024

F32 = jnp.float32
BF16 = jnp.bfloat16


def _slopes():
    n = NSA_HEADS
    return (2.0 ** (-8.0 * np.arange(1, n + 1) / n)).astype(np.float32)


def _dot(a, b):
    return jnp.dot(a, b, preferred_element_type=F32)


def _dot_nt(a, b):
    return lax.dot_general(a, b, (((1,), (1,)), ((), ())), preferred_element_type=F32)


def _rms(x, g):
    return x * lax.rsqrt(jnp.mean(x * x, axis=-1, keepdims=True) + EPS) * g


def _const_spec(shape):
    nd = len(shape)
    return pl.BlockSpec(shape, lambda *_: (0,) * nd)


def _swiglu_residual(x, g_ref, wg_ref, wu_ref, wd_ref, h_ref, acc_ref):
    h_ref[...] = _rms(x, g_ref[...]).astype(BF16)
    for c in range(wg_ref.shape[1] // FF_CHUNK):
        cols = slice(c * FF_CHUNK, (c + 1) * FF_CHUNK)
        h = h_ref[...]
        gate = _dot(h, wg_ref[:, cols])
        up = _dot(h, wu_ref[:, cols])
        a = (gate * jax.nn.sigmoid(gate) * up).astype(BF16)
        part = _dot(a, wd_ref[cols, :])
        if c == 0:
            acc_ref[...] = part
        else:
            acc_ref[...] += part
    return x + 0.5 * acc_ref[...]


def _resident(shape):
    nd = len(shape)
    return pl.BlockSpec(shape, lambda *_: (0,) * nd, pipeline_mode=pl.Buffered(1))


_TOK_KC = (0, 128)
_TOK_VC = (128, 256)
_TOK_KS = (256, 384)
_TOK_KW = (384, 512)
_TOK_U = (512, 1024)
_FEAT_Q = (0, 512)
_FEAT_VS = (512, 640)
_FEAT_VW = (640, 768)
_FEAT_GT = (768, 768 + GATE_ROWS)


def _ffn_proj_kernel(x_ref, g1_ref, wg_ref, wu_ref, wd_ref, g_ref, wt_ref, wf_ref,
                     x1_ref, kcin_ref, vcin_ref, ks_ref, kw_ref, u_ref,
                     qt_ref, vst_ref, vwt_ref, gt_ref, h_ref, acc_ref, *, tm):
    x1 = _swiglu_residual(x_ref[...], g1_ref, wg_ref, wu_ref, wd_ref, h_ref, acc_ref)
    x1_ref[...] = x1
    h = _rms(x1, g_ref[...]).astype(BF16)

    def tok(lo_hi):
        return _dot(h, wt_ref[:, lo_hi[0]:lo_hi[1]])

    kcin_ref[...] = tok(_TOK_KC)
    vcin_ref[...] = tok(_TOK_VC)
    ks_ref[...] = tok(_TOK_KS).astype(BF16)
    kw_ref[...] = tok(_TOK_KW).astype(BF16)
    u_ref[...] = tok(_TOK_U)
    zt = _dot_nt(wf_ref[...], h)
    qt_ref[...] = (zt[_FEAT_Q[0]:_FEAT_Q[1]] * (HEAD_DIM ** -0.5 * LOG2E)).astype(BF16)
    gt_ref[...] = zt[_FEAT_GT[0]:_FEAT_GT[1]]
    for c in range(tm // LANES):
        cols = slice(c * LANES, (c + 1) * LANES)
        vst_ref[c] = zt[_FEAT_VS[0]:_FEAT_VS[1], cols].astype(BF16)
        vwt_ref[c] = zt[_FEAT_VW[0]:_FEAT_VW[1], cols].astype(BF16)


def _ffn_proj(x3, g1, wg, wu, wd, g, w_tok, w_feat, tm=512):
    b, t, d = x3.shape
    nck = tm // LANES
    tok = lambda wd: pl.BlockSpec((None, tm, wd), lambda i, j: (i, j, 0))
    chunked = pl.BlockSpec((None, nck, KV_WIDTH, LANES), lambda i, j: (i, j, 0, 0))
    return pl.pallas_call(
        functools.partial(_ffn_proj_kernel, tm=tm),
        grid=(b, t // tm),
        in_specs=[tok(d), _resident((1, d)), _resident(wg.shape), _resident(wu.shape), _resident(wd.shape),
                  _resident((1, d)), _resident(w_tok.shape), _resident(w_feat.shape)],
        out_specs=[tok(d), tok(KV_WIDTH), tok(KV_WIDTH), tok(KV_WIDTH), tok(KV_WIDTH), tok(POOL_CH),
                   pl.BlockSpec((None, NSA_WIDTH, tm), lambda i, j: (i, 0, j)),
                   chunked, chunked,
                   pl.BlockSpec((None, GATE_ROWS, tm), lambda i, j: (i, 0, j))],
        out_shape=[jax.ShapeDtypeStruct((b, t, d), F32),
                   jax.ShapeDtypeStruct((b, t, KV_WIDTH), F32),
                   jax.ShapeDtypeStruct((b, t, KV_WIDTH), F32),
                   jax.ShapeDtypeStruct((b, t, KV_WIDTH), BF16),
                   jax.ShapeDtypeStruct((b, t, KV_WIDTH), BF16),
                   jax.ShapeDtypeStruct((b, t, POOL_CH), F32),
                   jax.ShapeDtypeStruct((b, NSA_WIDTH, t), BF16),
                   jax.ShapeDtypeStruct((b, t // LANES, KV_WIDTH, LANES), BF16),
                   jax.ShapeDtypeStruct((b, t // LANES, KV_WIDTH, LANES), BF16),
                   jax.ShapeDtypeStruct((b, GATE_ROWS, t), F32)],
        scratch_shapes=[pltpu.VMEM((tm, d), BF16), pltpu.VMEM((tm, d), F32)],
        compiler_params=pltpu.CompilerParams(
            dimension_semantics=("parallel", "parallel"), vmem_limit_bytes=VMEM_LIMIT),
        name="ffn_proj",
    )(x3, g1, wg, wu, wd, g, w_tok, w_feat)


def _split_w_in(w):
    q0, kv0 = 0, NSA_WIDTH
    col = lambda i: slice(kv0 + i * KV_WIDTH, kv0 + (i + 1) * KV_WIDTH)
    g0 = kv0 + 6 * KV_WIDTH
    w_tok = jnp.concatenate([w[:, col(0)], w[:, col(1)], w[:, col(2)], w[:, col(4)],
                             w[:, g0 + N_GATES:]], axis=1).astype(BF16)
    pad = jnp.zeros((w.shape[0], GATE_ROWS - N_GATES), w.dtype)
    w_feat = jnp.concatenate([w[:, q0:NSA_WIDTH], w[:, col(3)], w[:, col(5)],
                              w[:, g0:g0 + N_GATES], pad], axis=1).T.astype(BF16)
    return w_tok, w_feat


def _gelu_tanh(x):
    c = np.float32(np.sqrt(2.0 / np.pi))
    return 0.5 * x * (1.0 + jnp.tanh(c * (x + 0.044715 * (x * x * x))))


def _compress_kernel(kcin_ref, vcin_ref, pek_ref, wk1_ref, wk2_ref, pev_ref, wv1_ref, wv2_ref,
                     kc_ref, vct_ref, *, n_rows):
    nl = CMP_STRIDE
    hid2 = KV_GROUPS * CMP_HIDDEN

    def hidden(src_ref, pe_ref, w1_ref):
        acc = jnp.zeros((n_rows + 16, 2 * hid2), F32)
        for l in range(nl):
            xl = src_ref[pl.ds(l, n_rows, stride=nl), :]
            pa = jnp.broadcast_to(pe_ref[l:l + 1, :], (8, KV_WIDTH))
            pb = jnp.broadcast_to(pe_ref[nl + l:nl + l + 1, :], (8, KV_WIDTH))
            lhs = jnp.concatenate([xl, pa, pb], axis=0).astype(BF16)
            acc = acc + _dot(lhs, w1_ref[l])
        sa = acc[0:n_rows, 0:hid2]
        sb = acc[0:n_rows, hid2:2 * hid2]
        bias = acc[n_rows:n_rows + 1, 0:hid2] + acc[n_rows + 8:n_rows + 9, hid2:2 * hid2]
        hcur = sa + pltpu.roll(sb, n_rows - 1, axis=0) + bias
        return _gelu_tanh(hcur).astype(BF16)

    hk = hidden(kcin_ref, pek_ref, wk1_ref)
    kc = _dot(hk, wk2_ref[...])
    for g in range(KV_GROUPS):
        kc_ref[g] = kc[:, g * LANES:(g + 1) * LANES].astype(BF16)
    hv = hidden(vcin_ref, pev_ref, wv1_ref)
    for g in range(KV_GROUPS):
        vct_ref[g] = _dot_nt(wv2_ref[g], hv).astype(BF16)


def _compress(kcin, vcin, pek2, wk1b, wk2b, pev2, wv1b, wv2t):
    b, t, _ = kcin.shape
    n_rows = t // CMP_STRIDE
    seq_spec = pl.BlockSpec((None, t, KV_WIDTH), lambda i: (i, 0, 0))
    return pl.pallas_call(
        functools.partial(_compress_kernel, n_rows=n_rows),
        grid=(b,),
        in_specs=[seq_spec, seq_spec,
                  _const_spec(pek2.shape), _const_spec(wk1b.shape), _const_spec(wk2b.shape),
                  _const_spec(pev2.shape), _const_spec(wv1b.shape), _const_spec(wv2t.shape)],
        out_specs=[pl.BlockSpec((None, KV_GROUPS, n_rows, LANES), lambda i: (i, 0, 0, 0)),
                   pl.BlockSpec((None, KV_GROUPS, HEAD_DIM, n_rows), lambda i: (i, 0, 0, 0))],
        out_shape=[jax.ShapeDtypeStruct((b, KV_GROUPS, n_rows, LANES), BF16),
                   jax.ShapeDtypeStruct((b, KV_GROUPS, HEAD_DIM, n_rows), BF16)],
        compiler_params=pltpu.CompilerParams(
            dimension_semantics=("parallel",), vmem_limit_bytes=VMEM_LIMIT),
        name="compress",
    )(kcin, vcin, pek2, wk1b, wk2b, pev2, wv1b, wv2t)


def _compress_weights(pe, w1, w2, value_layout):
    nl, dh, hid = CMP_STRIDE, HEAD_DIM, CMP_HIDDEN
    w1r = w1.reshape(2, nl, dh, hid)
    z = jnp.zeros((nl, dh, hid), w1.dtype)
    row_g0 = jnp.concatenate([w1r[0], z, w1r[1], z], axis=-1)
    row_g1 = jnp.concatenate([z, w1r[0], z, w1r[1]], axis=-1)
    w1b = jnp.concatenate([row_g0, row_g1], axis=1).astype(BF16)
    pe2 = jnp.concatenate([pe, pe], axis=-1)
    zz = jnp.zeros((hid, dh), w2.dtype)
    if value_layout:
        w2b = jnp.stack([jnp.concatenate([w2, zz], axis=0).T,
                         jnp.concatenate([zz, w2], axis=0).T]).astype(BF16)
    else:
        top = jnp.concatenate([w2, zz, zz, zz], axis=-1)
        bot = jnp.concatenate([zz, zz, w2, zz], axis=-1)
        w2b = jnp.concatenate([top, bot], axis=0).astype(BF16)
    return pe2, w1b, w2b


def _nsa_tables(t):
    slopes = _slopes().reshape(KV_GROUPS, HEADS_PER_GROUP)
    n_rows = t // CMP_STRIDE
    n_blk = t // SEL_BLOCK
    i = np.arange(Q_BLOCK)
    c = np.arange(n_rows)
    d0c = (i[None, :] - (c[:, None] * CMP_STRIDE + CMP_BLOCK - 1)).astype(np.int32)
    tblc = -LOG2E * slopes[:, None, :, None].astype(np.float64) * d0c[None, :, None, :]
    tblc = tblc.astype(np.float32)
    tblc = tblc.reshape(KV_GROUPS, n_rows, GQ)
    d0c = np.tile(d0c, (1, HEADS_PER_GROUP))
    j = np.arange(WIN_KEYS)
    dist = i[None, :] + WINDOW - j[:, None]
    ok = (dist >= 0) & (dist < WINDOW)
    tblw = np.where(ok[None, :, None, :],
                    -LOG2E * slopes[:, None, :, None].astype(np.float64) * dist[None, :, None, :],
                    NEG).astype(np.float32).reshape(KV_GROUPS, WIN_KEYS, GQ)
    jj = np.arange(SEL_CHUNK)
    per = SEL_CHUNK // Q_BLOCK
    tailm = np.stack([np.where(jj[:, None] > i[None, :] + Q_BLOCK * r, np.float32(NEG), np.float32(0))
                      for r in range(per)]).astype(np.float32)
    ci = c[None, :] * CMP_STRIDE
    sj = np.arange(n_blk)[:, None] * SEL_BLOCK
    n_cmp = (t - CMP_BLOCK) // CMP_STRIDE + 1
    ov = ((ci <= sj + SEL_BLOCK - 1) & (ci + CMP_BLOCK - 1 >= sj) & (c[None, :] < n_cmp))
    ov = ov.astype(np.float32)
    pos = np.arange(t)
    ktag = np.zeros((t, 2 * LANES), np.float32)
    ktag[pos, HEAD_DIM + pos // SEL_BLOCK] = 1.0
    ktag[pos, LANES + pos // SEL_BLOCK] = 1.0
    ktag[:, LANES + HEAD_DIM] = pos % SEL_BLOCK
    ktag[:, LANES + HEAD_DIM + 1] = pos % SEL_BLOCK
    qflag = np.zeros((HEAD_DIM, GQ), np.float32)
    qflag[0, :] = NEG
    s2 = np.repeat(slopes, Q_BLOCK, axis=1).astype(np.float64) * LOG2E
    s2_hi = s2.astype(BF16).astype(np.float64)
    qslope = np.zeros((KV_GROUPS, HEAD_DIM, GQ), np.float32)
    qslope[:, 0, :] = s2_hi
    qslope[:, 1, :] = s2 - s2_hi
    return (jnp.asarray(tblc), jnp.asarray(d0c), jnp.asarray(tblw), jnp.asarray(tailm),
            jnp.asarray(ov, dtype=BF16), jnp.asarray(ktag, dtype=BF16),
            jnp.asarray(qflag, dtype=BF16), jnp.asarray(qslope, dtype=BF16))


def _nsa_kernel(qt_ref, gt_ref, ks_ref, kw_ref, vst_ref, vwt_ref, kc_ref, vct_ref,
                tblc_ref, d0c_ref, tblw_ref, tailm_ref, ov_ref, ktag_ref, qflag_ref, qslope_ref,
                o_ref,
                kaug_ref, kwaug_ref, m_ref, acc_ref, need_ref, *, t, n_blk):
    qi = pl.program_id(1)
    slopes = _slopes()
    hg = HEADS_PER_GROUP
    prep_rows = 512

    @pl.when(qi == 0)
    def _prep():
        low = lax.broadcasted_iota(jnp.int32, (prep_rows, LANES), 1) < HEAD_DIM
        flag = jnp.where(lax.broadcasted_iota(jnp.int32, (WINDOW, LANES), 1) == HEAD_DIM, 1.0, 0.0)
        for g in range(KV_GROUPS):
            kwaug_ref[g, 0:WINDOW, :] = flag.astype(BF16)

        def body(c, carry):
            r0 = pl.multiple_of(c * prep_rows, prep_rows)
            rows = pl.ds(r0, prep_rows)
            rows_w = pl.ds(r0 + WINDOW, prep_rows)
            tag = ktag_ref[rows, 0:LANES].astype(F32)
            ks = ks_ref[rows, :].astype(F32)
            kw = kw_ref[rows, :].astype(F32)
            ks_sw = pltpu.roll(ks, HEAD_DIM, axis=1)
            kw_sw = pltpu.roll(kw, HEAD_DIM, axis=1)
            for g in range(KV_GROUPS):
                kaug_ref[g, rows, 0:LANES] = jnp.where(low, ks if g == 0 else ks_sw, tag).astype(BF16)
                kaug_ref[g, rows, LANES:2 * LANES] = ktag_ref[rows, LANES:2 * LANES]
                kwaug_ref[g, rows_w, :] = jnp.where(low, kw if g == 0 else kw_sw, 0.0).astype(BF16)
            return carry

        lax.fori_loop(0, t // prep_rows, body, 0)

    gates = jax.nn.sigmoid(gt_ref[...])
    bi = lax.broadcasted_iota(jnp.int32, (n_blk, Q_BLOCK), 0)
    lq = lax.broadcasted_iota(jnp.int32, (n_blk, Q_BLOCK), 1)
    cur = (Q_BLOCK // SEL_BLOCK) * qi + lax.shift_right_logical(lq, SEL_BLOCK.bit_length() - 1)
    valid = bi <= cur
    near = (bi == 0) | (bi >= cur - 1)
    always = valid & (near | (cur < N_SEL))
    free = valid & jnp.logical_not(near)
    later = bi >= FIRST_KEYS // SEL_BLOCK
    rel_blk = (bi - cur).astype(F32)

    head_slabs = [slice(h * Q_BLOCK, (h + SLAB_HEADS) * Q_BLOCK) for h in range(0, hg, SLAB_HEADS)]

    def with_ones(vt):
        first = lax.broadcasted_iota(jnp.int32, (ONES_ROWS, vt.shape[1]), 0) == 0
        return jnp.concatenate([vt, jnp.where(first, 1.0, 0.0).astype(BF16)], axis=0)

    def queries(g):
        heads = lambda ref: jnp.concatenate([ref[(g * hg + h) * HEAD_DIM:(g * hg + h + 1) * HEAD_DIM, :]
                                             for h in range(hg)], axis=1)
        qt = heads(qt_ref)
        return qt, jnp.concatenate([qt, qflag_ref[...]], axis=0)

    def select(g, qt, s):
        o_parts, psum = [], None
        for lanes in head_slabs:
            sh = s[:, lanes] + tblc_ref[g, :, lanes]
            sh = jnp.where(d0c_ref[:, lanes] + Q_BLOCK * qi >= 0, sh, NEG)
            m = jnp.max(sh, axis=0, keepdims=True)
            m = jnp.where(m < 0.5 * NEG, 0.0, m)
            p = jnp.exp2(sh - m)
            l = jnp.sum(p, axis=0, keepdims=True)
            pn = p * (1.0 / jnp.maximum(l, 1e-30))
            o_parts.append(_dot(vct_ref[g], pn.astype(BF16)))
            for h in range(SLAB_HEADS):
                ph = pn[:, h * Q_BLOCK:(h + 1) * Q_BLOCK]
                psum = ph if psum is None else psum + ph
        o_cmp = jnp.concatenate(o_parts, axis=1)

        p_hi = psum.astype(BF16)
        p_lo = (psum - p_hi.astype(F32)).astype(BF16)
        imp = _dot(ov_ref[...], p_hi) + _dot(ov_ref[...], p_lo)
        val = jnp.where(free, imp, -1.0)
        sel = jnp.zeros((n_blk, Q_BLOCK), F32)
        for _ in range(N_SEL - 3):
            top = jnp.max(val, axis=0, keepdims=True)
            idx = jnp.min(jnp.where(val == top, bi, n_blk), axis=0, keepdims=True)
            pick = bi == idx
            sel = jnp.where(pick, 1.0, sel)
            val = jnp.where(pick, -2.0, val)
        chosen = jnp.where(always, 1.0, jnp.where(free, sel, 0.0)) > 0.0

        def tagged(blocks):
            bias = jnp.concatenate(
                [(float(slopes[g * hg + h]) * SEL_BLOCK * LOG2E) * rel_blk for h in range(hg)], axis=1)
            on = jnp.concatenate([blocks] * hg, axis=1)
            hi = bias.astype(BF16)
            lo = (bias - hi.astype(F32)).astype(BF16)
            pad = [jnp.zeros((HEAD_DIM - n_blk, GQ), BF16)] if n_blk < HEAD_DIM else []
            parts = [qt, jnp.where(on, hi, NEG)] + pad + [jnp.where(on, lo, 0.0)] + pad + [qslope_ref[g]]
            return jnp.concatenate(parts, axis=0)

        return o_cmp, tagged(chosen), tagged(chosen & later), jnp.where(chosen & later, 1.0, 0.0)

    def sel_scores(g, q, chunk, mask):
        keys = kaug_ref[g, pl.ds(pl.multiple_of(chunk * SEL_CHUNK, SEL_CHUNK), SEL_CHUNK), :]
        if mask is None:
            return _dot(keys, q)
        keys = jnp.concatenate([keys, kaug_ref[g, 0:FIRST_KEYS, :]], axis=0)
        return _dot(keys, q) + mask

    def sel_update(g, s, chunk, carry, with_first):
        m, acc = carry
        per_chunk = SEL_CHUNK // LANES
        slabs = [chunk * per_chunk + k for k in range(per_chunk)]
        if with_first:
            slabs += list(range(FIRST_KEYS // LANES))
        vt = with_ones(jnp.concatenate([vst_ref[k, g * HEAD_DIM:(g + 1) * HEAD_DIM, :] for k in slabs], axis=1))
        ms, accs = [], []
        for lanes in head_slabs:
            sh, mh = s[:, lanes], m[:, lanes]
            m_new = jnp.maximum(mh, jnp.max(sh, axis=0, keepdims=True))
            alpha = jnp.exp2(mh - m_new)
            p = jnp.exp2(sh - m_new).astype(BF16)
            ms.append(m_new)
            accs.append(acc[:, lanes] * alpha + _dot(vt, p))
        return jnp.concatenate(ms, axis=1), jnp.concatenate(accs, axis=1)

    def sel_steps(chunk, mask, carries):
        q = qaug if mask is not None else qfar
        scores = [sel_scores(g, q[g], chunk, mask) for g in groups]
        return tuple(sel_update(g, scores[g], chunk, carries[g], mask is not None) for g in groups)

    def window_scores(g, qw):
        rows = pl.ds(pl.multiple_of(qi * Q_BLOCK, Q_BLOCK), WIN_KEYS)
        return _dot(kwaug_ref[g, rows, :], qw)

    def window(g, s):
        back = WINDOW // LANES
        ahead = Q_BLOCK // LANES
        vt = jnp.concatenate([vwt_ref[jnp.maximum(qi * ahead - back + k, 0), g * HEAD_DIM:(g + 1) * HEAD_DIM, :]
                              for k in range(back + ahead)], axis=1)
        vt = with_ones(vt)
        outs = []
        for lanes in head_slabs:
            sh = s[:, lanes] + tblw_ref[g, :, lanes]
            m = jnp.max(sh, axis=0, keepdims=True)
            p = jnp.exp2(sh - m).astype(BF16)
            acc = _dot(vt, p)
            outs.append(acc[0:HEAD_DIM] * (1.0 / acc[HEAD_DIM:HEAD_DIM + 1]))
        return jnp.concatenate(outs, axis=1)

    groups = tuple(range(KV_GROUPS))
    qt, qw = zip(*[queries(g) for g in groups])
    s_cmp = [_dot(kc_ref[g], qw[g]) for g in groups]
    s_win = [window_scores(g, qw[g]) for g in groups]
    sel_out, o_win = [], []
    for g in groups:
        sel_out.append(select(g, qt[g], s_cmp[g]))
        o_win.append(window(g, s_win[g]))
    o_cmp, qaug, qfar, picked = zip(*sel_out)

    per = SEL_CHUNK // Q_BLOCK
    blk_per_chunk = SEL_CHUNK // SEL_BLOCK
    n_full = qi // per
    wanted = picked[0]
    for g in groups[1:]:
        wanted = wanted + picked[g]
    for c in range(n_blk // blk_per_chunk - 1):
        hit = jnp.sum(wanted[c * blk_per_chunk:(c + 1) * blk_per_chunk, :])
        need_ref[c] = (hit > 0.0).astype(jnp.int32)
    def load_carries():
        return tuple((m_ref[g, 0:1, :], acc_ref[g]) for g in groups)

    def store_carries(carries):
        for g, (m, acc) in enumerate(carries):
            m_ref[g, 0:1, :] = m
            acc_ref[g] = acc

    first_mask = jnp.where(n_full == 0, NEG, 0.0) + jnp.zeros((FIRST_KEYS, GQ), F32)
    tail_mask = jnp.concatenate([jnp.concatenate([tailm_ref[qi % per]] * hg, axis=1), first_mask], axis=0)
    init = (jnp.full((1, GQ), NEG, F32), jnp.zeros((HEAD_DIM + ONES_ROWS, GQ), F32))
    store_carries(sel_steps(n_full, tail_mask, tuple(init for _ in groups)))

    def full_chunk(c, carry):
        @pl.when(need_ref[c] > 0)
        def _():
            store_carries(sel_steps(c, None, load_carries()))
        return carry

    lax.fori_loop(0, n_full, full_chunk, 0)
    o_slc = [acc[0:HEAD_DIM] * (1.0 / acc[HEAD_DIM:HEAD_DIM + 1]) for _, acc in load_carries()]

    for g in groups:
        for pr in range(hg // 2):
            outs = []
            for h in (2 * pr, 2 * pr + 1):
                hh = g * hg + h
                lanes = slice(h * Q_BLOCK, (h + 1) * Q_BLOCK)
                out = gates[3 * hh:3 * hh + 1, :] * o_cmp[g][:, lanes]
                out = out + gates[3 * hh + 1:3 * hh + 2, :] * o_slc[g][:, lanes]
                out = out + gates[3 * hh + 2:3 * hh + 3, :] * o_win[g][:, lanes]
                outs.append(out)
            pair = jnp.concatenate(outs, axis=0)
            o_ref[:, (g * 2 + pr) * LANES:(g * 2 + pr + 1) * LANES] = pair.T.astype(BF16)


def _nsa(qt, gt, ks, kw, vst, vwt, kc, vct, tables):
    b, _, t = qt.shape
    n_blk = t // SEL_BLOCK
    n_rows = t // CMP_STRIDE
    assert t % SEL_CHUNK == 0 and SEL_CHUNK % Q_BLOCK == 0
    assert N_SEL <= n_blk <= HEAD_DIM and n_blk % 16 == 0
    seq_spec = pl.BlockSpec((None, t, LANES), lambda i, j: (i, 0, 0))
    chunked = pl.BlockSpec((None, t // LANES, KV_WIDTH, LANES), lambda i, j: (i, 0, 0, 0))
    return pl.pallas_call(
        functools.partial(_nsa_kernel, t=t, n_blk=n_blk),
        grid=(b, t // Q_BLOCK),
        in_specs=[pl.BlockSpec((None, NSA_WIDTH, Q_BLOCK), lambda i, j: (i, 0, j)),
                  pl.BlockSpec((None, GATE_ROWS, Q_BLOCK), lambda i, j: (i, 0, j)),
                  seq_spec, seq_spec, chunked, chunked,
                  pl.BlockSpec((None, KV_GROUPS, n_rows, LANES), lambda i, j: (i, 0, 0, 0)),
                  pl.BlockSpec((None, KV_GROUPS, HEAD_DIM, n_rows), lambda i, j: (i, 0, 0, 0))]
                 + [_const_spec(a.shape) for a in tables],
        out_specs=pl.BlockSpec((None, Q_BLOCK, NSA_WIDTH), lambda i, j: (i, j, 0)),
        out_shape=jax.ShapeDtypeStruct((b, t, NSA_WIDTH), BF16),
        scratch_shapes=[pltpu.VMEM((KV_GROUPS, t, 2 * LANES), BF16),
                        pltpu.VMEM((KV_GROUPS, t + WINDOW, LANES), BF16),
                        pltpu.VMEM((KV_GROUPS, 8, GQ), F32),
                        pltpu.VMEM((KV_GROUPS, HEAD_DIM + ONES_ROWS, GQ), F32),
                        pltpu.SMEM((t // SEL_CHUNK,), jnp.int32)],
        compiler_params=pltpu.CompilerParams(
            dimension_semantics=("arbitrary", "arbitrary"), vmem_limit_bytes=VMEM_LIMIT),
        name="nsa",
    )(qt, gt, ks, kw, vst, vwt, kc, vct, *tables)


def _mix_ffn_kernel(x_ref, o_ref, u_ref, halo_ref, wp_ref, sc_ref, wo_ref, g2_ref, wg_ref, wu_ref, wd_ref,
                    *rest, tm, final):
    if final:
        fn_ref, y_ref, mix_ref, h_ref, acc_ref = rest
    else:
        y_ref, mix_ref, h_ref, acc_ref = rest
    ti = pl.program_id(1)
    mix_ref[:, 0:NSA_WIDTH] = o_ref[...]
    halo = jnp.where(ti > 0, halo_ref[...], 0.0)
    pos = ti * tm + lax.broadcasted_iota(jnp.int32, (tm, 1), 0)
    for gi, w in enumerate(POOL_WIDTHS):
        lanes = slice(gi * POOL_GROUP, (gi + 1) * POOL_GROUP)
        u = u_ref[:, lanes]
        ext = jnp.concatenate([halo[:, lanes], u], axis=0)
        run = ext
        step = 1
        while step < w:
            run = run + pltpu.roll(run, step, axis=0)
            step *= 2
        cnt = jnp.minimum(pos + 1, w).astype(F32)
        y = run[POOL_HALO:, :] / cnt - u
        yo = _dot(y.astype(BF16), wp_ref[gi]) * sc_ref[:, lanes]
        mix_ref[:, NSA_WIDTH + gi * POOL_GROUP:NSA_WIDTH + (gi + 1) * POOL_GROUP] = yo.astype(BF16)
    x2 = x_ref[...] + _dot(mix_ref[...], wo_ref[...])
    y = _swiglu_residual(x2, g2_ref, wg_ref, wu_ref, wd_ref, h_ref, acc_ref)
    if final:
        y = _rms(y, fn_ref[...])
    y_ref[...] = y


def _mix_ffn(x3, o_nsa, u, wp, sc, wo, g2, wg, wu, wd, final_g=None, tm=512):
    b, t, d = x3.shape
    per = tm // POOL_HALO
    final = final_g is not None
    tail_specs = [_resident((1, d))] if final else []
    tail_args = [final_g] if final else []
    return pl.pallas_call(
        functools.partial(_mix_ffn_kernel, tm=tm, final=final),
        grid=(b, t // tm),
        in_specs=[pl.BlockSpec((None, tm, d), lambda i, j: (i, j, 0)),
                  pl.BlockSpec((None, tm, NSA_WIDTH), lambda i, j: (i, j, 0)),
                  pl.BlockSpec((None, tm, POOL_CH), lambda i, j: (i, j, 0)),
                  pl.BlockSpec((None, POOL_HALO, POOL_CH), lambda i, j: (i, jnp.maximum(j * per - 1, 0), 0)),
                  _resident(wp.shape), _resident(sc.shape), _resident(wo.shape),
                  _resident((1, d)), _resident(wg.shape), _resident(wu.shape), _resident(wd.shape)] + tail_specs,
        out_specs=pl.BlockSpec((None, tm, d), lambda i, j: (i, j, 0)),
        out_shape=jax.ShapeDtypeStruct((b, t, d), F32),
        scratch_shapes=[pltpu.VMEM((tm, NSA_WIDTH + POOL_CH), BF16),
                        pltpu.VMEM((tm, d), BF16), pltpu.VMEM((tm, d), F32)],
        compiler_params=pltpu.CompilerParams(
            dimension_semantics=("parallel", "parallel"), vmem_limit_bytes=VMEM_LIMIT),
        name="mix_ffn_final" if final else "mix_ffn",
    )(x3, o_nsa, u, u, wp, sc, wo, g2, wg, wu, wd, *tail_args)


def kernel(x, ffn1_norm, ffn1_wg, ffn1_wu, ffn1_wd, mix_norm, w_in, cmp_pe_k, cmp_wk1, cmp_wk2,
           cmp_pe_v, cmp_wv1, cmp_wv2, pool_w, pool_scale, w_out, ffn2_norm, ffn2_wg, ffn2_wu,
           ffn2_wd, final_norm):
    b, t, d = x.shape
    depth = ffn1_norm.shape[0]
    tables = _nsa_tables(t)
    f1g, f1u, f1d, f2g, f2u, f2d, wp, wo = (
        a.astype(BF16) for a in (ffn1_wg, ffn1_wu, ffn1_wd, ffn2_wg, ffn2_wu, ffn2_wd, pool_w, w_out))
    for l in range(depth):
        x, kcin, vcin, ks, kw, u, qt, vst, vwt, gt = _ffn_proj(
            x, ffn1_norm[l][None], f1g[l], f1u[l], f1d[l], mix_norm[l][None], *_split_w_in(w_in[l]))
        kc, vct = _compress(kcin, vcin,
                            *_compress_weights(cmp_pe_k[l], cmp_wk1[l], cmp_wk2[l], False),
                            *_compress_weights(cmp_pe_v[l], cmp_wv1[l], cmp_wv2[l], True))
        o_nsa = _nsa(qt, gt, ks, kw, vst, vwt, kc, vct, tables)
        x = _mix_ffn(x, o_nsa, u, wp[l], pool_scale[l][None], wo[l], ffn2_norm[l][None], f2g[l], f2u[l], f2d[l],
                     final_g=final_norm[None] if l == depth - 1 else None)
    return x
```

```python
import functools

import numpy as np
import jax
import jax.numpy as jnp
from jax import lax
from jax.experimental import pallas as pl
from jax.experimental.pallas import tpu as pltpu

HEAD_DIM = 64
NSA_HEADS = 8
KV_GROUPS = 2
HEADS_PER_GROUP = NSA_HEADS // KV_GROUPS
NSA_WIDTH = NSA_HEADS * HEAD_DIM
KV_WIDTH = KV_GROUPS * HEAD_DIM
CMP_BLOCK = 32
CMP_STRIDE = 16
CMP_HIDDEN = 128
SEL_BLOCK = 64
N_SEL = 8
WINDOW = 512
Q_BLOCK = 256
POOL_WIDTHS = (2, 4, 8, 16)
POOL_GROUP = 128
POOL_CH = POOL_GROUP * len(POOL_WIDTHS)
N_GATES = 3 * NSA_HEADS
GATE_ROWS = 32
EPS = 1e-6

LANES = 128
GQ = HEADS_PER_GROUP * Q_BLOCK
SEL_CHUNK = 256
PREV_KEYS = 512
FAR_CHUNK = 512
SLAB_HEADS = 2
FIRST_KEYS = 128
WIN_KEYS = WINDOW + Q_BLOCK
ONES_ROWS = 16
POOL_HALO = 16
FF_CHUNK = 256
LOG2E = 1.4426950408889634
NEG = -1e30
VMEM_LIMIT = 56 * 1024 * 1024

F32 = jnp.float32
BF16 = jnp.bfloat16


def _slopes():
    n = NSA_HEADS
    return (2.0 ** (-8.0 * np.arange(1, n + 1) / n)).astype(np.float32)


def _dot(a, b):
    return jnp.dot(a, b, preferred_element_type=F32)


def _dot_nt(a, b):
    return lax.dot_general(a, b, (((1,), (1,)), ((), ())), preferred_element_type=F32)


def _rms(x, g):
    return x * lax.rsqrt(jnp.mean(x * x, axis=-1, keepdims=True) + EPS) * g


def _const_spec(shape):
    nd = len(shape)
    return pl.BlockSpec(shape, lambda *_: (0,) * nd)


def _swiglu_residual(x, g_ref, wg_ref, wu_ref, wd_ref, h_ref, acc_ref):
    h_ref[...] = _rms(x, g_ref[...]).astype(BF16)
    for c in range(wg_ref.shape[1] // FF_CHUNK):
        cols = slice(c * FF_CHUNK, (c + 1) * FF_CHUNK)
        h = h_ref[...]
        gate = _dot(h, wg_ref[:, cols])
        up = _dot(h, wu_ref[:, cols])
        a = (gate * jax.nn.sigmoid(gate) * up).astype(BF16)
        part = _dot(a, wd_ref[cols, :])
        if c == 0:
            acc_ref[...] = part
        else:
            acc_ref[...] += part
    return x + 0.5 * acc_ref[...]


def _resident(shape):
    nd = len(shape)
    return pl.BlockSpec(shape, lambda *_: (0,) * nd, pipeline_mode=pl.Buffered(1))


_TOK_KC = (0, 128)
_TOK_VC = (128, 256)
_TOK_KS = (256, 384)
_TOK_KW = (384, 512)
_TOK_U = (512, 1024)
_FEAT_Q = (0, 512)
_FEAT_VS = (512, 640)
_FEAT_VW = (640, 768)
_FEAT_GT = (768, 768 + GATE_ROWS)


def _ffn_proj_kernel(x_ref, g1_ref, wg_ref, wu_ref, wd_ref, g_ref, wt_ref, wf_ref,
                     x1_ref, kcin_ref, vcin_ref, ks_ref, kw_ref, u_ref,
                     qt_ref, vst_ref, vwt_ref, gt_ref, h_ref, acc_ref, *, tm):
    x1 = _swiglu_residual(x_ref[...], g1_ref, wg_ref, wu_ref, wd_ref, h_ref, acc_ref)
    x1_ref[...] = x1
    h = _rms(x1, g_ref[...]).astype(BF16)

    def tok(lo_hi):
        return _dot(h, wt_ref[:, lo_hi[0]:lo_hi[1]])

    kcin_ref[...] = tok(_TOK_KC)
    vcin_ref[...] = tok(_TOK_VC)
    ks_ref[...] = tok(_TOK_KS).astype(BF16)
    kw_ref[...] = tok(_TOK_KW).astype(BF16)
    u_ref[...] = tok(_TOK_U)
    zt = _dot_nt(wf_ref[...], h)
    qt_ref[...] = (zt[_FEAT_Q[0]:_FEAT_Q[1]] * (HEAD_DIM ** -0.5 * LOG2E)).astype(BF16)
    gt_ref[...] = zt[_FEAT_GT[0]:_FEAT_GT[1]]
    for c in range(tm // LANES):
        cols = slice(c * LANES, (c + 1) * LANES)
        vst_ref[c] = zt[_FEAT_VS[0]:_FEAT_VS[1], cols].astype(BF16)
        vwt_ref[c] = zt[_FEAT_VW[0]:_FEAT_VW[1], cols].astype(BF16)


def _ffn_proj(x3, g1, wg, wu, wd, g, w_tok, w_feat, tm=512):
    b, t, d = x3.shape
    nck = tm // LANES
    tok = lambda wd: pl.BlockSpec((None, tm, wd), lambda i, j: (i, j, 0))
    chunked = pl.BlockSpec((None, nck, KV_WIDTH, LANES), lambda i, j: (i, j, 0, 0))
    return pl.pallas_call(
        functools.partial(_ffn_proj_kernel, tm=tm),
        grid=(b, t // tm),
        in_specs=[tok(d), _resident((1, d)), _resident(wg.shape), _resident(wu.shape), _resident(wd.shape),
                  _resident((1, d)), _resident(w_tok.shape), _resident(w_feat.shape)],
        out_specs=[tok(d), tok(KV_WIDTH), tok(KV_WIDTH), tok(KV_WIDTH), tok(KV_WIDTH), tok(POOL_CH),
                   pl.BlockSpec((None, NSA_WIDTH, tm), lambda i, j: (i, 0, j)),
                   chunked, chunked,
                   pl.BlockSpec((None, GATE_ROWS, tm), lambda i, j: (i, 0, j))],
        out_shape=[jax.ShapeDtypeStruct((b, t, d), F32),
                   jax.ShapeDtypeStruct((b, t, KV_WIDTH), F32),
                   jax.ShapeDtypeStruct((b, t, KV_WIDTH), F32),
                   jax.ShapeDtypeStruct((b, t, KV_WIDTH), BF16),
                   jax.ShapeDtypeStruct((b, t, KV_WIDTH), BF16),
                   jax.ShapeDtypeStruct((b, t, POOL_CH), F32),
                   jax.ShapeDtypeStruct((b, NSA_WIDTH, t), BF16),
                   jax.ShapeDtypeStruct((b, t // LANES, KV_WIDTH, LANES), BF16),
                   jax.ShapeDtypeStruct((b, t // LANES, KV_WIDTH, LANES), BF16),
                   jax.ShapeDtypeStruct((b, GATE_ROWS, t), F32)],
        scratch_shapes=[pltpu.VMEM((tm, d), BF16), pltpu.VMEM((tm, d), F32)],
        compiler_params=pltpu.CompilerParams(
            dimension_semantics=("parallel", "parallel"), vmem_limit_bytes=VMEM_LIMIT),
        name="ffn_proj",
    )(x3, g1, wg, wu, wd, g, w_tok, w_feat)


def _split_w_in(w):
    q0, kv0 = 0, NSA_WIDTH
    col = lambda i: slice(kv0 + i * KV_WIDTH, kv0 + (i + 1) * KV_WIDTH)
    g0 = kv0 + 6 * KV_WIDTH
    w_tok = jnp.concatenate([w[:, col(0)], w[:, col(1)], w[:, col(2)], w[:, col(4)],
                             w[:, g0 + N_GATES:]], axis=1).astype(BF16)
    pad = jnp.zeros((w.shape[0], GATE_ROWS - N_GATES), w.dtype)
    w_feat = jnp.concatenate([w[:, q0:NSA_WIDTH], w[:, col(3)], w[:, col(5)],
                              w[:, g0:g0 + N_GATES], pad], axis=1).T.astype(BF16)
    return w_tok, w_feat


def _gelu_tanh(x):
    c = np.float32(np.sqrt(2.0 / np.pi))
    return 0.5 * x * (1.0 + jnp.tanh(c * (x + 0.044715 * (x * x * x))))


def _compress_kernel(kcin_ref, vcin_ref, pek_ref, wk1_ref, wk2_ref, pev_ref, wv1_ref, wv2_ref,
                     kc_ref, vct_ref, *, n_rows):
    nl = CMP_STRIDE
    hid2 = KV_GROUPS * CMP_HIDDEN

    def hidden(src_ref, pe_ref, w1_ref):
        acc = jnp.zeros((n_rows + 16, 2 * hid2), F32)
        for l in range(nl):
            xl = src_ref[pl.ds(l, n_rows, stride=nl), :]
            pa = jnp.broadcast_to(pe_ref[l:l + 1, :], (8, KV_WIDTH))
            pb = jnp.broadcast_to(pe_ref[nl + l:nl + l + 1, :], (8, KV_WIDTH))
            lhs = jnp.concatenate([xl, pa, pb], axis=0).astype(BF16)
            acc = acc + _dot(lhs, w1_ref[l])
        sa = acc[0:n_rows, 0:hid2]
        sb = acc[0:n_rows, hid2:2 * hid2]
        bias = acc[n_rows:n_rows + 1, 0:hid2] + acc[n_rows + 8:n_rows + 9, hid2:2 * hid2]
        hcur = sa + pltpu.roll(sb, n_rows - 1, axis=0) + bias
        return _gelu_tanh(hcur).astype(BF16)

    hk = hidden(kcin_ref, pek_ref, wk1_ref)
    kc = _dot(hk, wk2_ref[...])
    for g in range(KV_GROUPS):
        kc_ref[g] = kc[:, g * LANES:(g + 1) * LANES].astype(BF16)
    hv = hidden(vcin_ref, pev_ref, wv1_ref)
    for g in range(KV_GROUPS):
        vct_ref[g] = _dot_nt(wv2_ref[g], hv).astype(BF16)


def _compress(kcin, vcin, pek2, wk1b, wk2b, pev2, wv1b, wv2t):
    b, t, _ = kcin.shape
    n_rows = t // CMP_STRIDE
    seq_spec = pl.BlockSpec((None, t, KV_WIDTH), lambda i: (i, 0, 0))
    return pl.pallas_call(
        functools.partial(_compress_kernel, n_rows=n_rows),
        grid=(b,),
        in_specs=[seq_spec, seq_spec,
                  _const_spec(pek2.shape), _const_spec(wk1b.shape), _const_spec(wk2b.shape),
                  _const_spec(pev2.shape), _const_spec(wv1b.shape), _const_spec(wv2t.shape)],
        out_specs=[pl.BlockSpec((None, KV_GROUPS, n_rows, LANES), lambda i: (i, 0, 0, 0)),
                   pl.BlockSpec((None, KV_GROUPS, HEAD_DIM, n_rows), lambda i: (i, 0, 0, 0))],
        out_shape=[jax.ShapeDtypeStruct((b, KV_GROUPS, n_rows, LANES), BF16),
                   jax.ShapeDtypeStruct((b, KV_GROUPS, HEAD_DIM, n_rows), BF16)],
        compiler_params=pltpu.CompilerParams(
            dimension_semantics=("parallel",), vmem_limit_bytes=VMEM_LIMIT),
        name="compress",
    )(kcin, vcin, pek2, wk1b, wk2b, pev2, wv1b, wv2t)


def _compress_weights(pe, w1, w2, value_layout):
    nl, dh, hid = CMP_STRIDE, HEAD_DIM, CMP_HIDDEN
    w1r = w1.reshape(2, nl, dh, hid)
    z = jnp.zeros((nl, dh, hid), w1.dtype)
    row_g0 = jnp.concatenate([w1r[0], z, w1r[1], z], axis=-1)
    row_g1 = jnp.concatenate([z, w1r[0], z, w1r[1]], axis=-1)
    w1b = jnp.concatenate([row_g0, row_g1], axis=1).astype(BF16)
    pe2 = jnp.concatenate([pe, pe], axis=-1)
    zz = jnp.zeros((hid, dh), w2.dtype)
    if value_layout:
        w2b = jnp.stack([jnp.concatenate([w2, zz], axis=0).T,
                         jnp.concatenate([zz, w2], axis=0).T]).astype(BF16)
    else:
        top = jnp.concatenate([w2, zz, zz, zz], axis=-1)
        bot = jnp.concatenate([zz, zz, w2, zz], axis=-1)
        w2b = jnp.concatenate([top, bot], axis=0).astype(BF16)
    return pe2, w1b, w2b


def _nsa_tables(t):
    slopes = _slopes().reshape(KV_GROUPS, HEADS_PER_GROUP)
    n_rows = t // CMP_STRIDE
    n_blk = t // SEL_BLOCK
    i = np.arange(Q_BLOCK)
    c = np.arange(n_rows)
    d0c = (i[None, :] - (c[:, None] * CMP_STRIDE + CMP_BLOCK - 1)).astype(np.int32)
    tblc = -LOG2E * slopes[:, None, :, None].astype(np.float64) * d0c[None, :, None, :]
    tblc = tblc.astype(np.float32)
    tblc = tblc.reshape(KV_GROUPS, n_rows, GQ)
    d0c = np.tile(d0c, (1, HEADS_PER_GROUP))
    j = np.arange(WIN_KEYS)
    dist = i[None, :] + WINDOW - j[:, None]
    ok = (dist >= 0) & (dist < WINDOW)
    tblw = np.where(ok[None, :, None, :],
                    -LOG2E * slopes[:, None, :, None].astype(np.float64) * dist[None, :, None, :],
                    NEG).astype(np.float32).reshape(KV_GROUPS, WIN_KEYS, GQ)
    jj = np.arange(SEL_CHUNK)
    per = SEL_CHUNK // Q_BLOCK
    tailm = np.stack([np.where(jj[:, None] > i[None, :] + Q_BLOCK * r, np.float32(NEG), np.float32(0))
                      for r in range(per)]).astype(np.float32)
    ci = c[None, :] * CMP_STRIDE
    sj = np.arange(n_blk)[:, None] * SEL_BLOCK
    n_cmp = (t - CMP_BLOCK) // CMP_STRIDE + 1
    ov = ((ci <= sj + SEL_BLOCK - 1) & (ci + CMP_BLOCK - 1 >= sj) & (c[None, :] < n_cmp))
    ov = ov.astype(np.float32)
    pos = np.arange(t)
    ktag = np.zeros((t, 2 * LANES), np.float32)
    ktag[pos, HEAD_DIM + pos // SEL_BLOCK] = 1.0
    ktag[pos, LANES + pos // SEL_BLOCK] = 1.0
    ktag[:, LANES + HEAD_DIM] = pos % SEL_BLOCK
    ktag[:, LANES + HEAD_DIM + 1] = pos % SEL_BLOCK
    qflag = np.zeros((HEAD_DIM, GQ), np.float32)
    qflag[0, :] = NEG
    s2 = np.repeat(slopes, Q_BLOCK, axis=1).astype(np.float64) * LOG2E
    s2_hi = s2.astype(BF16).astype(np.float64)
    qslope = np.zeros((KV_GROUPS, HEAD_DIM, GQ), np.float32)
    qslope[:, 0, :] = s2_hi
    qslope[:, 1, :] = s2 - s2_hi
    return (jnp.asarray(tblc), jnp.asarray(d0c), jnp.asarray(tblw), jnp.asarray(tailm),
            jnp.asarray(ov, dtype=BF16), jnp.asarray(ktag, dtype=BF16),
            jnp.asarray(qflag, dtype=BF16), jnp.asarray(qslope, dtype=BF16))


def _nsa_kernel(qt_ref, gt_ref, ks_ref, kw_ref, vst_ref, vwt_ref, kc_ref, vct_ref,
                tblc_ref, d0c_ref, tblw_ref, tailm_ref, ov_ref, ktag_ref, qflag_ref, qslope_ref,
                o_ref,
                kaug_ref, kwaug_ref, m_ref, acc_ref, need_ref, *, t, n_blk):
    qi = pl.program_id(1)
    slopes = _slopes()
    hg = HEADS_PER_GROUP
    prep_rows = 512

    @pl.when(qi == 0)
    def _prep():
        low = lax.broadcasted_iota(jnp.int32, (prep_rows, LANES), 1) < HEAD_DIM
        flag = jnp.where(lax.broadcasted_iota(jnp.int32, (WINDOW, LANES), 1) == HEAD_DIM, 1.0, 0.0)
        for g in range(KV_GROUPS):
            kwaug_ref[g, 0:WINDOW, :] = flag.astype(BF16)

        def body(c, carry):
            r0 = pl.multiple_of(c * prep_rows, prep_rows)
            rows = pl.ds(r0, prep_rows)
            rows_w = pl.ds(r0 + WINDOW, prep_rows)
            tag = ktag_ref[rows, 0:LANES].astype(F32)
            ks = ks_ref[rows, :].astype(F32)
            kw = kw_ref[rows, :].astype(F32)
            ks_sw = pltpu.roll(ks, HEAD_DIM, axis=1)
            kw_sw = pltpu.roll(kw, HEAD_DIM, axis=1)
            for g in range(KV_GROUPS):
                kaug_ref[g, rows, 0:LANES] = jnp.where(low, ks if g == 0 else ks_sw, tag).astype(BF16)
                kaug_ref[g, rows, LANES:2 * LANES] = ktag_ref[rows, LANES:2 * LANES]
                kwaug_ref[g, rows_w, :] = jnp.where(low, kw if g == 0 else kw_sw, 0.0).astype(BF16)
            return carry

        lax.fori_loop(0, t // prep_rows, body, 0)

    gates = jax.nn.sigmoid(gt_ref[...])
    bi = lax.broadcasted_iota(jnp.int32, (n_blk, Q_BLOCK), 0)
    lq = lax.broadcasted_iota(jnp.int32, (n_blk, Q_BLOCK), 1)
    cur = (Q_BLOCK // SEL_BLOCK) * qi + lax.shift_right_logical(lq, SEL_BLOCK.bit_length() - 1)
    valid = bi <= cur
    near = (bi == 0) | (bi >= cur - 1)
    always = valid & (near | (cur < N_SEL))
    free = valid & jnp.logical_not(near)
    far = (bi >= FIRST_KEYS // SEL_BLOCK) & (bi < (qi * Q_BLOCK - PREV_KEYS) // SEL_BLOCK)
    rel_blk = (bi - cur).astype(F32)

    head_slabs = [slice(h * Q_BLOCK, (h + SLAB_HEADS) * Q_BLOCK) for h in range(0, hg, SLAB_HEADS)]

    def with_ones(vt):
        first = lax.broadcasted_iota(jnp.int32, (ONES_ROWS, vt.shape[1]), 0) == 0
        return jnp.concatenate([vt, jnp.where(first, 1.0, 0.0).astype(BF16)], axis=0)

    def queries(g):
        heads = lambda ref: jnp.concatenate([ref[(g * hg + h) * HEAD_DIM:(g * hg + h + 1) * HEAD_DIM, :]
                                             for h in range(hg)], axis=1)
        qt = heads(qt_ref)
        return qt, jnp.concatenate([qt, qflag_ref[...]], axis=0)

    def select(g, qt, s):
        o_parts, psum = [], None
        for lanes in head_slabs:
            sh = s[:, lanes] + tblc_ref[g, :, lanes]
            sh = jnp.where(d0c_ref[:, lanes] + Q_BLOCK * qi >= 0, sh, NEG)
            m = jnp.max(sh, axis=0, keepdims=True)
            m = jnp.where(m < 0.5 * NEG, 0.0, m)
            p = jnp.exp2(sh - m)
            l = jnp.sum(p, axis=0, keepdims=True)
            pn = p * (1.0 / jnp.maximum(l, 1e-30))
            o_parts.append(_dot(vct_ref[g], pn.astype(BF16)))
            for h in range(SLAB_HEADS):
                ph = pn[:, h * Q_BLOCK:(h + 1) * Q_BLOCK]
                psum = ph if psum is None else psum + ph
        o_cmp = jnp.concatenate(o_parts, axis=1)

        p_hi = psum.astype(BF16)
        p_lo = (psum - p_hi.astype(F32)).astype(BF16)
        imp = _dot(ov_ref[...], p_hi) + _dot(ov_ref[...], p_lo)
        val = jnp.where(free, imp, -1.0)
        sel = jnp.zeros((n_blk, Q_BLOCK), F32)
        for _ in range(N_SEL - 3):
            top = jnp.max(val, axis=0, keepdims=True)
            idx = jnp.min(jnp.where(val == top, bi, n_blk), axis=0, keepdims=True)
            pick = bi == idx
            sel = jnp.where(pick, 1.0, sel)
            val = jnp.where(pick, -2.0, val)
        chosen = jnp.where(always, 1.0, jnp.where(free, sel, 0.0)) > 0.0

        def tagged(blocks):
            bias = jnp.concatenate(
                [(float(slopes[g * hg + h]) * SEL_BLOCK * LOG2E) * rel_blk for h in range(hg)], axis=1)
            on = jnp.concatenate([blocks] * hg, axis=1)
            hi = bias.astype(BF16)
            lo = (bias - hi.astype(F32)).astype(BF16)
            pad = [jnp.zeros((HEAD_DIM - n_blk, GQ), BF16)] if n_blk < HEAD_DIM else []
            parts = [qt, jnp.where(on, hi, NEG)] + pad + [jnp.where(on, lo, 0.0)] + pad + [qslope_ref[g]]
            return jnp.concatenate(parts, axis=0)

        return o_cmp, tagged(chosen), tagged(chosen & far), jnp.where(chosen & far, 1.0, 0.0)

    def sel_keys(g, pieces):
        return jnp.concatenate([kaug_ref[g, pl.ds(pl.multiple_of(s0 * LANES, LANES), n * LANES), :]
                                for s0, n in pieces], axis=0)

    def sel_values(g, pieces):
        return with_ones(jnp.concatenate([vst_ref[s0 + k, g * HEAD_DIM:(g + 1) * HEAD_DIM, :]
                                          for s0, n in pieces for k in range(n)], axis=1))

    def sel_update(s, vt, carry):
        m, acc = carry
        ms, accs = [], []
        for lanes in head_slabs:
            sh, mh = s[:, lanes], m[:, lanes]
            m_new = jnp.maximum(mh, jnp.max(sh, axis=0, keepdims=True))
            alpha = jnp.exp2(mh - m_new)
            p = jnp.exp2(sh - m_new).astype(BF16)
            ms.append(m_new)
            accs.append(acc[:, lanes] * alpha + _dot(vt, p))
        return jnp.concatenate(ms, axis=1), jnp.concatenate(accs, axis=1)

    def sel_steps(pieces, q, mask, carries):
        scores = [_dot(sel_keys(g, pieces), q[g]) for g in groups]
        if mask is not None:
            scores = [jnp.concatenate([sc[0:mask.shape[0]] + mask, sc[mask.shape[0]:]], axis=0) for sc in scores]
        return tuple(sel_update(scores[g], sel_values(g, pieces), carries[g]) for g in groups)

    def window_scores(g, qw):
        rows = pl.ds(pl.multiple_of(qi * Q_BLOCK, Q_BLOCK), WIN_KEYS)
        return _dot(kwaug_ref[g, rows, :], qw)

    def window(g, s):
        back = WINDOW // LANES
        ahead = Q_BLOCK // LANES
        vt = jnp.concatenate([vwt_ref[jnp.maximum(qi * ahead - back + k, 0), g * HEAD_DIM:(g + 1) * HEAD_DIM, :]
                              for k in range(back + ahead)], axis=1)
        vt = with_ones(vt)
        outs = []
        for lanes in head_slabs:
            sh = s[:, lanes] + tblw_ref[g, :, lanes]
            m = jnp.max(sh, axis=0, keepdims=True)
            p = jnp.exp2(sh - m).astype(BF16)
            acc = _dot(vt, p)
            outs.append(acc[0:HEAD_DIM] * (1.0 / acc[HEAD_DIM:HEAD_DIM + 1]))
        return jnp.concatenate(outs, axis=1)

    groups = tuple(range(KV_GROUPS))
    qt, qw = zip(*[queries(g) for g in groups])
    s_cmp = [_dot(kc_ref[g], qw[g]) for g in groups]
    s_win = [window_scores(g, qw[g]) for g in groups]
    sel_out, o_win = [], []
    for g in groups:
        sel_out.append(select(g, qt[g], s_cmp[g]))
        o_win.append(window(g, s_win[g]))
    o_cmp, qaug, qfar, picked = zip(*sel_out)

    nq, n_first = Q_BLOCK // LANES, FIRST_KEYS // LANES
    future = (qi + 1) * nq
    diag_pieces = [(qi * nq, nq)]
    for back in range(1, PREV_KEYS // Q_BLOCK + 1):
        diag_pieces.append((jnp.where(qi >= back, (qi - back) * nq, future), nq))
    diag_pieces.append((jnp.where(qi * Q_BLOCK >= PREV_KEYS + FIRST_KEYS, 0, future), n_first))
    causal = jnp.concatenate([tailm_ref[0]] * hg, axis=1)
    init = (jnp.full((1, GQ), NEG, F32), jnp.zeros((HEAD_DIM + ONES_ROWS, GQ), F32))

    def load_carries():
        return tuple((m_ref[g, 0:1, :], acc_ref[g]) for g in groups)

    def store_carries(carries):
        for g, (m, acc) in enumerate(carries):
            m_ref[g, 0:1, :] = m
            acc_ref[g] = acc

    store_carries(sel_steps(diag_pieces, qaug, causal, tuple(init for _ in groups)))

    blk_per_chunk = FAR_CHUNK // SEL_BLOCK
    wanted = picked[0]
    for g in groups[1:]:
        wanted = wanted + picked[g]
    for c in range(n_blk // blk_per_chunk):
        hit = jnp.sum(wanted[c * blk_per_chunk:(c + 1) * blk_per_chunk, :])
        need_ref[c] = (hit > 0.0).astype(jnp.int32)

    def far_chunk(c, carry):
        @pl.when(need_ref[c] > 0)
        def _():
            store_carries(sel_steps([(c * (FAR_CHUNK // LANES), FAR_CHUNK // LANES)], qfar, None, load_carries()))
        return carry

    far_keys = jnp.maximum(qi * Q_BLOCK - PREV_KEYS, 0)
    lax.fori_loop(0, (far_keys + FAR_CHUNK - 1) // FAR_CHUNK, far_chunk, 0)
    o_slc = [acc[0:HEAD_DIM] * (1.0 / acc[HEAD_DIM:HEAD_DIM + 1]) for _, acc in load_carries()]

    for g in groups:
        for pr in range(hg // 2):
            outs = []
            for h in (2 * pr, 2 * pr + 1):
                hh = g * hg + h
                lanes = slice(h * Q_BLOCK, (h + 1) * Q_BLOCK)
                out = gates[3 * hh:3 * hh + 1, :] * o_cmp[g][:, lanes]
                out = out + gates[3 * hh + 1:3 * hh + 2, :] * o_slc[g][:, lanes]
                out = out + gates[3 * hh + 2:3 * hh + 3, :] * o_win[g][:, lanes]
                outs.append(out)
            pair = jnp.concatenate(outs, axis=0)
            o_ref[:, (g * 2 + pr) * LANES:(g * 2 + pr + 1) * LANES] = pair.T.astype(BF16)


def _nsa(qt, gt, ks, kw, vst, vwt, kc, vct, tables):
    b, _, t = qt.shape
    n_blk = t // SEL_BLOCK
    n_rows = t // CMP_STRIDE
    assert SEL_CHUNK == Q_BLOCK and PREV_KEYS % Q_BLOCK == 0 and t % FAR_CHUNK == 0
    assert t >= PREV_KEYS + 2 * Q_BLOCK
    assert N_SEL <= n_blk <= HEAD_DIM and n_blk % 16 == 0
    seq_spec = pl.BlockSpec((None, t, LANES), lambda i, j: (i, 0, 0))
    chunked = pl.BlockSpec((None, t // LANES, KV_WIDTH, LANES), lambda i, j: (i, 0, 0, 0))
    return pl.pallas_call(
        functools.partial(_nsa_kernel, t=t, n_blk=n_blk),
        grid=(b, t // Q_BLOCK),
        in_specs=[pl.BlockSpec((None, NSA_WIDTH, Q_BLOCK), lambda i, j: (i, 0, j)),
                  pl.BlockSpec((None, GATE_ROWS, Q_BLOCK), lambda i, j: (i, 0, j)),
                  seq_spec, seq_spec, chunked, chunked,
                  pl.BlockSpec((None, KV_GROUPS, n_rows, LANES), lambda i, j: (i, 0, 0, 0)),
                  pl.BlockSpec((None, KV_GROUPS, HEAD_DIM, n_rows), lambda i, j: (i, 0, 0, 0))]
                 + [_const_spec(a.shape) for a in tables],
        out_specs=pl.BlockSpec((None, Q_BLOCK, NSA_WIDTH), lambda i, j: (i, j, 0)),
        out_shape=jax.ShapeDtypeStruct((b, t, NSA_WIDTH), BF16),
        scratch_shapes=[pltpu.VMEM((KV_GROUPS, t, 2 * LANES), BF16),
                        pltpu.VMEM((KV_GROUPS, t + WINDOW, LANES), BF16),
                        pltpu.VMEM((KV_GROUPS, 8, GQ), F32),
                        pltpu.VMEM((KV_GROUPS, HEAD_DIM + ONES_ROWS, GQ), F32),
                        pltpu.SMEM((t // FAR_CHUNK,), jnp.int32)],
        compiler_params=pltpu.CompilerParams(
            dimension_semantics=("arbitrary", "arbitrary"), vmem_limit_bytes=VMEM_LIMIT),
        name="nsa",
    )(qt, gt, ks, kw, vst, vwt, kc, vct, *tables)


def _mix_ffn_kernel(x_ref, o_ref, u_ref, halo_ref, wp_ref, sc_ref, wo_ref, g2_ref, wg_ref, wu_ref, wd_ref,
                    *rest, tm, final):
    if final:
        fn_ref, y_ref, mix_ref, h_ref, acc_ref = rest
    else:
        y_ref, mix_ref, h_ref, acc_ref = rest
    ti = pl.program_id(1)
    mix_ref[:, 0:NSA_WIDTH] = o_ref[...]
    halo = jnp.where(ti > 0, halo_ref[...], 0.0)
    pos = ti * tm + lax.broadcasted_iota(jnp.int32, (tm, 1), 0)
    for gi, w in enumerate(POOL_WIDTHS):
        lanes = slice(gi * POOL_GROUP, (gi + 1) * POOL_GROUP)
        u = u_ref[:, lanes]
        ext = jnp.concatenate([halo[:, lanes], u], axis=0)
        run = ext
        step = 1
        while step < w:
            run = run + pltpu.roll(run, step, axis=0)
            step *= 2
        cnt = jnp.minimum(pos + 1, w).astype(F32)
        y = run[POOL_HALO:, :] / cnt - u
        yo = _dot(y.astype(BF16), wp_ref[gi]) * sc_ref[:, lanes]
        mix_ref[:, NSA_WIDTH + gi * POOL_GROUP:NSA_WIDTH + (gi + 1) * POOL_GROUP] = yo.astype(BF16)
    x2 = x_ref[...] + _dot(mix_ref[...], wo_ref[...])
    y = _swiglu_residual(x2, g2_ref, wg_ref, wu_ref, wd_ref, h_ref, acc_ref)
    if final:
        y = _rms(y, fn_ref[...])
    y_ref[...] = y


def _mix_ffn(x3, o_nsa, u, wp, sc, wo, g2, wg, wu, wd, final_g=None, tm=512):
    b, t, d = x3.shape
    per = tm // POOL_HALO
    final = final_g is not None
    tail_specs = [_resident((1, d))] if final else []
    tail_args = [final_g] if final else []
    return pl.pallas_call(
        functools.partial(_mix_ffn_kernel, tm=tm, final=final),
        grid=(b, t // tm),
        in_specs=[pl.BlockSpec((None, tm, d), lambda i, j: (i, j, 0)),
                  pl.BlockSpec((None, tm, NSA_WIDTH), lambda i, j: (i, j, 0)),
                  pl.BlockSpec((None, tm, POOL_CH), lambda i, j: (i, j, 0)),
                  pl.BlockSpec((None, POOL_HALO, POOL_CH), lambda i, j: (i, jnp.maximum(j * per - 1, 0), 0)),
                  _resident(wp.shape), _resident(sc.shape), _resident(wo.shape),
                  _resident((1, d)), _resident(wg.shape), _resident(wu.shape), _resident(wd.shape)] + tail_specs,
        out_specs=pl.BlockSpec((None, tm, d), lambda i, j: (i, j, 0)),
        out_shape=jax.ShapeDtypeStruct((b, t, d), F32),
        scratch_shapes=[pltpu.VMEM((tm, NSA_WIDTH + POOL_CH), BF16),
                        pltpu.VMEM((tm, d), BF16), pltpu.VMEM((tm, d), F32)],
        compiler_params=pltpu.CompilerParams(
            dimension_semantics=("parallel", "parallel"), vmem_limit_bytes=VMEM_LIMIT),
        name="mix_ffn_final" if final else "mix_ffn",
    )(x3, o_nsa, u, u, wp, sc, wo, g2, wg, wu, wd, *tail_args)


def kernel(x, ffn1_norm, ffn1_wg, ffn1_wu, ffn1_wd, mix_norm, w_in, cmp_pe_k, cmp_wk1, cmp_wk2,
           cmp_pe_v, cmp_wv1, cmp_wv2, pool_w, pool_scale, w_out, ffn2_norm, ffn2_wg, ffn2_wu,
           ffn2_wd, final_norm):
    b, t, d = x.shape
    depth = ffn1_norm.shape[0]
    tables = _nsa_tables(t)
    f1g, f1u, f1d, f2g, f2u, f2d, wp, wo = (
        a.astype(BF16) for a in (ffn1_wg, ffn1_wu, ffn1_wd, ffn2_wg, ffn2_wu, ffn2_wd, pool_w, w_out))
    for l in range(depth):
        x, kcin, vcin, ks, kw, u, qt, vst, vwt, gt = _ffn_proj(
            x, ffn1_norm[l][None], f1g[l], f1u[l], f1d[l], mix_norm[l][None], *_split_w_in(w_in[l]))
        kc, vct = _compress(kcin, vcin,
                            *_compress_weights(cmp_pe_k[l], cmp_wk1[l], cmp_wk2[l], False),
                            *_compress_weights(cmp_pe_v[l], cmp_wv1[l], cmp_wv2[l], True))
        o_nsa = _nsa(qt, gt, ks, kw, vst, vwt, kc, vct, tables)
        x = _mix_ffn(x, o_nsa, u, wp[l], pool_scale[l][None], wo[l], ffn2_norm[l][None], f2g[l], f2u[l], f2d[l],
                     final_g=final_norm[None] if l == depth - 1 else None)
    return x
```

```python
import functools

import numpy as np
import jax
import jax.numpy as jnp
from jax import lax
from jax.experimental import pallas as pl
from jax.experimental.pallas import tpu as pltpu

HEAD_DIM = 64
NSA_HEADS = 8
KV_GROUPS = 2
HEADS_PER_GROUP = NSA_HEADS // KV_GROUPS
NSA_WIDTH = NSA_HEADS * HEAD_DIM
KV_WIDTH = KV_GROUPS * HEAD_DIM
CMP_BLOCK = 32
CMP_STRIDE = 16
CMP_HIDDEN = 128
SEL_BLOCK = 64
N_SEL = 8
WINDOW = 512
Q_BLOCK = 256
POOL_WIDTHS = (2, 4, 8, 16)
POOL_GROUP = 128
POOL_CH = POOL_GROUP * len(POOL_WIDTHS)
N_GATES = 3 * NSA_HEADS
GATE_ROWS = 32
EPS = 1e-6

LANES = 128
GQ = HEADS_PER_GROUP * Q_BLOCK
SEL_CHUNK = 256
PREV_KEYS = 512
FAR_CHUNK = 512
SLAB_HEADS = 2
FIRST_KEYS = 128
WIN_KEYS = WINDOW + Q_BLOCK
ONES_ROWS = 16
POOL_HALO = 16
FF_CHUNK = 256
LOG2E = 1.4426950408889634
NEG = -1e30
VMEM_LIMIT = 56 * 1024 * 1024

F32 = jnp.float32
BF16 = jnp.bfloat16


def _slopes():
    n = NSA_HEADS
    return (2.0 ** (-8.0 * np.arange(1, n + 1) / n)).astype(np.float32)


def _dot(a, b):
    return jnp.dot(a, b, preferred_element_type=F32)


def _dot_nt(a, b):
    return lax.dot_general(a, b, (((1,), (1,)), ((), ())), preferred_element_type=F32)


def _rms(x, g):
    return x * lax.rsqrt(jnp.mean(x * x, axis=-1, keepdims=True) + EPS) * g


def _resident(shape, layer=None):
    nd = len(shape)
    if layer is None:
        return pl.BlockSpec(shape, lambda *_: (0,) * nd, pipeline_mode=pl.Buffered(1))
    return pl.BlockSpec((None,) + tuple(shape), lambda *_: (layer,) + (0,) * nd, pipeline_mode=pl.Buffered(1))


def _swiglu_residual(x, g_ref, wg_ref, wu_ref, wd_ref, h_ref, acc_ref):
    h_ref[...] = _rms(x, g_ref[...]).astype(BF16)
    for c in range(wg_ref.shape[1] // FF_CHUNK):
        cols = slice(c * FF_CHUNK, (c + 1) * FF_CHUNK)
        h = h_ref[...]
        gate = _dot(h, wg_ref[:, cols])
        up = _dot(h, wu_ref[:, cols])
        a = (gate * jax.nn.sigmoid(gate) * up).astype(BF16)
        part = _dot(a, wd_ref[cols, :])
        if c == 0:
            acc_ref[...] = part
        else:
            acc_ref[...] += part
    return x + 0.5 * acc_ref[...]


_TOK_KC = (0, KV_WIDTH)
_TOK_VC = (KV_WIDTH, 2 * KV_WIDTH)
_TOK_KS = (2 * KV_WIDTH, 3 * KV_WIDTH)
_TOK_KW = (3 * KV_WIDTH, 4 * KV_WIDTH)
_TOK_U = (4 * KV_WIDTH, 4 * KV_WIDTH + POOL_CH)
_FEAT_Q = (0, NSA_WIDTH)
_FEAT_VS = (NSA_WIDTH, NSA_WIDTH + KV_WIDTH)
_FEAT_VW = (NSA_WIDTH + KV_WIDTH, NSA_WIDTH + 2 * KV_WIDTH)
_FEAT_GT = (NSA_WIDTH + 2 * KV_WIDTH, NSA_WIDTH + 2 * KV_WIDTH + GATE_ROWS)


def _ffn_proj_kernel(x_ref, g1_ref, wg_ref, wu_ref, wd_ref, g_ref, wt_ref, wf_ref,
                     x1_ref, kcin_ref, vcin_ref, ks_ref, kw_ref, u_ref,
                     qt_ref, vst_ref, vwt_ref, gt_ref, h_ref, acc_ref, *, tm):
    x1 = _swiglu_residual(x_ref[...], g1_ref, wg_ref, wu_ref, wd_ref, h_ref, acc_ref)
    x1_ref[...] = x1
    h = _rms(x1, g_ref[...]).astype(BF16)

    def tok(lo_hi):
        return _dot(h, wt_ref[:, lo_hi[0]:lo_hi[1]])

    kcin_ref[...] = tok(_TOK_KC)
    vcin_ref[...] = tok(_TOK_VC)
    ks_ref[...] = tok(_TOK_KS).astype(BF16)
    kw_ref[...] = tok(_TOK_KW).astype(BF16)
    u_ref[...] = tok(_TOK_U)
    zt = _dot_nt(wf_ref[...], h)
    qt_ref[...] = (zt[_FEAT_Q[0]:_FEAT_Q[1]] * (HEAD_DIM ** -0.5 * LOG2E)).astype(BF16)
    gt_ref[...] = zt[_FEAT_GT[0]:_FEAT_GT[1]]
    for c in range(tm // LANES):
        cols = slice(c * LANES, (c + 1) * LANES)
        vst_ref[c] = zt[_FEAT_VS[0]:_FEAT_VS[1], cols].astype(BF16)
        vwt_ref[c] = zt[_FEAT_VW[0]:_FEAT_VW[1], cols].astype(BF16)


def _ffn_proj(x3, layer, g1, wg, wu, wd, g, w_tok, w_feat, tm=512):
    b, t, d = x3.shape
    nck = tm // LANES
    tok = lambda wd: pl.BlockSpec((None, tm, wd), lambda i, j: (i, j, 0))
    chunked = pl.BlockSpec((None, nck, KV_WIDTH, LANES), lambda i, j: (i, j, 0, 0))
    return pl.pallas_call(
        functools.partial(_ffn_proj_kernel, tm=tm),
        grid=(b, t // tm),
        in_specs=[tok(d), _resident((1, d)),
                  _resident(wg.shape[1:], layer), _resident(wu.shape[1:], layer), _resident(wd.shape[1:], layer),
                  _resident((1, d)), _resident(w_tok.shape), _resident(w_feat.shape)],
        out_specs=[tok(d), tok(KV_WIDTH), tok(KV_WIDTH), tok(KV_WIDTH), tok(KV_WIDTH), tok(POOL_CH),
                   pl.BlockSpec((None, NSA_WIDTH, tm), lambda i, j: (i, 0, j)),
                   chunked, chunked,
                   pl.BlockSpec((None, GATE_ROWS, tm), lambda i, j: (i, 0, j))],
        out_shape=[jax.ShapeDtypeStruct((b, t, d), F32),
                   jax.ShapeDtypeStruct((b, t, KV_WIDTH), F32),
                   jax.ShapeDtypeStruct((b, t, KV_WIDTH), F32),
                   jax.ShapeDtypeStruct((b, t, KV_WIDTH), BF16),
                   jax.ShapeDtypeStruct((b, t, KV_WIDTH), BF16),
                   jax.ShapeDtypeStruct((b, t, POOL_CH), F32),
                   jax.ShapeDtypeStruct((b, NSA_WIDTH, t), BF16),
                   jax.ShapeDtypeStruct((b, t // LANES, KV_WIDTH, LANES), BF16),
                   jax.ShapeDtypeStruct((b, t // LANES, KV_WIDTH, LANES), BF16),
                   jax.ShapeDtypeStruct((b, GATE_ROWS, t), F32)],
        scratch_shapes=[pltpu.VMEM((tm, d), BF16), pltpu.VMEM((tm, d), F32)],
        compiler_params=pltpu.CompilerParams(
            dimension_semantics=("parallel", "parallel"), vmem_limit_bytes=VMEM_LIMIT),
        name="ffn_proj",
    )(x3, g1, wg, wu, wd, g, w_tok, w_feat)


def _split_w_in(w):
    q0, kv0 = 0, NSA_WIDTH
    col = lambda i: slice(kv0 + i * KV_WIDTH, kv0 + (i + 1) * KV_WIDTH)
    g0 = kv0 + 6 * KV_WIDTH
    w_tok = jnp.concatenate([w[:, col(0)], w[:, col(1)], w[:, col(2)], w[:, col(4)],
                             w[:, g0 + N_GATES:]], axis=1).astype(BF16)
    pad = jnp.zeros((w.shape[0], GATE_ROWS - N_GATES), w.dtype)
    w_feat = jnp.concatenate([w[:, q0:NSA_WIDTH], w[:, col(3)], w[:, col(5)],
                              w[:, g0:g0 + N_GATES], pad], axis=1).T.astype(BF16)
    return w_tok, w_feat


def _gelu_tanh(x):
    c = np.float32(np.sqrt(2.0 / np.pi))
    return 0.5 * x * (1.0 + jnp.tanh(c * (x + 0.044715 * (x * x * x))))


def _compress_kernel(kcin_ref, vcin_ref, pek_ref, wk1_ref, wk2_ref, pev_ref, wv1_ref, wv2_ref,
                     kc_ref, vct_ref, *, n_rows):
    nl = CMP_STRIDE
    hid2 = KV_GROUPS * CMP_HIDDEN

    def hidden(src_ref, pe_ref, w1_ref):
        acc = jnp.zeros((n_rows + 16, 2 * hid2), F32)
        for l in range(nl):
            xl = src_ref[pl.ds(l, n_rows, stride=nl), :]
            pa = jnp.broadcast_to(pe_ref[l:l + 1, :], (8, KV_WIDTH))
            pb = jnp.broadcast_to(pe_ref[nl + l:nl + l + 1, :], (8, KV_WIDTH))
            lhs = jnp.concatenate([xl, pa, pb], axis=0).astype(BF16)
            acc = acc + _dot(lhs, w1_ref[l])
        sa = acc[0:n_rows, 0:hid2]
        sb = acc[0:n_rows, hid2:2 * hid2]
        bias = acc[n_rows:n_rows + 1, 0:hid2] + acc[n_rows + 8:n_rows + 9, hid2:2 * hid2]
        hcur = sa + pltpu.roll(sb, n_rows - 1, axis=0) + bias
        return _gelu_tanh(hcur).astype(BF16)

    hk = hidden(kcin_ref, pek_ref, wk1_ref)
    kc = _dot(hk, wk2_ref[...])
    for g in range(KV_GROUPS):
        kc_ref[g] = kc[:, g * LANES:(g + 1) * LANES].astype(BF16)
    hv = hidden(vcin_ref, pev_ref, wv1_ref)
    for g in range(KV_GROUPS):
        vct_ref[g] = _dot_nt(wv2_ref[g], hv).astype(BF16)


def _compress(kcin, vcin, pek2, wk1b, wk2b, pev2, wv1b, wv2t):
    b, t, _ = kcin.shape
    n_rows = t // CMP_STRIDE
    seq_spec = pl.BlockSpec((None, t, KV_WIDTH), lambda i: (i, 0, 0))
    return pl.pallas_call(
        functools.partial(_compress_kernel, n_rows=n_rows),
        grid=(b,),
        in_specs=[seq_spec, seq_spec,
                  _resident(pek2.shape), _resident(wk1b.shape), _resident(wk2b.shape),
                  _resident(pev2.shape), _resident(wv1b.shape), _resident(wv2t.shape)],
        out_specs=[pl.BlockSpec((None, KV_GROUPS, n_rows, LANES), lambda i: (i, 0, 0, 0)),
                   pl.BlockSpec((None, KV_GROUPS, HEAD_DIM, n_rows), lambda i: (i, 0, 0, 0))],
        out_shape=[jax.ShapeDtypeStruct((b, KV_GROUPS, n_rows, LANES), BF16),
                   jax.ShapeDtypeStruct((b, KV_GROUPS, HEAD_DIM, n_rows), BF16)],
        compiler_params=pltpu.CompilerParams(
            dimension_semantics=("parallel",), vmem_limit_bytes=VMEM_LIMIT),
        name="compress",
    )(kcin, vcin, pek2, wk1b, wk2b, pev2, wv1b, wv2t)


def _compress_weights(pe, w1, w2, value_layout):
    nl, dh, hid = CMP_STRIDE, HEAD_DIM, CMP_HIDDEN
    w1r = w1.reshape(2, nl, dh, hid)
    z = jnp.zeros((nl, dh, hid), w1.dtype)
    row_g0 = jnp.concatenate([w1r[0], z, w1r[1], z], axis=-1)
    row_g1 = jnp.concatenate([z, w1r[0], z, w1r[1]], axis=-1)
    w1b = jnp.concatenate([row_g0, row_g1], axis=1).astype(BF16)
    pe2 = jnp.concatenate([pe, pe], axis=-1)
    zz = jnp.zeros((hid, dh), w2.dtype)
    if value_layout:
        w2b = jnp.stack([jnp.concatenate([w2, zz], axis=0).T,
                         jnp.concatenate([zz, w2], axis=0).T]).astype(BF16)
    else:
        top = jnp.concatenate([w2, zz, zz, zz], axis=-1)
        bot = jnp.concatenate([zz, zz, w2, zz], axis=-1)
        w2b = jnp.concatenate([top, bot], axis=0).astype(BF16)
    return pe2, w1b, w2b


def _nsa_tables(t):
    slopes = _slopes().reshape(KV_GROUPS, HEADS_PER_GROUP)
    n_rows = t // CMP_STRIDE
    n_blk = t // SEL_BLOCK
    i = np.arange(Q_BLOCK)
    c = np.arange(n_rows)
    d0c = (i[None, :] - (c[:, None] * CMP_STRIDE + CMP_BLOCK - 1)).astype(np.int32)
    tblc = -LOG2E * slopes[:, None, :, None].astype(np.float64) * d0c[None, :, None, :]
    tblc = tblc.astype(np.float32)
    tblc = tblc.reshape(KV_GROUPS, n_rows, GQ)
    d0c = np.tile(d0c, (1, HEADS_PER_GROUP))
    j = np.arange(WIN_KEYS)
    dist = i[None, :] + WINDOW - j[:, None]
    ok = (dist >= 0) & (dist < WINDOW)
    tblw = np.where(ok[None, :, None, :],
                    -LOG2E * slopes[:, None, :, None].astype(np.float64) * dist[None, :, None, :],
                    NEG).astype(np.float32).reshape(KV_GROUPS, WIN_KEYS, GQ)
    jj = np.arange(SEL_CHUNK)
    per = SEL_CHUNK // Q_BLOCK
    tailm = np.stack([np.where(jj[:, None] > i[None, :] + Q_BLOCK * r, np.float32(NEG), np.float32(0))
                      for r in range(per)]).astype(np.float32)
    ci = c[None, :] * CMP_STRIDE
    sj = np.arange(n_blk)[:, None] * SEL_BLOCK
    n_cmp = (t - CMP_BLOCK) // CMP_STRIDE + 1
    ov = ((ci <= sj + SEL_BLOCK - 1) & (ci + CMP_BLOCK - 1 >= sj) & (c[None, :] < n_cmp))
    ov = ov.astype(np.float32)
    pos = np.arange(t)
    ktag = np.zeros((t, 2 * LANES), np.float32)
    ktag[pos, HEAD_DIM + pos // SEL_BLOCK] = 1.0
    ktag[pos, LANES + pos // SEL_BLOCK] = 1.0
    ktag[:, LANES + HEAD_DIM] = pos % SEL_BLOCK
    ktag[:, LANES + HEAD_DIM + 1] = pos % SEL_BLOCK
    qflag = np.zeros((HEAD_DIM, GQ), np.float32)
    qflag[0, :] = NEG
    s2 = np.repeat(slopes, Q_BLOCK, axis=1).astype(np.float64) * LOG2E
    s2_hi = s2.astype(BF16).astype(np.float64)
    qslope = np.zeros((KV_GROUPS, HEAD_DIM, GQ), np.float32)
    qslope[:, 0, :] = s2_hi
    qslope[:, 1, :] = s2 - s2_hi
    return (jnp.asarray(tblc), jnp.asarray(d0c), jnp.asarray(tblw), jnp.asarray(tailm),
            jnp.asarray(ov, dtype=BF16), jnp.asarray(ktag, dtype=BF16),
            jnp.asarray(qflag, dtype=BF16), jnp.asarray(qslope, dtype=BF16))


def _nsa_kernel(qt_ref, gt_ref, ks_ref, kw_ref, vst_ref, vwt_ref, kc_ref, vct_ref,
                tblc_ref, d0c_ref, tblw_ref, tailm_ref, ov_ref, ktag_ref, qflag_ref, qslope_ref,
                o_ref,
                kaug_ref, kwaug_ref, m_ref, acc_ref, need_ref, *, t, n_blk):
    qi = pl.program_id(1)
    slopes = _slopes()
    hg = HEADS_PER_GROUP
    prep_rows = WINDOW

    @pl.when(qi == 0)
    def _prep():
        low = lax.broadcasted_iota(jnp.int32, (prep_rows, LANES), 1) < HEAD_DIM
        flag = jnp.where(lax.broadcasted_iota(jnp.int32, (WINDOW, LANES), 1) == HEAD_DIM, 1.0, 0.0)
        for g in range(KV_GROUPS):
            kwaug_ref[g, 0:WINDOW, :] = flag.astype(BF16)

        def body(c, carry):
            r0 = pl.multiple_of(c * prep_rows, prep_rows)
            rows = pl.ds(r0, prep_rows)
            rows_w = pl.ds(r0 + WINDOW, prep_rows)
            tag = ktag_ref[rows, 0:LANES].astype(F32)
            ks = ks_ref[rows, :].astype(F32)
            kw = kw_ref[rows, :].astype(F32)
            ks_sw = pltpu.roll(ks, HEAD_DIM, axis=1)
            kw_sw = pltpu.roll(kw, HEAD_DIM, axis=1)
            for g in range(KV_GROUPS):
                kaug_ref[g, rows, 0:LANES] = jnp.where(low, ks if g == 0 else ks_sw, tag).astype(BF16)
                kaug_ref[g, rows, LANES:2 * LANES] = ktag_ref[rows, LANES:2 * LANES]
                kwaug_ref[g, rows_w, :] = jnp.where(low, kw if g == 0 else kw_sw, 0.0).astype(BF16)
            return carry

        lax.fori_loop(0, t // prep_rows, body, 0)

    gates = jax.nn.sigmoid(gt_ref[...])
    bi = lax.broadcasted_iota(jnp.int32, (n_blk, Q_BLOCK), 0)
    lq = lax.broadcasted_iota(jnp.int32, (n_blk, Q_BLOCK), 1)
    cur = (Q_BLOCK // SEL_BLOCK) * qi + lax.shift_right_logical(lq, SEL_BLOCK.bit_length() - 1)
    valid = bi <= cur
    near = (bi == 0) | (bi >= cur - 1)
    always = valid & (near | (cur < N_SEL))
    free = valid & jnp.logical_not(near)
    far = (bi >= FIRST_KEYS // SEL_BLOCK) & (bi < (qi * Q_BLOCK - PREV_KEYS) // SEL_BLOCK)
    rel_blk = (bi - cur).astype(F32)

    head_slabs = [slice(h * Q_BLOCK, (h + SLAB_HEADS) * Q_BLOCK) for h in range(0, hg, SLAB_HEADS)]

    def with_ones(vt):
        first = lax.broadcasted_iota(jnp.int32, (ONES_ROWS, vt.shape[1]), 0) == 0
        return jnp.concatenate([vt, jnp.where(first, 1.0, 0.0).astype(BF16)], axis=0)

    def queries(g):
        heads = lambda ref: jnp.concatenate([ref[(g * hg + h) * HEAD_DIM:(g * hg + h + 1) * HEAD_DIM, :]
                                             for h in range(hg)], axis=1)
        qt = heads(qt_ref)
        return qt, jnp.concatenate([qt, qflag_ref[...]], axis=0)

    def select(g, qt, s):
        o_parts, psum = [], None
        for lanes in head_slabs:
            sh = s[:, lanes] + tblc_ref[g, :, lanes]
            sh = jnp.where(d0c_ref[:, lanes] + Q_BLOCK * qi >= 0, sh, NEG)
            m = jnp.max(sh, axis=0, keepdims=True)
            m = jnp.where(m < 0.5 * NEG, 0.0, m)
            p = jnp.exp2(sh - m)
            l = jnp.sum(p, axis=0, keepdims=True)
            pn = p * (1.0 / jnp.maximum(l, 1e-30))
            o_parts.append(_dot(vct_ref[g], pn.astype(BF16)))
            for h in range(SLAB_HEADS):
                ph = pn[:, h * Q_BLOCK:(h + 1) * Q_BLOCK]
                psum = ph if psum is None else psum + ph
        o_cmp = jnp.concatenate(o_parts, axis=1)

        p_hi = psum.astype(BF16)
        p_lo = (psum - p_hi.astype(F32)).astype(BF16)
        imp = _dot(ov_ref[...], p_hi) + _dot(ov_ref[...], p_lo)
        val = jnp.where(free, imp, -1.0)
        sel = jnp.zeros((n_blk, Q_BLOCK), F32)
        for _ in range(N_SEL - 3):
            top = jnp.max(val, axis=0, keepdims=True)
            idx = jnp.min(jnp.where(val == top, bi, n_blk), axis=0, keepdims=True)
            pick = bi == idx
            sel = jnp.where(pick, 1.0, sel)
            val = jnp.where(pick, -2.0, val)
        chosen = jnp.where(always, 1.0, jnp.where(free, sel, 0.0)) > 0.0

        def tagged(blocks):
            bias = jnp.concatenate(
                [(float(slopes[g * hg + h]) * SEL_BLOCK * LOG2E) * rel_blk for h in range(hg)], axis=1)
            on = jnp.concatenate([blocks] * hg, axis=1)
            hi = bias.astype(BF16)
            lo = (bias - hi.astype(F32)).astype(BF16)
            pad = [jnp.zeros((HEAD_DIM - n_blk, GQ), BF16)] if n_blk < HEAD_DIM else []
            parts = [qt, jnp.where(on, hi, NEG)] + pad + [jnp.where(on, lo, 0.0)] + pad + [qslope_ref[g]]
            return jnp.concatenate(parts, axis=0)

        return o_cmp, tagged(chosen), tagged(chosen & far), jnp.where(chosen & far, 1.0, 0.0)

    def sel_keys(g, pieces):
        return jnp.concatenate([kaug_ref[g, pl.ds(pl.multiple_of(s0 * LANES, LANES), n * LANES), :]
                                for s0, n in pieces], axis=0)

    def sel_values(g, pieces):
        return with_ones(jnp.concatenate([vst_ref[s0 + k, g * HEAD_DIM:(g + 1) * HEAD_DIM, :]
                                          for s0, n in pieces for k in range(n)], axis=1))

    def sel_update(s, vt, carry):
        m, acc = carry
        ms, accs = [], []
        for lanes in head_slabs:
            sh, mh = s[:, lanes], m[:, lanes]
            m_new = jnp.maximum(mh, jnp.max(sh, axis=0, keepdims=True))
            alpha = jnp.exp2(mh - m_new)
            p = jnp.exp2(sh - m_new).astype(BF16)
            ms.append(m_new)
            accs.append(acc[:, lanes] * alpha + _dot(vt, p))
        return jnp.concatenate(ms, axis=1), jnp.concatenate(accs, axis=1)

    def sel_steps(pieces, q, mask, carries):
        scores = [_dot(sel_keys(g, pieces), q[g]) for g in groups]
        if mask is not None:
            scores = [jnp.concatenate([sc[0:mask.shape[0]] + mask, sc[mask.shape[0]:]], axis=0) for sc in scores]
        return tuple(sel_update(scores[g], sel_values(g, pieces), carries[g]) for g in groups)

    def window_scores(g, qw):
        rows = pl.ds(pl.multiple_of(qi * Q_BLOCK, Q_BLOCK), WIN_KEYS)
        return _dot(kwaug_ref[g, rows, :], qw)

    def window(g, s):
        back = WINDOW // LANES
        ahead = Q_BLOCK // LANES
        vt = jnp.concatenate([vwt_ref[jnp.maximum(qi * ahead - back + k, 0), g * HEAD_DIM:(g + 1) * HEAD_DIM, :]
                              for k in range(back + ahead)], axis=1)
        vt = with_ones(vt)
        outs = []
        for lanes in head_slabs:
            sh = s[:, lanes] + tblw_ref[g, :, lanes]
            m = jnp.max(sh, axis=0, keepdims=True)
            p = jnp.exp2(sh - m).astype(BF16)
            acc = _dot(vt, p)
            outs.append(acc[0:HEAD_DIM] * (1.0 / acc[HEAD_DIM:HEAD_DIM + 1]))
        return jnp.concatenate(outs, axis=1)

    groups = tuple(range(KV_GROUPS))
    qt, qw = zip(*[queries(g) for g in groups])
    s_cmp = [_dot(kc_ref[g], qw[g]) for g in groups]
    s_win = [window_scores(g, qw[g]) for g in groups]
    sel_out, o_win = [], []
    for g in groups:
        sel_out.append(select(g, qt[g], s_cmp[g]))
        o_win.append(window(g, s_win[g]))
    o_cmp, qaug, qfar, picked = zip(*sel_out)

    nq, n_first = Q_BLOCK // LANES, FIRST_KEYS // LANES
    future = (qi + 1) * nq
    diag_pieces = [(qi * nq, nq)]
    for back in range(1, PREV_KEYS // Q_BLOCK + 1):
        diag_pieces.append((jnp.where(qi >= back, (qi - back) * nq, future), nq))
    diag_pieces.append((jnp.where(qi * Q_BLOCK >= PREV_KEYS + FIRST_KEYS, 0, future), n_first))
    causal = jnp.concatenate([tailm_ref[0]] * hg, axis=1)
    init = (jnp.full((1, GQ), NEG, F32), jnp.zeros((HEAD_DIM + ONES_ROWS, GQ), F32))

    def load_carries():
        return tuple((m_ref[g, 0:1, :], acc_ref[g]) for g in groups)

    def store_carries(carries):
        for g, (m, acc) in enumerate(carries):
            m_ref[g, 0:1, :] = m
            acc_ref[g] = acc

    store_carries(sel_steps(diag_pieces, qaug, causal, tuple(init for _ in groups)))

    blk_per_chunk = FAR_CHUNK // SEL_BLOCK
    wanted = picked[0]
    for g in groups[1:]:
        wanted = wanted + picked[g]
    for c in range(n_blk // blk_per_chunk):
        hit = jnp.sum(wanted[c * blk_per_chunk:(c + 1) * blk_per_chunk, :])
        need_ref[c] = (hit > 0.0).astype(jnp.int32)

    def far_chunk(c, carry):
        @pl.when(need_ref[c] > 0)
        def _():
            store_carries(sel_steps([(c * (FAR_CHUNK // LANES), FAR_CHUNK // LANES)], qfar, None, load_carries()))
        return carry

    far_keys = jnp.maximum(qi * Q_BLOCK - PREV_KEYS, 0)
    lax.fori_loop(0, (far_keys + FAR_CHUNK - 1) // FAR_CHUNK, far_chunk, 0)
    o_slc = [acc[0:HEAD_DIM] * (1.0 / acc[HEAD_DIM:HEAD_DIM + 1]) for _, acc in load_carries()]

    for g in groups:
        for pr in range(hg // 2):
            outs = []
            for h in (2 * pr, 2 * pr + 1):
                hh = g * hg + h
                lanes = slice(h * Q_BLOCK, (h + 1) * Q_BLOCK)
                out = gates[3 * hh:3 * hh + 1, :] * o_cmp[g][:, lanes]
                out = out + gates[3 * hh + 1:3 * hh + 2, :] * o_slc[g][:, lanes]
                out = out + gates[3 * hh + 2:3 * hh + 3, :] * o_win[g][:, lanes]
                outs.append(out)
            pair = jnp.concatenate(outs, axis=0)
            o_ref[:, (g * 2 + pr) * LANES:(g * 2 + pr + 1) * LANES] = pair.T.astype(BF16)


def _nsa(qt, gt, ks, kw, vst, vwt, kc, vct, tables):
    b, _, t = qt.shape
    n_blk = t // SEL_BLOCK
    n_rows = t // CMP_STRIDE
    assert SEL_CHUNK == Q_BLOCK and PREV_KEYS % Q_BLOCK == 0 and t % FAR_CHUNK == 0
    assert t >= PREV_KEYS + 2 * Q_BLOCK
    assert N_SEL <= n_blk <= HEAD_DIM and n_blk % 16 == 0
    seq_spec = pl.BlockSpec((None, t, LANES), lambda i, j: (i, 0, 0))
    chunked = pl.BlockSpec((None, t // LANES, KV_WIDTH, LANES), lambda i, j: (i, 0, 0, 0))
    return pl.pallas_call(
        functools.partial(_nsa_kernel, t=t, n_blk=n_blk),
        grid=(b, t // Q_BLOCK),
        in_specs=[pl.BlockSpec((None, NSA_WIDTH, Q_BLOCK), lambda i, j: (i, 0, j)),
                  pl.BlockSpec((None, GATE_ROWS, Q_BLOCK), lambda i, j: (i, 0, j)),
                  seq_spec, seq_spec, chunked, chunked,
                  pl.BlockSpec((None, KV_GROUPS, n_rows, LANES), lambda i, j: (i, 0, 0, 0)),
                  pl.BlockSpec((None, KV_GROUPS, HEAD_DIM, n_rows), lambda i, j: (i, 0, 0, 0))]
                 + [_resident(a.shape) for a in tables],
        out_specs=pl.BlockSpec((None, Q_BLOCK, NSA_WIDTH), lambda i, j: (i, j, 0)),
        out_shape=jax.ShapeDtypeStruct((b, t, NSA_WIDTH), BF16),
        scratch_shapes=[pltpu.VMEM((KV_GROUPS, t, 2 * LANES), BF16),
                        pltpu.VMEM((KV_GROUPS, t + WINDOW, LANES), BF16),
                        pltpu.VMEM((KV_GROUPS, 8, GQ), F32),
                        pltpu.VMEM((KV_GROUPS, HEAD_DIM + ONES_ROWS, GQ), F32),
                        pltpu.SMEM((t // FAR_CHUNK,), jnp.int32)],
        compiler_params=pltpu.CompilerParams(
            dimension_semantics=("arbitrary", "arbitrary"), vmem_limit_bytes=VMEM_LIMIT),
        name="nsa",
    )(qt, gt, ks, kw, vst, vwt, kc, vct, *tables)


def _mix_ffn_kernel(x_ref, o_ref, u_ref, halo_ref, wp_ref, sc_ref, wo_ref, g2_ref, wg_ref, wu_ref, wd_ref,
                    *rest, tm, final):
    if final:
        fn_ref, y_ref, mix_ref, h_ref, acc_ref = rest
    else:
        y_ref, mix_ref, h_ref, acc_ref = rest
    ti = pl.program_id(1)
    mix_ref[:, 0:NSA_WIDTH] = o_ref[...]
    halo = jnp.where(ti > 0, halo_ref[...], 0.0)
    pos = ti * tm + lax.broadcasted_iota(jnp.int32, (tm, 1), 0)
    for gi, w in enumerate(POOL_WIDTHS):
        lanes = slice(gi * POOL_GROUP, (gi + 1) * POOL_GROUP)
        u = u_ref[:, lanes]
        ext = jnp.concatenate([halo[:, lanes], u], axis=0)
        run = ext
        step = 1
        while step < w:
            run = run + pltpu.roll(run, step, axis=0)
            step *= 2
        cnt = jnp.minimum(pos + 1, w).astype(F32)
        y = run[POOL_HALO:, :] / cnt - u
        yo = _dot(y.astype(BF16), wp_ref[gi]) * sc_ref[:, lanes]
        mix_ref[:, NSA_WIDTH + gi * POOL_GROUP:NSA_WIDTH + (gi + 1) * POOL_GROUP] = yo.astype(BF16)
    x2 = x_ref[...] + _dot(mix_ref[...], wo_ref[...])
    y = _swiglu_residual(x2, g2_ref, wg_ref, wu_ref, wd_ref, h_ref, acc_ref)
    if final:
        y = _rms(y, fn_ref[...])
    y_ref[...] = y


def _mix_ffn(x3, layer, o_nsa, u, wp, sc, wo, g2, wg, wu, wd, final_g=None, tm=512):
    b, t, d = x3.shape
    per = tm // POOL_HALO
    final = final_g is not None
    tail_specs = [_resident((1, d))] if final else []
    tail_args = [final_g] if final else []
    return pl.pallas_call(
        functools.partial(_mix_ffn_kernel, tm=tm, final=final),
        grid=(b, t // tm),
        in_specs=[pl.BlockSpec((None, tm, d), lambda i, j: (i, j, 0)),
                  pl.BlockSpec((None, tm, NSA_WIDTH), lambda i, j: (i, j, 0)),
                  pl.BlockSpec((None, tm, POOL_CH), lambda i, j: (i, j, 0)),
                  pl.BlockSpec((None, POOL_HALO, POOL_CH), lambda i, j: (i, jnp.maximum(j * per - 1, 0), 0)),
                  _resident(wp.shape[1:], layer), _resident(sc.shape), _resident(wo.shape[1:], layer),
                  _resident((1, d)), _resident(wg.shape[1:], layer), _resident(wu.shape[1:], layer),
                  _resident(wd.shape[1:], layer)] + tail_specs,
        out_specs=pl.BlockSpec((None, tm, d), lambda i, j: (i, j, 0)),
        out_shape=jax.ShapeDtypeStruct((b, t, d), F32),
        scratch_shapes=[pltpu.VMEM((tm, NSA_WIDTH + POOL_CH), BF16),
                        pltpu.VMEM((tm, d), BF16), pltpu.VMEM((tm, d), F32)],
        compiler_params=pltpu.CompilerParams(
            dimension_semantics=("parallel", "parallel"), vmem_limit_bytes=VMEM_LIMIT),
        name="mix_ffn_final" if final else "mix_ffn",
    )(x3, o_nsa, u, u, wp, sc, wo, g2, wg, wu, wd, *tail_args)


def kernel(x, ffn1_norm, ffn1_wg, ffn1_wu, ffn1_wd, mix_norm, w_in, cmp_pe_k, cmp_wk1, cmp_wk2,
           cmp_pe_v, cmp_wv1, cmp_wv2, pool_w, pool_scale, w_out, ffn2_norm, ffn2_wg, ffn2_wu,
           ffn2_wd, final_norm):
    b, t, d = x.shape
    depth = ffn1_norm.shape[0]
    tables = _nsa_tables(t)
    f1g, f1u, f1d, f2g, f2u, f2d, wp, wo = (
        a.astype(BF16) for a in (ffn1_wg, ffn1_wu, ffn1_wd, ffn2_wg, ffn2_wu, ffn2_wd, pool_w, w_out))
    for l in range(depth):
        x, kcin, vcin, ks, kw, u, qt, vst, vwt, gt = _ffn_proj(
            x, l, ffn1_norm[l][None], f1g, f1u, f1d, mix_norm[l][None], *_split_w_in(w_in[l]))
        kc, vct = _compress(kcin, vcin,
                            *_compress_weights(cmp_pe_k[l], cmp_wk1[l], cmp_wk2[l], False),
                            *_compress_weights(cmp_pe_v[l], cmp_wv1[l], cmp_wv2[l], True))
        o_nsa = _nsa(qt, gt, ks, kw, vst, vwt, kc, vct, tables)
        x = _mix_ffn(x, l, o_nsa, u, wp, pool_scale[l][None], wo, ffn2_norm[l][None], f2g, f2u, f2d,
                     final_g=final_norm[None] if l == depth - 1 else None)
    return x
```

```python
import functools

import numpy as np
import jax
import jax.numpy as jnp
from jax import lax
from jax.experimental import pallas as pl
from jax.experimental.pallas import tpu as pltpu

HEAD_DIM = 64
NSA_HEADS = 8
KV_GROUPS = 2
HEADS_PER_GROUP = NSA_HEADS // KV_GROUPS
NSA_WIDTH = NSA_HEADS * HEAD_DIM
KV_WIDTH = KV_GROUPS * HEAD_DIM
CMP_BLOCK = 32
CMP_STRIDE = 16
CMP_HIDDEN = 128
SEL_BLOCK = 64
N_SEL = 8
WINDOW = 512
Q_BLOCK = 256
POOL_WIDTHS = (2, 4, 8, 16)
POOL_GROUP = 128
POOL_CH = POOL_GROUP * len(POOL_WIDTHS)
N_GATES = 3 * NSA_HEADS
GATE_ROWS = 32
EPS = 1e-6

LANES = 128
GQ = HEADS_PER_GROUP * Q_BLOCK
SEL_CHUNK = 256
PREV_KEYS = 512
FAR_CHUNK = 512
SLAB_HEADS = 2
FIRST_KEYS = 128
WIN_KEYS = WINDOW + Q_BLOCK
ONES_ROWS = 16
POOL_HALO = 16
FF_CHUNK = 256
LOG2E = 1.4426950408889634
NEG = -1e30
VMEM_LIMIT = 56 * 1024 * 1024

F32 = jnp.float32
BF16 = jnp.bfloat16


def _slopes():
    n = NSA_HEADS
    return (2.0 ** (-8.0 * np.arange(1, n + 1) / n)).astype(np.float32)


def _dot(a, b):
    return jnp.dot(a, b, preferred_element_type=F32)


def _dot_nt(a, b):
    return lax.dot_general(a, b, (((1,), (1,)), ((), ())), preferred_element_type=F32)


def _rms(x, g):
    return x * lax.rsqrt(jnp.mean(x * x, axis=-1, keepdims=True) + EPS) * g


def _resident(shape, layer=None):
    nd = len(shape)
    if layer is None:
        return pl.BlockSpec(shape, lambda *_: (0,) * nd, pipeline_mode=pl.Buffered(1))
    return pl.BlockSpec((None,) + tuple(shape), lambda *_: (layer,) + (0,) * nd, pipeline_mode=pl.Buffered(1))


def _swiglu_residual(x, g_ref, wg_ref, wu_ref, wd_ref, h_ref, acc_ref):
    h_ref[...] = _rms(x, g_ref[...]).astype(BF16)
    for c in range(wg_ref.shape[1] // FF_CHUNK):
        cols = slice(c * FF_CHUNK, (c + 1) * FF_CHUNK)
        h = h_ref[...]
        gate = _dot(h, wg_ref[:, cols])
        up = _dot(h, wu_ref[:, cols])
        a = (gate * jax.nn.sigmoid(gate) * up).astype(BF16)
        part = _dot(a, wd_ref[cols, :])
        if c == 0:
            acc_ref[...] = part
        else:
            acc_ref[...] += part
    return x + 0.5 * acc_ref[...]


_TOK_KC = (0, KV_WIDTH)
_TOK_VC = (KV_WIDTH, 2 * KV_WIDTH)
_TOK_KS = (2 * KV_WIDTH, 3 * KV_WIDTH)
_TOK_KW = (3 * KV_WIDTH, 4 * KV_WIDTH)
_TOK_U = (4 * KV_WIDTH, 4 * KV_WIDTH + POOL_CH)
_FEAT_Q = (0, NSA_WIDTH)
_FEAT_VS = (NSA_WIDTH, NSA_WIDTH + KV_WIDTH)
_FEAT_VW = (NSA_WIDTH + KV_WIDTH, NSA_WIDTH + 2 * KV_WIDTH)
_FEAT_GT = (NSA_WIDTH + 2 * KV_WIDTH, NSA_WIDTH + 2 * KV_WIDTH + GATE_ROWS)


def _ffn_proj_kernel(x_ref, g1_ref, wg_ref, wu_ref, wd_ref, g_ref, wt_ref, wf_ref,
                     x1_ref, kcin_ref, vcin_ref, ks_ref, kw_ref, u_ref,
                     qt_ref, vst_ref, vwt_ref, gt_ref, h_ref, acc_ref, *, tm):
    x1 = _swiglu_residual(x_ref[...], g1_ref, wg_ref, wu_ref, wd_ref, h_ref, acc_ref)
    x1_ref[...] = x1
    h = _rms(x1, g_ref[...]).astype(BF16)

    def tok(lo_hi):
        return _dot(h, wt_ref[:, lo_hi[0]:lo_hi[1]])

    kcin_ref[...] = tok(_TOK_KC)
    vcin_ref[...] = tok(_TOK_VC)
    ks_ref[...] = tok(_TOK_KS).astype(BF16)
    kw_ref[...] = tok(_TOK_KW).astype(BF16)
    u_ref[...] = tok(_TOK_U)
    zt = _dot_nt(wf_ref[...], h)
    qt_ref[...] = (zt[_FEAT_Q[0]:_FEAT_Q[1]] * (HEAD_DIM ** -0.5 * LOG2E)).astype(BF16)
    gt_ref[...] = zt[_FEAT_GT[0]:_FEAT_GT[1]]
    for c in range(tm // LANES):
        cols = slice(c * LANES, (c + 1) * LANES)
        vst_ref[c] = zt[_FEAT_VS[0]:_FEAT_VS[1], cols].astype(BF16)
        vwt_ref[c] = zt[_FEAT_VW[0]:_FEAT_VW[1], cols].astype(BF16)


def _ffn_proj(x3, layer, g1, wg, wu, wd, g, w_tok, w_feat, tm=512):
    b, t, d = x3.shape
    nck = tm // LANES
    tok = lambda wd: pl.BlockSpec((None, tm, wd), lambda i, j: (i, j, 0))
    chunked = pl.BlockSpec((None, nck, KV_WIDTH, LANES), lambda i, j: (i, j, 0, 0))
    return pl.pallas_call(
        functools.partial(_ffn_proj_kernel, tm=tm),
        grid=(b, t // tm),
        in_specs=[tok(d), _resident((1, d)),
                  _resident(wg.shape[1:], layer), _resident(wu.shape[1:], layer), _resident(wd.shape[1:], layer),
                  _resident((1, d)), _resident(w_tok.shape), _resident(w_feat.shape)],
        out_specs=[tok(d), tok(KV_WIDTH), tok(KV_WIDTH), tok(KV_WIDTH), tok(KV_WIDTH), tok(POOL_CH),
                   pl.BlockSpec((None, NSA_WIDTH, tm), lambda i, j: (i, 0, j)),
                   chunked, chunked,
                   pl.BlockSpec((None, GATE_ROWS, tm), lambda i, j: (i, 0, j))],
        out_shape=[jax.ShapeDtypeStruct((b, t, d), F32),
                   jax.ShapeDtypeStruct((b, t, KV_WIDTH), F32),
                   jax.ShapeDtypeStruct((b, t, KV_WIDTH), F32),
                   jax.ShapeDtypeStruct((b, t, KV_WIDTH), BF16),
                   jax.ShapeDtypeStruct((b, t, KV_WIDTH), BF16),
                   jax.ShapeDtypeStruct((b, t, POOL_CH), F32),
                   jax.ShapeDtypeStruct((b, NSA_WIDTH, t), BF16),
                   jax.ShapeDtypeStruct((b, t // LANES, KV_WIDTH, LANES), BF16),
                   jax.ShapeDtypeStruct((b, t // LANES, KV_WIDTH, LANES), BF16),
                   jax.ShapeDtypeStruct((b, GATE_ROWS, t), F32)],
        scratch_shapes=[pltpu.VMEM((tm, d), BF16), pltpu.VMEM((tm, d), F32)],
        compiler_params=pltpu.CompilerParams(
            dimension_semantics=("parallel", "parallel"), vmem_limit_bytes=VMEM_LIMIT),
        name="ffn_proj",
    )(x3, g1, wg, wu, wd, g, w_tok, w_feat)


def _split_w_in(w):
    q0, kv0 = 0, NSA_WIDTH
    col = lambda i: slice(kv0 + i * KV_WIDTH, kv0 + (i + 1) * KV_WIDTH)
    g0 = kv0 + 6 * KV_WIDTH
    w_tok = jnp.concatenate([w[:, col(0)], w[:, col(1)], w[:, col(2)], w[:, col(4)],
                             w[:, g0 + N_GATES:]], axis=1).astype(BF16)
    pad = jnp.zeros((w.shape[0], GATE_ROWS - N_GATES), w.dtype)
    w_feat = jnp.concatenate([w[:, q0:NSA_WIDTH], w[:, col(3)], w[:, col(5)],
                              w[:, g0:g0 + N_GATES], pad], axis=1).T.astype(BF16)
    return w_tok, w_feat


def _gelu_tanh(x):
    c = np.float32(np.sqrt(2.0 / np.pi))
    return 0.5 * x * (1.0 + jnp.tanh(c * (x + 0.044715 * (x * x * x))))


def _compress_kernel(kcin_ref, vcin_ref, pek_ref, wk1_ref, wk2_ref, pev_ref, wv1_ref, wv2_ref,
                     kc_ref, vct_ref, *, n_rows):
    nl = CMP_STRIDE
    hid2 = KV_GROUPS * CMP_HIDDEN

    def hidden(src_ref, pe_ref, w1_ref):
        acc = jnp.zeros((n_rows + 16, 2 * hid2), F32)
        for l in range(nl):
            xl = src_ref[pl.ds(l, n_rows, stride=nl), :]
            pa = jnp.broadcast_to(pe_ref[l:l + 1, :], (8, KV_WIDTH))
            pb = jnp.broadcast_to(pe_ref[nl + l:nl + l + 1, :], (8, KV_WIDTH))
            lhs = jnp.concatenate([xl, pa, pb], axis=0).astype(BF16)
            acc = acc + _dot(lhs, w1_ref[l])
        sa = acc[0:n_rows, 0:hid2]
        sb = acc[0:n_rows, hid2:2 * hid2]
        bias = acc[n_rows:n_rows + 1, 0:hid2] + acc[n_rows + 8:n_rows + 9, hid2:2 * hid2]
        hcur = sa + pltpu.roll(sb, n_rows - 1, axis=0) + bias
        return _gelu_tanh(hcur).astype(BF16)

    hk = hidden(kcin_ref, pek_ref, wk1_ref)
    kc = _dot(hk, wk2_ref[...])
    for g in range(KV_GROUPS):
        kc_ref[g] = kc[:, g * LANES:(g + 1) * LANES].astype(BF16)
    hv = hidden(vcin_ref, pev_ref, wv1_ref)
    for g in range(KV_GROUPS):
        vct_ref[g] = _dot_nt(wv2_ref[g], hv).astype(BF16)


def _compress(kcin, vcin, pek2, wk1b, wk2b, pev2, wv1b, wv2t):
    b, t, _ = kcin.shape
    n_rows = t // CMP_STRIDE
    seq_spec = pl.BlockSpec((None, t, KV_WIDTH), lambda i: (i, 0, 0))
    return pl.pallas_call(
        functools.partial(_compress_kernel, n_rows=n_rows),
        grid=(b,),
        in_specs=[seq_spec, seq_spec,
                  _resident(pek2.shape), _resident(wk1b.shape), _resident(wk2b.shape),
                  _resident(pev2.shape), _resident(wv1b.shape), _resident(wv2t.shape)],
        out_specs=[pl.BlockSpec((None, KV_GROUPS, n_rows, LANES), lambda i: (i, 0, 0, 0)),
                   pl.BlockSpec((None, KV_GROUPS, HEAD_DIM, n_rows), lambda i: (i, 0, 0, 0))],
        out_shape=[jax.ShapeDtypeStruct((b, KV_GROUPS, n_rows, LANES), BF16),
                   jax.ShapeDtypeStruct((b, KV_GROUPS, HEAD_DIM, n_rows), BF16)],
        compiler_params=pltpu.CompilerParams(
            dimension_semantics=("parallel",), vmem_limit_bytes=VMEM_LIMIT),
        name="compress",
    )(kcin, vcin, pek2, wk1b, wk2b, pev2, wv1b, wv2t)


def _compress_weights(pe, w1, w2, value_layout):
    nl, dh, hid = CMP_STRIDE, HEAD_DIM, CMP_HIDDEN
    w1r = w1.reshape(2, nl, dh, hid)
    z = jnp.zeros((nl, dh, hid), w1.dtype)
    row_g0 = jnp.concatenate([w1r[0], z, w1r[1], z], axis=-1)
    row_g1 = jnp.concatenate([z, w1r[0], z, w1r[1]], axis=-1)
    w1b = jnp.concatenate([row_g0, row_g1], axis=1).astype(BF16)
    pe2 = jnp.concatenate([pe, pe], axis=-1)
    zz = jnp.zeros((hid, dh), w2.dtype)
    if value_layout:
        w2b = jnp.stack([jnp.concatenate([w2, zz], axis=0).T,
                         jnp.concatenate([zz, w2], axis=0).T]).astype(BF16)
    else:
        top = jnp.concatenate([w2, zz, zz, zz], axis=-1)
        bot = jnp.concatenate([zz, zz, w2, zz], axis=-1)
        w2b = jnp.concatenate([top, bot], axis=0).astype(BF16)
    return pe2, w1b, w2b


def _nsa_tables(t):
    slopes = _slopes().reshape(KV_GROUPS, HEADS_PER_GROUP)
    n_rows = t // CMP_STRIDE
    n_blk = t // SEL_BLOCK
    i = np.arange(Q_BLOCK)
    c = np.arange(n_rows)
    d0c = (i[None, :] - (c[:, None] * CMP_STRIDE + CMP_BLOCK - 1)).astype(np.int32)
    tblc = -LOG2E * slopes[:, None, :, None].astype(np.float64) * d0c[None, :, None, :]
    tblc = tblc.astype(np.float32)
    tblc = tblc.reshape(KV_GROUPS, n_rows, GQ)
    j = np.arange(WIN_KEYS)
    dist = i[None, :] + WINDOW - j[:, None]
    ok = (dist >= 0) & (dist < WINDOW)
    tblw = np.where(ok[None, :, None, :],
                    -LOG2E * slopes[:, None, :, None].astype(np.float64) * dist[None, :, None, :],
                    NEG).astype(np.float32).reshape(KV_GROUPS, WIN_KEYS, GQ)
    jj = np.arange(SEL_CHUNK)
    per = SEL_CHUNK // Q_BLOCK
    tailm = np.stack([np.where(jj[:, None] > i[None, :] + Q_BLOCK * r, np.float32(NEG), np.float32(0))
                      for r in range(per)]).astype(np.float32)
    ci = c[None, :] * CMP_STRIDE
    sj = np.arange(n_blk)[:, None] * SEL_BLOCK
    n_cmp = (t - CMP_BLOCK) // CMP_STRIDE + 1
    ov = ((ci <= sj + SEL_BLOCK - 1) & (ci + CMP_BLOCK - 1 >= sj) & (c[None, :] < n_cmp))
    ov = ov.astype(np.float32)
    pos = np.arange(t)
    ktag = np.zeros((t, 2 * LANES), np.float32)
    ktag[pos, HEAD_DIM + pos // SEL_BLOCK] = 1.0
    ktag[pos, LANES + pos // SEL_BLOCK] = 1.0
    ktag[:, LANES + HEAD_DIM] = pos % SEL_BLOCK
    ktag[:, LANES + HEAD_DIM + 1] = pos % SEL_BLOCK
    qflag = np.zeros((HEAD_DIM, GQ), np.float32)
    qflag[0, :] = NEG
    s2 = np.repeat(slopes, Q_BLOCK, axis=1).astype(np.float64) * LOG2E
    s2_hi = s2.astype(BF16).astype(np.float64)
    qslope = np.zeros((KV_GROUPS, HEAD_DIM, GQ), np.float32)
    qslope[:, 0, :] = s2_hi
    qslope[:, 1, :] = s2 - s2_hi
    return (jnp.asarray(tblc), jnp.asarray(tblw), jnp.asarray(tailm),
            jnp.asarray(ov, dtype=BF16), jnp.asarray(ktag, dtype=BF16),
            jnp.asarray(qflag, dtype=BF16), jnp.asarray(qslope, dtype=BF16))


def _nsa_kernel(qt_ref, gt_ref, ks_ref, kw_ref, vst_ref, vwt_ref, kc_ref, vct_ref,
                tblc_ref, tblw_ref, tailm_ref, ov_ref, ktag_ref, qflag_ref, qslope_ref,
                o_ref,
                kaug_ref, kwaug_ref, m_ref, acc_ref, need_ref, *, t, n_blk):
    qi = pl.program_id(1)
    slopes = _slopes()
    hg = HEADS_PER_GROUP
    prep_rows = WINDOW

    @pl.when(qi == 0)
    def _prep():
        low = lax.broadcasted_iota(jnp.int32, (prep_rows, LANES), 1) < HEAD_DIM
        flag = jnp.where(lax.broadcasted_iota(jnp.int32, (WINDOW, LANES), 1) == HEAD_DIM, 1.0, 0.0)
        for g in range(KV_GROUPS):
            kwaug_ref[g, 0:WINDOW, :] = flag.astype(BF16)

        def body(c, carry):
            r0 = pl.multiple_of(c * prep_rows, prep_rows)
            rows = pl.ds(r0, prep_rows)
            rows_w = pl.ds(r0 + WINDOW, prep_rows)
            tag = ktag_ref[rows, 0:LANES].astype(F32)
            ks = ks_ref[rows, :].astype(F32)
            kw = kw_ref[rows, :].astype(F32)
            ks_sw = pltpu.roll(ks, HEAD_DIM, axis=1)
            kw_sw = pltpu.roll(kw, HEAD_DIM, axis=1)
            for g in range(KV_GROUPS):
                kaug_ref[g, rows, 0:LANES] = jnp.where(low, ks if g == 0 else ks_sw, tag).astype(BF16)
                kaug_ref[g, rows, LANES:2 * LANES] = ktag_ref[rows, LANES:2 * LANES]
                kwaug_ref[g, rows_w, :] = jnp.where(low, kw if g == 0 else kw_sw, 0.0).astype(BF16)
            return carry

        lax.fori_loop(0, t // prep_rows, body, 0)

    gates = jax.nn.sigmoid(gt_ref[...])
    bi = lax.broadcasted_iota(jnp.int32, (n_blk, Q_BLOCK), 0)
    lq = lax.broadcasted_iota(jnp.int32, (n_blk, Q_BLOCK), 1)
    cur = (Q_BLOCK // SEL_BLOCK) * qi + lax.shift_right_logical(lq, SEL_BLOCK.bit_length() - 1)
    valid = bi <= cur
    near = (bi == 0) | (bi >= cur - 1)
    always = valid & (near | (cur < N_SEL))
    free = valid & jnp.logical_not(near)
    far = (bi >= FIRST_KEYS // SEL_BLOCK) & (bi < (qi * Q_BLOCK - PREV_KEYS) // SEL_BLOCK)
    rel_blk = (bi - cur).astype(F32)

    head_slabs = [slice(h * Q_BLOCK, (h + SLAB_HEADS) * Q_BLOCK) for h in range(0, hg, SLAB_HEADS)]

    q_pos = qi * Q_BLOCK + lax.broadcasted_iota(jnp.int32, (1, Q_BLOCK), 1)
    cmp_last = jnp.concatenate([lax.shift_right_arithmetic(q_pos - (CMP_BLOCK - 1), CMP_STRIDE.bit_length() - 1)] * hg,
                               axis=1)
    cmp_row = lax.broadcasted_iota(jnp.int32, (t // CMP_STRIDE, SLAB_HEADS * Q_BLOCK), 0)

    def with_ones(vt):
        first = lax.broadcasted_iota(jnp.int32, (ONES_ROWS, vt.shape[1]), 0) == 0
        return jnp.concatenate([vt, jnp.where(first, 1.0, 0.0).astype(BF16)], axis=0)

    def queries(g):
        heads = lambda ref: jnp.concatenate([ref[(g * hg + h) * HEAD_DIM:(g * hg + h + 1) * HEAD_DIM, :]
                                             for h in range(hg)], axis=1)
        qt = heads(qt_ref)
        return qt, jnp.concatenate([qt, qflag_ref[...]], axis=0)

    def select(g, qt, s):
        o_parts, psum = [], None
        for lanes in head_slabs:
            sh = s[:, lanes] + tblc_ref[g, :, lanes]
            sh = jnp.where(cmp_row <= cmp_last[:, lanes], sh, NEG)
            m = jnp.max(sh, axis=0, keepdims=True)
            m = jnp.where(m < 0.5 * NEG, 0.0, m)
            p = jnp.exp2(sh - m)
            l = jnp.sum(p, axis=0, keepdims=True)
            pn = p * (1.0 / jnp.maximum(l, 1e-30))
            o_parts.append(_dot(vct_ref[g], pn.astype(BF16)))
            for h in range(SLAB_HEADS):
                ph = pn[:, h * Q_BLOCK:(h + 1) * Q_BLOCK]
                psum = ph if psum is None else psum + ph
        o_cmp = jnp.concatenate(o_parts, axis=1)

        p_hi = psum.astype(BF16)
        p_lo = (psum - p_hi.astype(F32)).astype(BF16)
        imp = _dot(ov_ref[...], p_hi) + _dot(ov_ref[...], p_lo)
        val = jnp.where(free, imp, -1.0)
        sel = jnp.zeros((n_blk, Q_BLOCK), F32)
        for _ in range(N_SEL - 3):
            top = jnp.max(val, axis=0, keepdims=True)
            idx = jnp.min(jnp.where(val == top, bi, n_blk), axis=0, keepdims=True)
            pick = bi == idx
            sel = jnp.where(pick, 1.0, sel)
            val = jnp.where(pick, -2.0, val)
        chosen = jnp.where(always, 1.0, jnp.where(free, sel, 0.0)) > 0.0

        def tagged(blocks):
            bias = jnp.concatenate(
                [(float(slopes[g * hg + h]) * SEL_BLOCK * LOG2E) * rel_blk for h in range(hg)], axis=1)
            on = jnp.concatenate([blocks] * hg, axis=1)
            hi = bias.astype(BF16)
            lo = (bias - hi.astype(F32)).astype(BF16)
            pad = [jnp.zeros((HEAD_DIM - n_blk, GQ), BF16)] if n_blk < HEAD_DIM else []
            parts = [qt, jnp.where(on, hi, NEG)] + pad + [jnp.where(on, lo, 0.0)] + pad + [qslope_ref[g]]
            return jnp.concatenate(parts, axis=0)

        return o_cmp, tagged(chosen), tagged(chosen & far), jnp.where(chosen & far, 1.0, 0.0)

    def sel_keys(g, pieces):
        return jnp.concatenate([kaug_ref[g, pl.ds(pl.multiple_of(s0 * LANES, LANES), n * LANES), :]
                                for s0, n in pieces], axis=0)

    def sel_values(g, pieces):
        return with_ones(jnp.concatenate([vst_ref[s0 + k, g * HEAD_DIM:(g + 1) * HEAD_DIM, :]
                                          for s0, n in pieces for k in range(n)], axis=1))

    def sel_update(s, vt, carry):
        m, acc = carry
        ms, accs = [], []
        for lanes in head_slabs:
            sh, mh = s[:, lanes], m[:, lanes]
            m_new = jnp.maximum(mh, jnp.max(sh, axis=0, keepdims=True))
            alpha = jnp.exp2(mh - m_new)
            p = jnp.exp2(sh - m_new).astype(BF16)
            ms.append(m_new)
            accs.append(acc[:, lanes] * alpha + _dot(vt, p))
        return jnp.concatenate(ms, axis=1), jnp.concatenate(accs, axis=1)

    def sel_steps(pieces, q, mask, carries):
        scores = [_dot(sel_keys(g, pieces), q[g]) for g in groups]
        if mask is not None:
            scores = [jnp.concatenate([sc[0:mask.shape[0]] + mask, sc[mask.shape[0]:]], axis=0) for sc in scores]
        return tuple(sel_update(scores[g], sel_values(g, pieces), carries[g]) for g in groups)

    def window_scores(g, qw):
        rows = pl.ds(pl.multiple_of(qi * Q_BLOCK, Q_BLOCK), WIN_KEYS)
        return _dot(kwaug_ref[g, rows, :], qw)

    def window(g, s):
        back = WINDOW // LANES
        ahead = Q_BLOCK // LANES
        vt = jnp.concatenate([vwt_ref[jnp.maximum(qi * ahead - back + k, 0), g * HEAD_DIM:(g + 1) * HEAD_DIM, :]
                              for k in range(back + ahead)], axis=1)
        vt = with_ones(vt)
        outs = []
        for lanes in head_slabs:
            sh = s[:, lanes] + tblw_ref[g, :, lanes]
            m = jnp.max(sh, axis=0, keepdims=True)
            p = jnp.exp2(sh - m).astype(BF16)
            acc = _dot(vt, p)
            outs.append(acc[0:HEAD_DIM] * (1.0 / acc[HEAD_DIM:HEAD_DIM + 1]))
        return jnp.concatenate(outs, axis=1)

    groups = tuple(range(KV_GROUPS))
    qt, qw = zip(*[queries(g) for g in groups])
    s_cmp = [_dot(kc_ref[g], qw[g]) for g in groups]
    s_win = [window_scores(g, qw[g]) for g in groups]
    sel_out, o_win = [], []
    for g in groups:
        sel_out.append(select(g, qt[g], s_cmp[g]))
        o_win.append(window(g, s_win[g]))
    o_cmp, qaug, qfar, picked = zip(*sel_out)

    nq, n_first = Q_BLOCK // LANES, FIRST_KEYS // LANES
    future = (qi + 1) * nq
    diag_pieces = [(qi * nq, nq)]
    for back in range(1, PREV_KEYS // Q_BLOCK + 1):
        diag_pieces.append((jnp.where(qi >= back, (qi - back) * nq, future), nq))
    diag_pieces.append((jnp.where(qi * Q_BLOCK >= PREV_KEYS + FIRST_KEYS, 0, future), n_first))
    causal = jnp.concatenate([tailm_ref[0]] * hg, axis=1)
    init = (jnp.full((1, GQ), NEG, F32), jnp.zeros((HEAD_DIM + ONES_ROWS, GQ), F32))

    def load_carries():
        return tuple((m_ref[g, 0:1, :], acc_ref[g]) for g in groups)

    def store_carries(carries):
        for g, (m, acc) in enumerate(carries):
            m_ref[g, 0:1, :] = m
            acc_ref[g] = acc

    store_carries(sel_steps(diag_pieces, qaug, causal, tuple(init for _ in groups)))

    blk_per_chunk = FAR_CHUNK // SEL_BLOCK
    wanted = picked[0]
    for g in groups[1:]:
        wanted = wanted + picked[g]
    for c in range(n_blk // blk_per_chunk):
        hit = jnp.sum(wanted[c * blk_per_chunk:(c + 1) * blk_per_chunk, :])
        need_ref[c] = (hit > 0.0).astype(jnp.int32)

    def far_chunk(c, carry):
        @pl.when(need_ref[c] > 0)
        def _():
            store_carries(sel_steps([(c * (FAR_CHUNK // LANES), FAR_CHUNK // LANES)], qfar, None, load_carries()))
        return carry

    far_keys = jnp.maximum(qi * Q_BLOCK - PREV_KEYS, 0)
    lax.fori_loop(0, (far_keys + FAR_CHUNK - 1) // FAR_CHUNK, far_chunk, 0)
    o_slc = [acc[0:HEAD_DIM] * (1.0 / acc[HEAD_DIM:HEAD_DIM + 1]) for _, acc in load_carries()]

    for g in groups:
        for pr in range(hg // 2):
            outs = []
            for h in (2 * pr, 2 * pr + 1):
                hh = g * hg + h
                lanes = slice(h * Q_BLOCK, (h + 1) * Q_BLOCK)
                out = gates[3 * hh:3 * hh + 1, :] * o_cmp[g][:, lanes]
                out = out + gates[3 * hh + 1:3 * hh + 2, :] * o_slc[g][:, lanes]
                out = out + gates[3 * hh + 2:3 * hh + 3, :] * o_win[g][:, lanes]
                outs.append(out)
            pair = jnp.concatenate(outs, axis=0)
            o_ref[:, (g * 2 + pr) * LANES:(g * 2 + pr + 1) * LANES] = pair.T.astype(BF16)


def _nsa(qt, gt, ks, kw, vst, vwt, kc, vct, tables):
    b, _, t = qt.shape
    n_blk = t // SEL_BLOCK
    n_rows = t // CMP_STRIDE
    assert SEL_CHUNK == Q_BLOCK and PREV_KEYS % Q_BLOCK == 0 and t % FAR_CHUNK == 0
    assert t >= PREV_KEYS + 2 * Q_BLOCK
    assert N_SEL <= n_blk <= HEAD_DIM and n_blk % 16 == 0
    seq_spec = pl.BlockSpec((None, t, LANES), lambda i, j: (i, 0, 0))
    chunked = pl.BlockSpec((None, t // LANES, KV_WIDTH, LANES), lambda i, j: (i, 0, 0, 0))
    return pl.pallas_call(
        functools.partial(_nsa_kernel, t=t, n_blk=n_blk),
        grid=(b, t // Q_BLOCK),
        in_specs=[pl.BlockSpec((None, NSA_WIDTH, Q_BLOCK), lambda i, j: (i, 0, j)),
                  pl.BlockSpec((None, GATE_ROWS, Q_BLOCK), lambda i, j: (i, 0, j)),
                  seq_spec, seq_spec, chunked, chunked,
                  pl.BlockSpec((None, KV_GROUPS, n_rows, LANES), lambda i, j: (i, 0, 0, 0)),
                  pl.BlockSpec((None, KV_GROUPS, HEAD_DIM, n_rows), lambda i, j: (i, 0, 0, 0))]
                 + [_resident(a.shape) for a in tables],
        out_specs=pl.BlockSpec((None, Q_BLOCK, NSA_WIDTH), lambda i, j: (i, j, 0)),
        out_shape=jax.ShapeDtypeStruct((b, t, NSA_WIDTH), BF16),
        scratch_shapes=[pltpu.VMEM((KV_GROUPS, t, 2 * LANES), BF16),
                        pltpu.VMEM((KV_GROUPS, t + WINDOW, LANES), BF16),
                        pltpu.VMEM((KV_GROUPS, 8, GQ), F32),
                        pltpu.VMEM((KV_GROUPS, HEAD_DIM + ONES_ROWS, GQ), F32),
                        pltpu.SMEM((t // FAR_CHUNK,), jnp.int32)],
        compiler_params=pltpu.CompilerParams(
            dimension_semantics=("arbitrary", "arbitrary"), vmem_limit_bytes=VMEM_LIMIT),
        name="nsa",
    )(qt, gt, ks, kw, vst, vwt, kc, vct, *tables)


def _mix_ffn_kernel(x_ref, o_ref, u_ref, halo_ref, wp_ref, sc_ref, wo_ref, g2_ref, wg_ref, wu_ref, wd_ref,
                    *rest, tm, final):
    if final:
        fn_ref, y_ref, mix_ref, h_ref, acc_ref = rest
    else:
        y_ref, mix_ref, h_ref, acc_ref = rest
    ti = pl.program_id(1)
    mix_ref[:, 0:NSA_WIDTH] = o_ref[...]
    halo = jnp.where(ti > 0, halo_ref[...], 0.0)
    pos = ti * tm + lax.broadcasted_iota(jnp.int32, (tm, 1), 0)
    for gi, w in enumerate(POOL_WIDTHS):
        lanes = slice(gi * POOL_GROUP, (gi + 1) * POOL_GROUP)
        u = u_ref[:, lanes]
        ext = jnp.concatenate([halo[:, lanes], u], axis=0)
        run = ext
        step = 1
        while step < w:
            run = run + pltpu.roll(run, step, axis=0)
            step *= 2
        cnt = jnp.minimum(pos + 1, w).astype(F32)
        y = run[POOL_HALO:, :] / cnt - u
        yo = _dot(y.astype(BF16), wp_ref[gi]) * sc_ref[:, lanes]
        mix_ref[:, NSA_WIDTH + gi * POOL_GROUP:NSA_WIDTH + (gi + 1) * POOL_GROUP] = yo.astype(BF16)
    x2 = x_ref[...] + _dot(mix_ref[...], wo_ref[...])
    y = _swiglu_residual(x2, g2_ref, wg_ref, wu_ref, wd_ref, h_ref, acc_ref)
    if final:
        y = _rms(y, fn_ref[...])
    y_ref[...] = y


def _mix_ffn(x3, layer, o_nsa, u, wp, sc, wo, g2, wg, wu, wd, final_g=None, tm=512):
    b, t, d = x3.shape
    per = tm // POOL_HALO
    final = final_g is not None
    tail_specs = [_resident((1, d))] if final else []
    tail_args = [final_g] if final else []
    return pl.pallas_call(
        functools.partial(_mix_ffn_kernel, tm=tm, final=final),
        grid=(b, t // tm),
        in_specs=[pl.BlockSpec((None, tm, d), lambda i, j: (i, j, 0)),
                  pl.BlockSpec((None, tm, NSA_WIDTH), lambda i, j: (i, j, 0)),
                  pl.BlockSpec((None, tm, POOL_CH), lambda i, j: (i, j, 0)),
                  pl.BlockSpec((None, POOL_HALO, POOL_CH), lambda i, j: (i, jnp.maximum(j * per - 1, 0), 0)),
                  _resident(wp.shape[1:], layer), _resident(sc.shape), _resident(wo.shape[1:], layer),
                  _resident((1, d)), _resident(wg.shape[1:], layer), _resident(wu.shape[1:], layer),
                  _resident(wd.shape[1:], layer)] + tail_specs,
        out_specs=pl.BlockSpec((None, tm, d), lambda i, j: (i, j, 0)),
        out_shape=jax.ShapeDtypeStruct((b, t, d), F32),
        scratch_shapes=[pltpu.VMEM((tm, NSA_WIDTH + POOL_CH), BF16),
                        pltpu.VMEM((tm, d), BF16), pltpu.VMEM((tm, d), F32)],
        compiler_params=pltpu.CompilerParams(
            dimension_semantics=("parallel", "parallel"), vmem_limit_bytes=VMEM_LIMIT),
        name="mix_ffn_final" if final else "mix_ffn",
    )(x3, o_nsa, u, u, wp, sc, wo, g2, wg, wu, wd, *tail_args)


def kernel(x, ffn1_norm, ffn1_wg, ffn1_wu, ffn1_wd, mix_norm, w_in, cmp_pe_k, cmp_wk1, cmp_wk2,
           cmp_pe_v, cmp_wv1, cmp_wv2, pool_w, pool_scale, w_out, ffn2_norm, ffn2_wg, ffn2_wu,
           ffn2_wd, final_norm):
    b, t, d = x.shape
    depth = ffn1_norm.shape[0]
    tables = _nsa_tables(t)
    f1g, f1u, f1d, f2g, f2u, f2d, wp, wo = (
        a.astype(BF16) for a in (ffn1_wg, ffn1_wu, ffn1_wd, ffn2_wg, ffn2_wu, ffn2_wd, pool_w, w_out))
    for l in range(depth):
        x, kcin, vcin, ks, kw, u, qt, vst, vwt, gt = _ffn_proj(
            x, l, ffn1_norm[l][None], f1g, f1u, f1d, mix_norm[l][None], *_split_w_in(w_in[l]))
        kc, vct = _compress(kcin, vcin,
                            *_compress_weights(cmp_pe_k[l], cmp_wk1[l], cmp_wk2[l], False),
                            *_compress_weights(cmp_pe_v[l], cmp_wv1[l], cmp_wv2[l], True))
        o_nsa = _nsa(qt, gt, ks, kw, vst, vwt, kc, vct, tables)
        x = _mix_ffn(x, l, o_nsa, u, wp, pool_scale[l][None], wo, ffn2_norm[l][None], f2g, f2u, f2d,
                     final_g=final_norm[None] if l == depth - 1 else None)
    return x
```

```python
import functools

import numpy as np
import jax
import jax.numpy as jnp
from jax import lax
from jax.experimental import pallas as pl
from jax.experimental.pallas import tpu as pltpu

HEAD_DIM = 64
NSA_HEADS = 8
KV_GROUPS = 2
HEADS_PER_GROUP = NSA_HEADS // KV_GROUPS
NSA_WIDTH = NSA_HEADS * HEAD_DIM
KV_WIDTH = KV_GROUPS * HEAD_DIM
CMP_BLOCK = 32
CMP_STRIDE = 16
CMP_HIDDEN = 128
SEL_BLOCK = 64
N_SEL = 8
WINDOW = 512
Q_BLOCK = 256
POOL_WIDTHS = (2, 4, 8, 16)
POOL_GROUP = 128
POOL_CH = POOL_GROUP * len(POOL_WIDTHS)
N_GATES = 3 * NSA_HEADS
GATE_ROWS = 32
EPS = 1e-6

LANES = 128
GQ = HEADS_PER_GROUP * Q_BLOCK
SEL_CHUNK = 256
PREV_KEYS = 512
FAR_CHUNK = 512
SLAB_HEADS = 2
FIRST_KEYS = 128
WIN_KEYS = WINDOW + Q_BLOCK
ONES_ROWS = 16
POOL_HALO = 16
FF_CHUNK = 256
LOG2E = 1.4426950408889634
NEG = -1e30
VMEM_LIMIT = 56 * 1024 * 1024

F32 = jnp.float32
BF16 = jnp.bfloat16


def _slopes():
    n = NSA_HEADS
    return (2.0 ** (-8.0 * np.arange(1, n + 1) / n)).astype(np.float32)


def _dot(a, b):
    return jnp.dot(a, b, preferred_element_type=F32)


def _dot_nt(a, b):
    return lax.dot_general(a, b, (((1,), (1,)), ((), ())), preferred_element_type=F32)


def _rms(x, g):
    return x * lax.rsqrt(jnp.mean(x * x, axis=-1, keepdims=True) + EPS) * g


def _resident(shape, layer=None):
    nd = len(shape)
    if layer is None:
        return pl.BlockSpec(shape, lambda *_: (0,) * nd, pipeline_mode=pl.Buffered(1))
    return pl.BlockSpec((None,) + tuple(shape), lambda *_: (layer,) + (0,) * nd, pipeline_mode=pl.Buffered(1))


def _swiglu_residual(x, g_ref, wg_ref, wu_ref, wd_ref, h_ref, acc_ref):
    h_ref[...] = _rms(x, g_ref[...]).astype(BF16)
    for c in range(wg_ref.shape[1] // FF_CHUNK):
        cols = slice(c * FF_CHUNK, (c + 1) * FF_CHUNK)
        h = h_ref[...]
        gate = _dot(h, wg_ref[:, cols])
        up = _dot(h, wu_ref[:, cols])
        a = (gate * jax.nn.sigmoid(gate) * up).astype(BF16)
        part = _dot(a, wd_ref[cols, :])
        if c == 0:
            acc_ref[...] = part
        else:
            acc_ref[...] += part
    return x + 0.5 * acc_ref[...]


_TOK_KC = (0, KV_WIDTH)
_TOK_VC = (KV_WIDTH, 2 * KV_WIDTH)
_TOK_KS = (2 * KV_WIDTH, 3 * KV_WIDTH)
_TOK_KW = (3 * KV_WIDTH, 4 * KV_WIDTH)
_TOK_U = (4 * KV_WIDTH, 4 * KV_WIDTH + POOL_CH)
_FEAT_Q = (0, NSA_WIDTH)
_FEAT_VS = (NSA_WIDTH, NSA_WIDTH + KV_WIDTH)
_FEAT_VW = (NSA_WIDTH + KV_WIDTH, NSA_WIDTH + 2 * KV_WIDTH)
_FEAT_GT = (NSA_WIDTH + 2 * KV_WIDTH, NSA_WIDTH + 2 * KV_WIDTH + GATE_ROWS)


def _ffn_proj_kernel(x_ref, g1_ref, wg_ref, wu_ref, wd_ref, g_ref, wt_ref, wf_ref,
                     x1_ref, kcin_ref, vcin_ref, ks_ref, kw_ref, u_ref,
                     qt_ref, vst_ref, vwt_ref, gt_ref, h_ref, acc_ref, *, tm):
    x1 = _swiglu_residual(x_ref[...], g1_ref, wg_ref, wu_ref, wd_ref, h_ref, acc_ref)
    x1_ref[...] = x1
    h = _rms(x1, g_ref[...]).astype(BF16)

    def tok(lo_hi):
        return _dot(h, wt_ref[:, lo_hi[0]:lo_hi[1]])

    kcin_ref[...] = tok(_TOK_KC)
    vcin_ref[...] = tok(_TOK_VC)
    ks_ref[...] = tok(_TOK_KS).astype(BF16)
    kw_ref[...] = tok(_TOK_KW).astype(BF16)
    u_ref[...] = tok(_TOK_U)
    zt = _dot_nt(wf_ref[...], h)
    qt_ref[...] = (zt[_FEAT_Q[0]:_FEAT_Q[1]] * (HEAD_DIM ** -0.5 * LOG2E)).astype(BF16)
    gt_ref[...] = zt[_FEAT_GT[0]:_FEAT_GT[1]]
    for c in range(tm // LANES):
        cols = slice(c * LANES, (c + 1) * LANES)
        vst_ref[c] = zt[_FEAT_VS[0]:_FEAT_VS[1], cols].astype(BF16)
        vwt_ref[c] = zt[_FEAT_VW[0]:_FEAT_VW[1], cols].astype(BF16)


def _ffn_proj(x3, layer, g1, wg, wu, wd, g, w_tok, w_feat, tm=512):
    b, t, d = x3.shape
    nck = tm // LANES
    tok = lambda wd: pl.BlockSpec((None, tm, wd), lambda i, j: (i, j, 0))
    chunked = pl.BlockSpec((None, nck, KV_WIDTH, LANES), lambda i, j: (i, j, 0, 0))
    return pl.pallas_call(
        functools.partial(_ffn_proj_kernel, tm=tm),
        grid=(b, t // tm),
        in_specs=[tok(d), _resident((1, d)),
                  _resident(wg.shape[1:], layer), _resident(wu.shape[1:], layer), _resident(wd.shape[1:], layer),
                  _resident((1, d)), _resident(w_tok.shape), _resident(w_feat.shape)],
        out_specs=[tok(d), tok(KV_WIDTH), tok(KV_WIDTH), tok(KV_WIDTH), tok(KV_WIDTH), tok(POOL_CH),
                   pl.BlockSpec((None, NSA_WIDTH, tm), lambda i, j: (i, 0, j)),
                   chunked, chunked,
                   pl.BlockSpec((None, GATE_ROWS, tm), lambda i, j: (i, 0, j))],
        out_shape=[jax.ShapeDtypeStruct((b, t, d), F32),
                   jax.ShapeDtypeStruct((b, t, KV_WIDTH), F32),
                   jax.ShapeDtypeStruct((b, t, KV_WIDTH), F32),
                   jax.ShapeDtypeStruct((b, t, KV_WIDTH), BF16),
                   jax.ShapeDtypeStruct((b, t, KV_WIDTH), BF16),
                   jax.ShapeDtypeStruct((b, t, POOL_CH), F32),
                   jax.ShapeDtypeStruct((b, NSA_WIDTH, t), BF16),
                   jax.ShapeDtypeStruct((b, t // LANES, KV_WIDTH, LANES), BF16),
                   jax.ShapeDtypeStruct((b, t // LANES, KV_WIDTH, LANES), BF16),
                   jax.ShapeDtypeStruct((b, GATE_ROWS, t), F32)],
        scratch_shapes=[pltpu.VMEM((tm, d), BF16), pltpu.VMEM((tm, d), F32)],
        compiler_params=pltpu.CompilerParams(
            dimension_semantics=("parallel", "parallel"), vmem_limit_bytes=VMEM_LIMIT),
        name="ffn_proj",
    )(x3, g1, wg, wu, wd, g, w_tok, w_feat)


def _split_w_in(w):
    q0, kv0 = 0, NSA_WIDTH
    col = lambda i: slice(kv0 + i * KV_WIDTH, kv0 + (i + 1) * KV_WIDTH)
    g0 = kv0 + 6 * KV_WIDTH
    w_tok = jnp.concatenate([w[:, col(0)], w[:, col(1)], w[:, col(2)], w[:, col(4)],
                             w[:, g0 + N_GATES:]], axis=1).astype(BF16)
    pad = jnp.zeros((w.shape[0], GATE_ROWS - N_GATES), w.dtype)
    w_feat = jnp.concatenate([w[:, q0:NSA_WIDTH], w[:, col(3)], w[:, col(5)],
                              w[:, g0:g0 + N_GATES], pad], axis=1).T.astype(BF16)
    return w_tok, w_feat


def _gelu_tanh(x):
    c = np.float32(np.sqrt(2.0 / np.pi))
    return 0.5 * x * (1.0 + jnp.tanh(c * (x + 0.044715 * (x * x * x))))


def _compress_kernel(kcin_ref, vcin_ref, pek_ref, wk1_ref, wk2_ref, pev_ref, wv1_ref, wv2_ref,
                     kc_ref, vct_ref, *, n_rows):
    nl = CMP_STRIDE
    hid2 = KV_GROUPS * CMP_HIDDEN

    def hidden(src_ref, pe_ref, w1_ref):
        acc = jnp.zeros((n_rows + 16, 2 * hid2), F32)
        for l in range(nl):
            xl = src_ref[pl.ds(l, n_rows, stride=nl), :]
            pa = jnp.broadcast_to(pe_ref[l:l + 1, :], (8, KV_WIDTH))
            pb = jnp.broadcast_to(pe_ref[nl + l:nl + l + 1, :], (8, KV_WIDTH))
            lhs = jnp.concatenate([xl, pa, pb], axis=0).astype(BF16)
            acc = acc + _dot(lhs, w1_ref[l])
        sa = acc[0:n_rows, 0:hid2]
        sb = acc[0:n_rows, hid2:2 * hid2]
        bias = acc[n_rows:n_rows + 1, 0:hid2] + acc[n_rows + 8:n_rows + 9, hid2:2 * hid2]
        hcur = sa + pltpu.roll(sb, n_rows - 1, axis=0) + bias
        return _gelu_tanh(hcur).astype(BF16)

    hk = hidden(kcin_ref, pek_ref, wk1_ref)
    kc = _dot(hk, wk2_ref[...])
    for g in range(KV_GROUPS):
        kc_ref[g] = kc[:, g * LANES:(g + 1) * LANES].astype(BF16)
    hv = hidden(vcin_ref, pev_ref, wv1_ref)
    for g in range(KV_GROUPS):
        vct_ref[g] = _dot_nt(wv2_ref[g], hv).astype(BF16)


def _compress(kcin, vcin, pek2, wk1b, wk2b, pev2, wv1b, wv2t):
    b, t, _ = kcin.shape
    n_rows = t // CMP_STRIDE
    seq_spec = pl.BlockSpec((None, t, KV_WIDTH), lambda i: (i, 0, 0))
    return pl.pallas_call(
        functools.partial(_compress_kernel, n_rows=n_rows),
        grid=(b,),
        in_specs=[seq_spec, seq_spec,
                  _resident(pek2.shape), _resident(wk1b.shape), _resident(wk2b.shape),
                  _resident(pev2.shape), _resident(wv1b.shape), _resident(wv2t.shape)],
        out_specs=[pl.BlockSpec((None, KV_GROUPS, n_rows, LANES), lambda i: (i, 0, 0, 0)),
                   pl.BlockSpec((None, KV_GROUPS, HEAD_DIM, n_rows), lambda i: (i, 0, 0, 0))],
        out_shape=[jax.ShapeDtypeStruct((b, KV_GROUPS, n_rows, LANES), BF16),
                   jax.ShapeDtypeStruct((b, KV_GROUPS, HEAD_DIM, n_rows), BF16)],
        compiler_params=pltpu.CompilerParams(
            dimension_semantics=("parallel",), vmem_limit_bytes=VMEM_LIMIT),
        name="compress",
    )(kcin, vcin, pek2, wk1b, wk2b, pev2, wv1b, wv2t)


def _compress_weights(pe, w1, w2, value_layout):
    nl, dh, hid = CMP_STRIDE, HEAD_DIM, CMP_HIDDEN
    w1r = w1.reshape(2, nl, dh, hid)
    z = jnp.zeros((nl, dh, hid), w1.dtype)
    row_g0 = jnp.concatenate([w1r[0], z, w1r[1], z], axis=-1)
    row_g1 = jnp.concatenate([z, w1r[0], z, w1r[1]], axis=-1)
    w1b = jnp.concatenate([row_g0, row_g1], axis=1).astype(BF16)
    pe2 = jnp.concatenate([pe, pe], axis=-1)
    zz = jnp.zeros((hid, dh), w2.dtype)
    if value_layout:
        w2b = jnp.stack([jnp.concatenate([w2, zz], axis=0).T,
                         jnp.concatenate([zz, w2], axis=0).T]).astype(BF16)
    else:
        top = jnp.concatenate([w2, zz, zz, zz], axis=-1)
        bot = jnp.concatenate([zz, zz, w2, zz], axis=-1)
        w2b = jnp.concatenate([top, bot], axis=0).astype(BF16)
    return pe2, w1b, w2b


def _nsa_tables(t):
    slopes = _slopes().reshape(KV_GROUPS, HEADS_PER_GROUP)
    n_rows = t // CMP_STRIDE
    n_blk = t // SEL_BLOCK
    i = np.arange(Q_BLOCK)
    c = np.arange(n_rows)
    d0c = (i[None, :] - (c[:, None] * CMP_STRIDE + CMP_BLOCK - 1)).astype(np.int32)
    tblc = -LOG2E * slopes[:, None, :, None].astype(np.float64) * d0c[None, :, None, :]
    tblc = tblc.astype(np.float32)
    tblc = tblc.reshape(KV_GROUPS, n_rows, GQ)
    e = np.arange(Q_BLOCK)
    wmask = np.stack([np.where(e[:, None] <= i[None, :], np.float32(NEG), np.float32(0)),
                      np.where(e[:, None] > i[None, :], np.float32(NEG), np.float32(0))])
    jj = np.arange(SEL_CHUNK)
    per = SEL_CHUNK // Q_BLOCK
    tailm = np.stack([np.where(jj[:, None] > i[None, :] + Q_BLOCK * r, np.float32(NEG), np.float32(0))
                      for r in range(per)]).astype(np.float32)
    ci = c[None, :] * CMP_STRIDE
    sj = np.arange(n_blk)[:, None] * SEL_BLOCK
    n_cmp = (t - CMP_BLOCK) // CMP_STRIDE + 1
    ov = ((ci <= sj + SEL_BLOCK - 1) & (ci + CMP_BLOCK - 1 >= sj) & (c[None, :] < n_cmp))
    ov = ov.astype(np.float32)
    pos = np.arange(t)
    ktag = np.zeros((t, 2 * LANES), np.float32)
    ktag[pos, HEAD_DIM + pos // SEL_BLOCK] = 1.0
    ktag[pos, LANES + pos // SEL_BLOCK] = 1.0
    ktag[:, LANES + HEAD_DIM] = pos % SEL_BLOCK
    ktag[:, LANES + HEAD_DIM + 1] = pos % SEL_BLOCK
    wtag = np.zeros((t, LANES), np.float32)
    wtag[:, HEAD_DIM + 1] = wtag[:, HEAD_DIM + 2] = pos // SEL_BLOCK
    wtag[:, HEAD_DIM + 3] = wtag[:, HEAD_DIM + 4] = pos % SEL_BLOCK
    s2 = np.repeat(slopes, Q_BLOCK, axis=1).astype(np.float64) * LOG2E
    s2_hi = s2.astype(BF16).astype(np.float64)
    b2 = s2 * SEL_BLOCK
    b2_hi = b2.astype(BF16).astype(np.float64)
    qflag = np.zeros((KV_GROUPS, HEAD_DIM, GQ), np.float32)
    qflag[:, 0, :] = NEG
    qflag[:, 1, :], qflag[:, 2, :] = b2_hi, b2 - b2_hi
    qflag[:, 3, :], qflag[:, 4, :] = s2_hi, s2 - s2_hi
    qslope = np.zeros((KV_GROUPS, HEAD_DIM, GQ), np.float32)
    qslope[:, 0, :] = s2_hi
    qslope[:, 1, :] = s2 - s2_hi
    return (jnp.asarray(tblc), jnp.asarray(wmask), jnp.asarray(tailm),
            jnp.asarray(ov, dtype=BF16), jnp.asarray(ktag, dtype=BF16), jnp.asarray(wtag, dtype=BF16),
            jnp.asarray(qflag, dtype=BF16), jnp.asarray(qslope, dtype=BF16))


def _nsa_kernel(qt_ref, gt_ref, ks_ref, kw_ref, vst_ref, vwt_ref, kc_ref, vct_ref,
                tblc_ref, wmask_ref, tailm_ref, ov_ref, ktag_ref, wtag_ref, qflag_ref, qslope_ref,
                o_ref,
                kaug_ref, kwaug_ref, m_ref, acc_ref, need_ref, *, t, n_blk):
    qi = pl.program_id(1)
    slopes = _slopes()
    hg = HEADS_PER_GROUP
    prep_rows = WINDOW

    @pl.when(qi == 0)
    def _prep():
        low = lax.broadcasted_iota(jnp.int32, (prep_rows, LANES), 1) < HEAD_DIM
        flag = jnp.where(lax.broadcasted_iota(jnp.int32, (WINDOW, LANES), 1) == HEAD_DIM, 1.0, 0.0)
        for g in range(KV_GROUPS):
            kwaug_ref[g, 0:WINDOW, :] = flag.astype(BF16)

        def body(c, carry):
            r0 = pl.multiple_of(c * prep_rows, prep_rows)
            rows = pl.ds(r0, prep_rows)
            rows_w = pl.ds(r0 + WINDOW, prep_rows)
            tag = ktag_ref[rows, 0:LANES].astype(F32)
            wtag = wtag_ref[rows, :].astype(F32)
            ks = ks_ref[rows, :].astype(F32)
            kw = kw_ref[rows, :].astype(F32)
            ks_sw = pltpu.roll(ks, HEAD_DIM, axis=1)
            kw_sw = pltpu.roll(kw, HEAD_DIM, axis=1)
            for g in range(KV_GROUPS):
                kaug_ref[g, rows, 0:LANES] = jnp.where(low, ks if g == 0 else ks_sw, tag).astype(BF16)
                kaug_ref[g, rows, LANES:2 * LANES] = ktag_ref[rows, LANES:2 * LANES]
                kwaug_ref[g, rows_w, :] = jnp.where(low, kw if g == 0 else kw_sw, wtag).astype(BF16)
            return carry

        lax.fori_loop(0, t // prep_rows, body, 0)

    gates = jax.nn.sigmoid(gt_ref[...])
    bi = lax.broadcasted_iota(jnp.int32, (n_blk, Q_BLOCK), 0)
    lq = lax.broadcasted_iota(jnp.int32, (n_blk, Q_BLOCK), 1)
    cur = (Q_BLOCK // SEL_BLOCK) * qi + lax.shift_right_logical(lq, SEL_BLOCK.bit_length() - 1)
    valid = bi <= cur
    near = (bi == 0) | (bi >= cur - 1)
    always = valid & (near | (cur < N_SEL))
    free = valid & jnp.logical_not(near)
    far = (bi >= FIRST_KEYS // SEL_BLOCK) & (bi < (qi * Q_BLOCK - PREV_KEYS) // SEL_BLOCK)
    rel_blk = (bi - cur).astype(F32)

    head_slabs = [slice(h * Q_BLOCK, (h + SLAB_HEADS) * Q_BLOCK) for h in range(0, hg, SLAB_HEADS)]

    q_pos = qi * Q_BLOCK + lax.broadcasted_iota(jnp.int32, (1, Q_BLOCK), 1)
    cmp_last = jnp.concatenate([lax.shift_right_arithmetic(q_pos - (CMP_BLOCK - 1), CMP_STRIDE.bit_length() - 1)] * hg,
                               axis=1)
    cmp_row = lax.broadcasted_iota(jnp.int32, (t // CMP_STRIDE, SLAB_HEADS * Q_BLOCK), 0)

    def with_ones(vt):
        first = lax.broadcasted_iota(jnp.int32, (ONES_ROWS, vt.shape[1]), 0) == 0
        return jnp.concatenate([vt, jnp.where(first, 1.0, 0.0).astype(BF16)], axis=0)

    def queries(g):
        heads = lambda ref: jnp.concatenate([ref[(g * hg + h) * HEAD_DIM:(g * hg + h + 1) * HEAD_DIM, :]
                                             for h in range(hg)], axis=1)
        qt = heads(qt_ref)
        return qt, jnp.concatenate([qt, qflag_ref[g]], axis=0)

    def select(g, qt, s):
        o_parts, psum = [], None
        for lanes in head_slabs:
            sh = s[:, lanes] + tblc_ref[g, :, lanes]
            sh = jnp.where(cmp_row <= cmp_last[:, lanes], sh, NEG)
            m = jnp.max(sh, axis=0, keepdims=True)
            m = jnp.where(m < 0.5 * NEG, 0.0, m)
            p = jnp.exp2(sh - m)
            l = jnp.sum(p, axis=0, keepdims=True)
            pn = p * (1.0 / jnp.maximum(l, 1e-30))
            o_parts.append(_dot(vct_ref[g], pn.astype(BF16)))
            for h in range(SLAB_HEADS):
                ph = pn[:, h * Q_BLOCK:(h + 1) * Q_BLOCK]
                psum = ph if psum is None else psum + ph
        o_cmp = jnp.concatenate(o_parts, axis=1)

        p_hi = psum.astype(BF16)
        p_lo = (psum - p_hi.astype(F32)).astype(BF16)
        imp = _dot(ov_ref[...], p_hi) + _dot(ov_ref[...], p_lo)
        val = jnp.where(free, imp, -1.0)
        sel = jnp.zeros((n_blk, Q_BLOCK), F32)
        for _ in range(N_SEL - 3):
            top = jnp.max(val, axis=0, keepdims=True)
            idx = jnp.min(jnp.where(val == top, bi, n_blk), axis=0, keepdims=True)
            pick = bi == idx
            sel = jnp.where(pick, 1.0, sel)
            val = jnp.where(pick, -2.0, val)
        chosen = jnp.where(always, 1.0, jnp.where(free, sel, 0.0)) > 0.0

        def tagged(blocks):
            bias = jnp.concatenate(
                [(float(slopes[g * hg + h]) * SEL_BLOCK * LOG2E) * rel_blk for h in range(hg)], axis=1)
            on = jnp.concatenate([blocks] * hg, axis=1)
            hi = bias.astype(BF16)
            lo = (bias - hi.astype(F32)).astype(BF16)
            pad = [jnp.zeros((HEAD_DIM - n_blk, GQ), BF16)] if n_blk < HEAD_DIM else []
            parts = [qt, jnp.where(on, hi, NEG)] + pad + [jnp.where(on, lo, 0.0)] + pad + [qslope_ref[g]]
            return jnp.concatenate(parts, axis=0)

        return o_cmp, tagged(chosen), tagged(chosen & far), jnp.where(chosen & far, 1.0, 0.0)

    def sel_keys(g, pieces):
        return jnp.concatenate([kaug_ref[g, pl.ds(pl.multiple_of(s0 * LANES, LANES), n * LANES), :]
                                for s0, n in pieces], axis=0)

    def sel_values(g, pieces):
        return with_ones(jnp.concatenate([vst_ref[s0 + k, g * HEAD_DIM:(g + 1) * HEAD_DIM, :]
                                          for s0, n in pieces for k in range(n)], axis=1))

    def sel_update(s, vt, carry):
        m, acc = carry
        ms, accs = [], []
        for lanes in head_slabs:
            sh, mh = s[:, lanes], m[:, lanes]
            m_new = jnp.maximum(mh, jnp.max(sh, axis=0, keepdims=True))
            alpha = jnp.exp2(mh - m_new)
            p = jnp.exp2(sh - m_new).astype(BF16)
            ms.append(m_new)
            accs.append(acc[:, lanes] * alpha + _dot(vt, p))
        return jnp.concatenate(ms, axis=1), jnp.concatenate(accs, axis=1)

    def sel_steps(pieces, q, mask, carries):
        scores = [_dot(sel_keys(g, pieces), q[g]) for g in groups]
        if mask is not None:
            scores = [jnp.concatenate([sc[0:mask.shape[0]] + mask, sc[mask.shape[0]:]], axis=0) for sc in scores]
        return tuple(sel_update(scores[g], sel_values(g, pieces), carries[g]) for g in groups)

    def window_scores(g, qw):
        rows = pl.ds(pl.multiple_of(qi * Q_BLOCK, Q_BLOCK), WIN_KEYS)
        return _dot(kwaug_ref[g, rows, :], qw)

    def window(g, s):
        back = WINDOW // LANES
        ahead = Q_BLOCK // LANES
        vt = jnp.concatenate([vwt_ref[jnp.maximum(qi * ahead - back + k, 0), g * HEAD_DIM:(g + 1) * HEAD_DIM, :]
                              for k in range(back + ahead)], axis=1)
        vt = with_ones(vt)
        outs = []
        too_old = jnp.concatenate([wmask_ref[0]] * SLAB_HEADS, axis=1)
        ahead_of = jnp.concatenate([wmask_ref[1]] * SLAB_HEADS, axis=1)
        for lanes in head_slabs:
            sh = jnp.concatenate([s[0:Q_BLOCK, lanes] + too_old, s[Q_BLOCK:WINDOW, lanes],
                                  s[WINDOW:WIN_KEYS, lanes] + ahead_of], axis=0)
            m = jnp.max(sh, axis=0, keepdims=True)
            p = jnp.exp2(sh - m).astype(BF16)
            acc = _dot(vt, p)
            outs.append(acc[0:HEAD_DIM] * (1.0 / acc[HEAD_DIM:HEAD_DIM + 1]))
        return jnp.concatenate(outs, axis=1)

    groups = tuple(range(KV_GROUPS))
    qt, qw = zip(*[queries(g) for g in groups])
    s_cmp = [_dot(kc_ref[g], qw[g]) for g in groups]
    s_win = [window_scores(g, qw[g]) for g in groups]
    sel_out, o_win = [], []
    for g in groups:
        sel_out.append(select(g, qt[g], s_cmp[g]))
        o_win.append(window(g, s_win[g]))
    o_cmp, qaug, qfar, picked = zip(*sel_out)

    nq, n_first = Q_BLOCK // LANES, FIRST_KEYS // LANES
    future = (qi + 1) * nq
    diag_pieces = [(qi * nq, nq)]
    for back in range(1, PREV_KEYS // Q_BLOCK + 1):
        diag_pieces.append((jnp.where(qi >= back, (qi - back) * nq, future), nq))
    diag_pieces.append((jnp.where(qi * Q_BLOCK >= PREV_KEYS + FIRST_KEYS, 0, future), n_first))
    causal = jnp.concatenate([tailm_ref[0]] * hg, axis=1)
    init = (jnp.full((1, GQ), NEG, F32), jnp.zeros((HEAD_DIM + ONES_ROWS, GQ), F32))

    def load_carries():
        return tuple((m_ref[g, 0:1, :], acc_ref[g]) for g in groups)

    def store_carries(carries):
        for g, (m, acc) in enumerate(carries):
            m_ref[g, 0:1, :] = m
            acc_ref[g] = acc

    store_carries(sel_steps(diag_pieces, qaug, causal, tuple(init for _ in groups)))

    blk_per_chunk = FAR_CHUNK // SEL_BLOCK
    wanted = picked[0]
    for g in groups[1:]:
        wanted = wanted + picked[g]
    for c in range(n_blk // blk_per_chunk):
        hit = jnp.sum(wanted[c * blk_per_chunk:(c + 1) * blk_per_chunk, :])
        need_ref[c] = (hit > 0.0).astype(jnp.int32)

    def far_chunk(c, carry):
        @pl.when(need_ref[c] > 0)
        def _():
            store_carries(sel_steps([(c * (FAR_CHUNK // LANES), FAR_CHUNK // LANES)], qfar, None, load_carries()))
        return carry

    far_keys = jnp.maximum(qi * Q_BLOCK - PREV_KEYS, 0)
    lax.fori_loop(0, (far_keys + FAR_CHUNK - 1) // FAR_CHUNK, far_chunk, 0)
    o_slc = [acc[0:HEAD_DIM] * (1.0 / acc[HEAD_DIM:HEAD_DIM + 1]) for _, acc in load_carries()]

    for g in groups:
        for pr in range(hg // 2):
            outs = []
            for h in (2 * pr, 2 * pr + 1):
                hh = g * hg + h
                lanes = slice(h * Q_BLOCK, (h + 1) * Q_BLOCK)
                out = gates[3 * hh:3 * hh + 1, :] * o_cmp[g][:, lanes]
                out = out + gates[3 * hh + 1:3 * hh + 2, :] * o_slc[g][:, lanes]
                out = out + gates[3 * hh + 2:3 * hh + 3, :] * o_win[g][:, lanes]
                outs.append(out)
            pair = jnp.concatenate(outs, axis=0)
            o_ref[:, (g * 2 + pr) * LANES:(g * 2 + pr + 1) * LANES] = pair.T.astype(BF16)


def _nsa(qt, gt, ks, kw, vst, vwt, kc, vct, tables):
    b, _, t = qt.shape
    n_blk = t // SEL_BLOCK
    n_rows = t // CMP_STRIDE
    assert SEL_CHUNK == Q_BLOCK and PREV_KEYS % Q_BLOCK == 0 and t % FAR_CHUNK == 0
    assert t >= PREV_KEYS + 2 * Q_BLOCK
    assert N_SEL <= n_blk <= HEAD_DIM and n_blk % 16 == 0
    seq_spec = pl.BlockSpec((None, t, LANES), lambda i, j: (i, 0, 0))
    chunked = pl.BlockSpec((None, t // LANES, KV_WIDTH, LANES), lambda i, j: (i, 0, 0, 0))
    return pl.pallas_call(
        functools.partial(_nsa_kernel, t=t, n_blk=n_blk),
        grid=(b, t // Q_BLOCK),
        in_specs=[pl.BlockSpec((None, NSA_WIDTH, Q_BLOCK), lambda i, j: (i, 0, j)),
                  pl.BlockSpec((None, GATE_ROWS, Q_BLOCK), lambda i, j: (i, 0, j)),
                  seq_spec, seq_spec, chunked, chunked,
                  pl.BlockSpec((None, KV_GROUPS, n_rows, LANES), lambda i, j: (i, 0, 0, 0)),
                  pl.BlockSpec((None, KV_GROUPS, HEAD_DIM, n_rows), lambda i, j: (i, 0, 0, 0))]
                 + [_resident(a.shape) for a in tables],
        out_specs=pl.BlockSpec((None, Q_BLOCK, NSA_WIDTH), lambda i, j: (i, j, 0)),
        out_shape=jax.ShapeDtypeStruct((b, t, NSA_WIDTH), BF16),
        scratch_shapes=[pltpu.VMEM((KV_GROUPS, t, 2 * LANES), BF16),
                        pltpu.VMEM((KV_GROUPS, t + WINDOW, LANES), BF16),
                        pltpu.VMEM((KV_GROUPS, 8, GQ), F32),
                        pltpu.VMEM((KV_GROUPS, HEAD_DIM + ONES_ROWS, GQ), F32),
                        pltpu.SMEM((t // FAR_CHUNK,), jnp.int32)],
        compiler_params=pltpu.CompilerParams(
            dimension_semantics=("arbitrary", "arbitrary"), vmem_limit_bytes=VMEM_LIMIT),
        name="nsa",
    )(qt, gt, ks, kw, vst, vwt, kc, vct, *tables)


def _mix_ffn_kernel(x_ref, o_ref, u_ref, halo_ref, wp_ref, sc_ref, wo_ref, g2_ref, wg_ref, wu_ref, wd_ref,
                    *rest, tm, final):
    if final:
        fn_ref, y_ref, mix_ref, h_ref, acc_ref = rest
    else:
        y_ref, mix_ref, h_ref, acc_ref = rest
    ti = pl.program_id(1)
    mix_ref[:, 0:NSA_WIDTH] = o_ref[...]
    halo = jnp.where(ti > 0, halo_ref[...], 0.0)
    pos = ti * tm + lax.broadcasted_iota(jnp.int32, (tm, 1), 0)
    for gi, w in enumerate(POOL_WIDTHS):
        lanes = slice(gi * POOL_GROUP, (gi + 1) * POOL_GROUP)
        u = u_ref[:, lanes]
        ext = jnp.concatenate([halo[:, lanes], u], axis=0)
        run = ext
        step = 1
        while step < w:
            run = run + pltpu.roll(run, step, axis=0)
            step *= 2
        cnt = jnp.minimum(pos + 1, w).astype(F32)
        y = run[POOL_HALO:, :] / cnt - u
        yo = _dot(y.astype(BF16), wp_ref[gi]) * sc_ref[:, lanes]
        mix_ref[:, NSA_WIDTH + gi * POOL_GROUP:NSA_WIDTH + (gi + 1) * POOL_GROUP] = yo.astype(BF16)
    x2 = x_ref[...] + _dot(mix_ref[...], wo_ref[...])
    y = _swiglu_residual(x2, g2_ref, wg_ref, wu_ref, wd_ref, h_ref, acc_ref)
    if final:
        y = _rms(y, fn_ref[...])
    y_ref[...] = y


def _mix_ffn(x3, layer, o_nsa, u, wp, sc, wo, g2, wg, wu, wd, final_g=None, tm=512):
    b, t, d = x3.shape
    per = tm // POOL_HALO
    final = final_g is not None
    tail_specs = [_resident((1, d))] if final else []
    tail_args = [final_g] if final else []
    return pl.pallas_call(
        functools.partial(_mix_ffn_kernel, tm=tm, final=final),
        grid=(b, t // tm),
        in_specs=[pl.BlockSpec((None, tm, d), lambda i, j: (i, j, 0)),
                  pl.BlockSpec((None, tm, NSA_WIDTH), lambda i, j: (i, j, 0)),
                  pl.BlockSpec((None, tm, POOL_CH), lambda i, j: (i, j, 0)),
                  pl.BlockSpec((None, POOL_HALO, POOL_CH), lambda i, j: (i, jnp.maximum(j * per - 1, 0), 0)),
                  _resident(wp.shape[1:], layer), _resident(sc.shape), _resident(wo.shape[1:], layer),
                  _resident((1, d)), _resident(wg.shape[1:], layer), _resident(wu.shape[1:], layer),
                  _resident(wd.shape[1:], layer)] + tail_specs,
        out_specs=pl.BlockSpec((None, tm, d), lambda i, j: (i, j, 0)),
        out_shape=jax.ShapeDtypeStruct((b, t, d), F32),
        scratch_shapes=[pltpu.VMEM((tm, NSA_WIDTH + POOL_CH), BF16),
                        pltpu.VMEM((tm, d), BF16), pltpu.VMEM((tm, d), F32)],
        compiler_params=pltpu.CompilerParams(
            dimension_semantics=("parallel", "parallel"), vmem_limit_bytes=VMEM_LIMIT),
        name="mix_ffn_final" if final else "mix_ffn",
    )(x3, o_nsa, u, u, wp, sc, wo, g2, wg, wu, wd, *tail_args)


def kernel(x, ffn1_norm, ffn1_wg, ffn1_wu, ffn1_wd, mix_norm, w_in, cmp_pe_k, cmp_wk1, cmp_wk2,
           cmp_pe_v, cmp_wv1, cmp_wv2, pool_w, pool_scale, w_out, ffn2_norm, ffn2_wg, ffn2_wu,
           ffn2_wd, final_norm):
    b, t, d = x.shape
    depth = ffn1_norm.shape[0]
    tables = _nsa_tables(t)
    f1g, f1u, f1d, f2g, f2u, f2d, wp, wo = (
        a.astype(BF16) for a in (ffn1_wg, ffn1_wu, ffn1_wd, ffn2_wg, ffn2_wu, ffn2_wd, pool_w, w_out))
    for l in range(depth):
        x, kcin, vcin, ks, kw, u, qt, vst, vwt, gt = _ffn_proj(
            x, l, ffn1_norm[l][None], f1g, f1u, f1d, mix_norm[l][None], *_split_w_in(w_in[l]))
        kc, vct = _compress(kcin, vcin,
                            *_compress_weights(cmp_pe_k[l], cmp_wk1[l], cmp_wk2[l], False),
                            *_compress_weights(cmp_pe_v[l], cmp_wv1[l], cmp_wv2[l], True))
        o_nsa = _nsa(qt, gt, ks, kw, vst, vwt, kc, vct, tables)
        x = _mix_ffn(x, l, o_nsa, u, wp, pool_scale[l][None], wo, ffn2_norm[l][None], f2g, f2u, f2d,
                     final_g=final_norm[None] if l == depth - 1 else None)
    return x
```

```python
import functools

import numpy as np
import jax
import jax.numpy as jnp
from jax import lax
from jax.experimental import pallas as pl
from jax.experimental.pallas import tpu as pltpu

HEAD_DIM = 64
NSA_HEADS = 8
KV_GROUPS = 2
HEADS_PER_GROUP = NSA_HEADS // KV_GROUPS
NSA_WIDTH = NSA_HEADS * HEAD_DIM
KV_WIDTH = KV_GROUPS * HEAD_DIM
CMP_BLOCK = 32
CMP_STRIDE = 16
CMP_HIDDEN = 128
SEL_BLOCK = 64
N_SEL = 8
WINDOW = 512
Q_BLOCK = 256
POOL_WIDTHS = (2, 4, 8, 16)
POOL_GROUP = 128
POOL_CH = POOL_GROUP * len(POOL_WIDTHS)
N_GATES = 3 * NSA_HEADS
GATE_ROWS = 32
EPS = 1e-6

LANES = 128
GQ = HEADS_PER_GROUP * Q_BLOCK
PREV_KEYS = 512
FAR_CHUNK = 512
SLAB_HEADS = 2
FIRST_KEYS = 128
WIN_KEYS = WINDOW + Q_BLOCK
ONES_ROWS = 16
POOL_HALO = 16
FF_CHUNK = 256
LOG2E = 1.4426950408889634
NEG = -1e30
VMEM_LIMIT = 56 * 1024 * 1024

F32 = jnp.float32
BF16 = jnp.bfloat16


def _slopes():
    n = NSA_HEADS
    return (2.0 ** (-8.0 * np.arange(1, n + 1) / n)).astype(np.float32)


def _dot(a, b):
    return jnp.dot(a, b, preferred_element_type=F32)


def _dot_nt(a, b):
    return lax.dot_general(a, b, (((1,), (1,)), ((), ())), preferred_element_type=F32)


def _rms(x, g):
    return x * lax.rsqrt(jnp.mean(x * x, axis=-1, keepdims=True) + EPS) * g


def _resident(shape, layer=None):
    nd = len(shape)
    if layer is None:
        return pl.BlockSpec(shape, lambda *_: (0,) * nd, pipeline_mode=pl.Buffered(1))
    return pl.BlockSpec((None,) + tuple(shape), lambda *_: (layer,) + (0,) * nd, pipeline_mode=pl.Buffered(1))


def _swiglu_residual(x, g_ref, wg_ref, wu_ref, wd_ref, h_ref, acc_ref):
    h_ref[...] = _rms(x, g_ref[...]).astype(BF16)
    for c in range(wg_ref.shape[1] // FF_CHUNK):
        cols = slice(c * FF_CHUNK, (c + 1) * FF_CHUNK)
        h = h_ref[...]
        gate = _dot(h, wg_ref[:, cols])
        up = _dot(h, wu_ref[:, cols])
        a = (gate * jax.nn.sigmoid(gate) * up).astype(BF16)
        part = _dot(a, wd_ref[cols, :])
        if c == 0:
            acc_ref[...] = part
        else:
            acc_ref[...] += part
    return x + 0.5 * acc_ref[...]


_TOK_KC = (0, KV_WIDTH)
_TOK_VC = (KV_WIDTH, 2 * KV_WIDTH)
_TOK_KS = (2 * KV_WIDTH, 3 * KV_WIDTH)
_TOK_KW = (3 * KV_WIDTH, 4 * KV_WIDTH)
_TOK_U = (4 * KV_WIDTH, 4 * KV_WIDTH + POOL_CH)
_FEAT_Q = (0, NSA_WIDTH)
_FEAT_VS = (NSA_WIDTH, NSA_WIDTH + KV_WIDTH)
_FEAT_VW = (NSA_WIDTH + KV_WIDTH, NSA_WIDTH + 2 * KV_WIDTH)
_FEAT_GT = (NSA_WIDTH + 2 * KV_WIDTH, NSA_WIDTH + 2 * KV_WIDTH + GATE_ROWS)


def _ffn_proj_kernel(x_ref, g1_ref, wg_ref, wu_ref, wd_ref, g_ref, wt_ref, wf_ref,
                     x1_ref, kcin_ref, vcin_ref, ks_ref, kw_ref, u_ref,
                     qt_ref, vst_ref, vwt_ref, gt_ref, h_ref, acc_ref, *, tm):
    x1 = _swiglu_residual(x_ref[...], g1_ref, wg_ref, wu_ref, wd_ref, h_ref, acc_ref)
    x1_ref[...] = x1
    h = _rms(x1, g_ref[...]).astype(BF16)

    def tok(lo_hi):
        return _dot(h, wt_ref[:, lo_hi[0]:lo_hi[1]])

    kcin_ref[...] = tok(_TOK_KC)
    vcin_ref[...] = tok(_TOK_VC)
    ks_ref[...] = tok(_TOK_KS).astype(BF16)
    kw_ref[...] = tok(_TOK_KW).astype(BF16)
    u_ref[...] = tok(_TOK_U)
    zt = _dot_nt(wf_ref[...], h)
    qt_ref[...] = (zt[_FEAT_Q[0]:_FEAT_Q[1]] * (HEAD_DIM ** -0.5 * LOG2E)).astype(BF16)
    gt_ref[...] = zt[_FEAT_GT[0]:_FEAT_GT[1]]
    for c in range(tm // LANES):
        cols = slice(c * LANES, (c + 1) * LANES)
        vst_ref[c] = zt[_FEAT_VS[0]:_FEAT_VS[1], cols].astype(BF16)
        vwt_ref[c] = zt[_FEAT_VW[0]:_FEAT_VW[1], cols].astype(BF16)


def _ffn_proj(x3, layer, g1, wg, wu, wd, g, w_tok, w_feat, tm=512):
    b, t, d = x3.shape
    nck = tm // LANES
    tok = lambda wd: pl.BlockSpec((None, tm, wd), lambda i, j: (i, j, 0))
    chunked = pl.BlockSpec((None, nck, KV_WIDTH, LANES), lambda i, j: (i, j, 0, 0))
    return pl.pallas_call(
        functools.partial(_ffn_proj_kernel, tm=tm),
        grid=(b, t // tm),
        in_specs=[tok(d), _resident((1, d)),
                  _resident(wg.shape[1:], layer), _resident(wu.shape[1:], layer), _resident(wd.shape[1:], layer),
                  _resident((1, d)), _resident(w_tok.shape), _resident(w_feat.shape)],
        out_specs=[tok(d), tok(KV_WIDTH), tok(KV_WIDTH), tok(KV_WIDTH), tok(KV_WIDTH), tok(POOL_CH),
                   pl.BlockSpec((None, NSA_WIDTH, tm), lambda i, j: (i, 0, j)),
                   chunked, chunked,
                   pl.BlockSpec((None, GATE_ROWS, tm), lambda i, j: (i, 0, j))],
        out_shape=[jax.ShapeDtypeStruct((b, t, d), F32),
                   jax.ShapeDtypeStruct((b, t, KV_WIDTH), F32),
                   jax.ShapeDtypeStruct((b, t, KV_WIDTH), F32),
                   jax.ShapeDtypeStruct((b, t, KV_WIDTH), BF16),
                   jax.ShapeDtypeStruct((b, t, KV_WIDTH), BF16),
                   jax.ShapeDtypeStruct((b, t, POOL_CH), F32),
                   jax.ShapeDtypeStruct((b, NSA_WIDTH, t), BF16),
                   jax.ShapeDtypeStruct((b, t // LANES, KV_WIDTH, LANES), BF16),
                   jax.ShapeDtypeStruct((b, t // LANES, KV_WIDTH, LANES), BF16),
                   jax.ShapeDtypeStruct((b, GATE_ROWS, t), F32)],
        scratch_shapes=[pltpu.VMEM((tm, d), BF16), pltpu.VMEM((tm, d), F32)],
        compiler_params=pltpu.CompilerParams(
            dimension_semantics=("parallel", "parallel"), vmem_limit_bytes=VMEM_LIMIT),
        name="ffn_proj",
    )(x3, g1, wg, wu, wd, g, w_tok, w_feat)


def _split_w_in(w):
    q0, kv0 = 0, NSA_WIDTH
    col = lambda i: slice(kv0 + i * KV_WIDTH, kv0 + (i + 1) * KV_WIDTH)
    g0 = kv0 + 6 * KV_WIDTH
    w_tok = jnp.concatenate([w[:, col(0)], w[:, col(1)], w[:, col(2)], w[:, col(4)],
                             w[:, g0 + N_GATES:]], axis=1).astype(BF16)
    pad = jnp.zeros((w.shape[0], GATE_ROWS - N_GATES), w.dtype)
    w_feat = jnp.concatenate([w[:, q0:NSA_WIDTH], w[:, col(3)], w[:, col(5)],
                              w[:, g0:g0 + N_GATES], pad], axis=1).T.astype(BF16)
    return w_tok, w_feat


def _gelu_tanh(x):
    c = np.float32(np.sqrt(2.0 / np.pi))
    return 0.5 * x * (1.0 + jnp.tanh(c * (x + 0.044715 * (x * x * x))))


def _compress_kernel(kcin_ref, vcin_ref, pek_ref, wk1_ref, wk2_ref, pev_ref, wv1_ref, wv2_ref,
                     kc_ref, vct_ref, *, n_rows):
    nl = CMP_STRIDE
    hid2 = KV_GROUPS * CMP_HIDDEN

    def hidden(src_ref, pe_ref, w1_ref):
        acc = jnp.zeros((n_rows + 16, 2 * hid2), F32)
        for l in range(nl):
            xl = src_ref[pl.ds(l, n_rows, stride=nl), :]
            pa = jnp.broadcast_to(pe_ref[l:l + 1, :], (8, KV_WIDTH))
            pb = jnp.broadcast_to(pe_ref[nl + l:nl + l + 1, :], (8, KV_WIDTH))
            lhs = jnp.concatenate([xl, pa, pb], axis=0).astype(BF16)
            acc = acc + _dot(lhs, w1_ref[l])
        sa = acc[0:n_rows, 0:hid2]
        sb = acc[0:n_rows, hid2:2 * hid2]
        bias = acc[n_rows:n_rows + 1, 0:hid2] + acc[n_rows + 8:n_rows + 9, hid2:2 * hid2]
        hcur = sa + pltpu.roll(sb, n_rows - 1, axis=0) + bias
        return _gelu_tanh(hcur).astype(BF16)

    hk = hidden(kcin_ref, pek_ref, wk1_ref)
    kc = _dot(hk, wk2_ref[...])
    for g in range(KV_GROUPS):
        kc_ref[g] = kc[:, g * LANES:(g + 1) * LANES].astype(BF16)
    hv = hidden(vcin_ref, pev_ref, wv1_ref)
    for g in range(KV_GROUPS):
        vct_ref[g] = _dot_nt(wv2_ref[g], hv).astype(BF16)


def _compress(kcin, vcin, pek2, wk1b, wk2b, pev2, wv1b, wv2t):
    b, t, _ = kcin.shape
    n_rows = t // CMP_STRIDE
    seq_spec = pl.BlockSpec((None, t, KV_WIDTH), lambda i: (i, 0, 0))
    return pl.pallas_call(
        functools.partial(_compress_kernel, n_rows=n_rows),
        grid=(b,),
        in_specs=[seq_spec, seq_spec,
                  _resident(pek2.shape), _resident(wk1b.shape), _resident(wk2b.shape),
                  _resident(pev2.shape), _resident(wv1b.shape), _resident(wv2t.shape)],
        out_specs=[pl.BlockSpec((None, KV_GROUPS, n_rows, LANES), lambda i: (i, 0, 0, 0)),
                   pl.BlockSpec((None, KV_GROUPS, HEAD_DIM, n_rows), lambda i: (i, 0, 0, 0))],
        out_shape=[jax.ShapeDtypeStruct((b, KV_GROUPS, n_rows, LANES), BF16),
                   jax.ShapeDtypeStruct((b, KV_GROUPS, HEAD_DIM, n_rows), BF16)],
        compiler_params=pltpu.CompilerParams(
            dimension_semantics=("parallel",), vmem_limit_bytes=VMEM_LIMIT),
        name="compress",
    )(kcin, vcin, pek2, wk1b, wk2b, pev2, wv1b, wv2t)


def _compress_weights(pe, w1, w2, value_layout):
    nl, dh, hid = CMP_STRIDE, HEAD_DIM, CMP_HIDDEN
    w1r = w1.reshape(2, nl, dh, hid)
    z = jnp.zeros((nl, dh, hid), w1.dtype)
    row_g0 = jnp.concatenate([w1r[0], z, w1r[1], z], axis=-1)
    row_g1 = jnp.concatenate([z, w1r[0], z, w1r[1]], axis=-1)
    w1b = jnp.concatenate([row_g0, row_g1], axis=1).astype(BF16)
    pe2 = jnp.concatenate([pe, pe], axis=-1)
    zz = jnp.zeros((hid, dh), w2.dtype)
    if value_layout:
        w2b = jnp.stack([jnp.concatenate([w2, zz], axis=0).T,
                         jnp.concatenate([zz, w2], axis=0).T]).astype(BF16)
    else:
        top = jnp.concatenate([w2, zz, zz, zz], axis=-1)
        bot = jnp.concatenate([zz, zz, w2, zz], axis=-1)
        w2b = jnp.concatenate([top, bot], axis=0).astype(BF16)
    return pe2, w1b, w2b


def _nsa_tables(t):
    slopes = _slopes().reshape(KV_GROUPS, HEADS_PER_GROUP)
    n_rows = t // CMP_STRIDE
    n_blk = t // SEL_BLOCK
    i = np.arange(Q_BLOCK)
    c = np.arange(n_rows)
    d0c = (i[None, :] - (c[:, None] * CMP_STRIDE + CMP_BLOCK - 1)).astype(np.int32)
    tblc = -LOG2E * slopes[:, None, :, None].astype(np.float64) * d0c[None, :, None, :]
    tblc = tblc.astype(np.float32)
    tblc = tblc.reshape(KV_GROUPS, n_rows, GQ)
    e = np.arange(Q_BLOCK)
    wmask = np.stack([np.where(e[:, None] <= i[None, :], np.float32(NEG), np.float32(0)),
                      np.where(e[:, None] > i[None, :], np.float32(NEG), np.float32(0))])
    tailm = np.where(e[:, None] > i[None, :], np.float32(NEG), np.float32(0))
    ci = c[None, :] * CMP_STRIDE
    sj = np.arange(n_blk)[:, None] * SEL_BLOCK
    n_cmp = (t - CMP_BLOCK) // CMP_STRIDE + 1
    ov = ((ci <= sj + SEL_BLOCK - 1) & (ci + CMP_BLOCK - 1 >= sj) & (c[None, :] < n_cmp))
    ov = ov.astype(np.float32)
    pos = np.arange(t)
    ktag = np.zeros((t, 2 * LANES), np.float32)
    ktag[pos, HEAD_DIM + pos // SEL_BLOCK] = 1.0
    ktag[pos, LANES + pos // SEL_BLOCK] = 1.0
    ktag[:, LANES + HEAD_DIM] = pos % SEL_BLOCK
    ktag[:, LANES + HEAD_DIM + 1] = pos % SEL_BLOCK
    wtag = np.zeros((t, LANES), np.float32)
    wtag[:, HEAD_DIM + 1] = wtag[:, HEAD_DIM + 2] = pos // SEL_BLOCK
    wtag[:, HEAD_DIM + 3] = wtag[:, HEAD_DIM + 4] = pos % SEL_BLOCK
    s2 = np.repeat(slopes, Q_BLOCK, axis=1).astype(np.float64) * LOG2E
    s2_hi = s2.astype(BF16).astype(np.float64)
    b2 = s2 * SEL_BLOCK
    b2_hi = b2.astype(BF16).astype(np.float64)
    qflag = np.zeros((KV_GROUPS, HEAD_DIM, GQ), np.float32)
    qflag[:, 0, :] = NEG
    qflag[:, 1, :], qflag[:, 2, :] = b2_hi, b2 - b2_hi
    qflag[:, 3, :], qflag[:, 4, :] = s2_hi, s2 - s2_hi
    qslope = np.zeros((KV_GROUPS, HEAD_DIM, GQ), np.float32)
    qslope[:, 0, :] = s2_hi
    qslope[:, 1, :] = s2 - s2_hi
    return (jnp.asarray(tblc), jnp.asarray(wmask), jnp.asarray(tailm),
            jnp.asarray(ov, dtype=BF16), jnp.asarray(ktag, dtype=BF16), jnp.asarray(wtag, dtype=BF16),
            jnp.asarray(qflag, dtype=BF16), jnp.asarray(qslope, dtype=BF16))


def _nsa_kernel(qt_ref, gt_ref, ks_ref, kw_ref, vst_ref, vwt_ref, kc_ref, vct_ref,
                tblc_ref, wmask_ref, tailm_ref, ov_ref, ktag_ref, wtag_ref, qflag_ref, qslope_ref,
                o_ref,
                kaug_ref, kwaug_ref, m_ref, acc_ref, need_ref, *, t, n_blk):
    qi = pl.program_id(1)
    slopes = _slopes()
    hg = HEADS_PER_GROUP
    prep_rows = WINDOW

    @pl.when(qi == 0)
    def _prep():
        low = lax.broadcasted_iota(jnp.int32, (prep_rows, LANES), 1) < HEAD_DIM
        flag = jnp.where(lax.broadcasted_iota(jnp.int32, (WINDOW, LANES), 1) == HEAD_DIM, 1.0, 0.0)
        for g in range(KV_GROUPS):
            kwaug_ref[g, 0:WINDOW, :] = flag.astype(BF16)

        def body(c, carry):
            r0 = pl.multiple_of(c * prep_rows, prep_rows)
            rows = pl.ds(r0, prep_rows)
            rows_w = pl.ds(r0 + WINDOW, prep_rows)
            tag = ktag_ref[rows, 0:LANES].astype(F32)
            wtag = wtag_ref[rows, :].astype(F32)
            ks = ks_ref[rows, :].astype(F32)
            kw = kw_ref[rows, :].astype(F32)
            ks_sw = pltpu.roll(ks, HEAD_DIM, axis=1)
            kw_sw = pltpu.roll(kw, HEAD_DIM, axis=1)
            for g in range(KV_GROUPS):
                kaug_ref[g, rows, 0:LANES] = jnp.where(low, ks if g == 0 else ks_sw, tag).astype(BF16)
                kaug_ref[g, rows, LANES:2 * LANES] = ktag_ref[rows, LANES:2 * LANES]
                kwaug_ref[g, rows_w, :] = jnp.where(low, kw if g == 0 else kw_sw, wtag).astype(BF16)
            return carry

        lax.fori_loop(0, t // prep_rows, body, 0)

    gates = jax.nn.sigmoid(gt_ref[...])
    bi = lax.broadcasted_iota(jnp.int32, (n_blk, Q_BLOCK), 0)
    lq = lax.broadcasted_iota(jnp.int32, (n_blk, Q_BLOCK), 1)
    cur = (Q_BLOCK // SEL_BLOCK) * qi + lax.shift_right_logical(lq, SEL_BLOCK.bit_length() - 1)
    valid = bi <= cur
    near = (bi == 0) | (bi >= cur - 1)
    always = valid & (near | (cur < N_SEL))
    free = valid & jnp.logical_not(near)
    far = (bi >= FIRST_KEYS // SEL_BLOCK) & (bi < (qi * Q_BLOCK - PREV_KEYS) // SEL_BLOCK)
    rel_blk = (bi - cur).astype(F32)

    head_slabs = [slice(h * Q_BLOCK, (h + SLAB_HEADS) * Q_BLOCK) for h in range(0, hg, SLAB_HEADS)]

    q_pos = qi * Q_BLOCK + lax.broadcasted_iota(jnp.int32, (1, Q_BLOCK), 1)
    cmp_last = jnp.concatenate([lax.shift_right_arithmetic(q_pos - (CMP_BLOCK - 1), CMP_STRIDE.bit_length() - 1)] * hg,
                               axis=1)
    cmp_row = lax.broadcasted_iota(jnp.int32, (t // CMP_STRIDE, SLAB_HEADS * Q_BLOCK), 0)

    def with_ones(vt):
        first = lax.broadcasted_iota(jnp.int32, (ONES_ROWS, vt.shape[1]), 0) == 0
        return jnp.concatenate([vt, jnp.where(first, 1.0, 0.0).astype(BF16)], axis=0)

    def queries(g):
        heads = lambda ref: jnp.concatenate([ref[(g * hg + h) * HEAD_DIM:(g * hg + h + 1) * HEAD_DIM, :]
                                             for h in range(hg)], axis=1)
        qt = heads(qt_ref)
        return qt, jnp.concatenate([qt, qflag_ref[g]], axis=0)

    def select(g, qt, s):
        o_parts, psum = [], None
        for lanes in head_slabs:
            sh = s[:, lanes] + tblc_ref[g, :, lanes]
            sh = jnp.where(cmp_row <= cmp_last[:, lanes], sh, NEG)
            m = jnp.max(sh, axis=0, keepdims=True)
            m = jnp.where(m < 0.5 * NEG, 0.0, m)
            p = jnp.exp2(sh - m)
            l = jnp.sum(p, axis=0, keepdims=True)
            pn = p * (1.0 / jnp.maximum(l, 1e-30))
            o_parts.append(_dot(vct_ref[g], pn.astype(BF16)))
            for h in range(SLAB_HEADS):
                ph = pn[:, h * Q_BLOCK:(h + 1) * Q_BLOCK]
                psum = ph if psum is None else psum + ph
        o_cmp = jnp.concatenate(o_parts, axis=1)

        p_hi = psum.astype(BF16)
        p_lo = (psum - p_hi.astype(F32)).astype(BF16)
        imp = _dot(ov_ref[...], p_hi) + _dot(ov_ref[...], p_lo)
        val = jnp.where(free, imp, -1.0)
        sel = jnp.zeros((n_blk, Q_BLOCK), F32)
        for _ in range(N_SEL - 3):
            top = jnp.max(val, axis=0, keepdims=True)
            idx = jnp.min(jnp.where(val == top, bi, n_blk), axis=0, keepdims=True)
            pick = bi == idx
            sel = jnp.where(pick, 1.0, sel)
            val = jnp.where(pick, -2.0, val)
        chosen = jnp.where(always, 1.0, jnp.where(free, sel, 0.0)) > 0.0

        def tagged(blocks):
            bias = jnp.concatenate(
                [(float(slopes[g * hg + h]) * SEL_BLOCK * LOG2E) * rel_blk for h in range(hg)], axis=1)
            on = jnp.concatenate([blocks] * hg, axis=1)
            hi = bias.astype(BF16)
            lo = (bias - hi.astype(F32)).astype(BF16)
            pad = [jnp.zeros((HEAD_DIM - n_blk, GQ), BF16)] if n_blk < HEAD_DIM else []
            parts = [qt, jnp.where(on, hi, NEG)] + pad + [jnp.where(on, lo, 0.0)] + pad + [qslope_ref[g]]
            return jnp.concatenate(parts, axis=0)

        return o_cmp, tagged(chosen), tagged(chosen & far), jnp.where(chosen & far, 1.0, 0.0)

    def sel_keys(g, pieces):
        return jnp.concatenate([kaug_ref[g, pl.ds(pl.multiple_of(s0 * LANES, LANES), n * LANES), :]
                                for s0, n in pieces], axis=0)

    def sel_values(g, pieces):
        return with_ones(jnp.concatenate([vst_ref[s0 + k, g * HEAD_DIM:(g + 1) * HEAD_DIM, :]
                                          for s0, n in pieces for k in range(n)], axis=1))

    def sel_update(s, vt, carry):
        m, acc = carry
        ms, accs = [], []
        for lanes in head_slabs:
            sh, mh = s[:, lanes], m[:, lanes]
            m_new = jnp.maximum(mh, jnp.max(sh, axis=0, keepdims=True))
            alpha = jnp.exp2(mh - m_new)
            p = jnp.exp2(sh - m_new).astype(BF16)
            ms.append(m_new)
            accs.append(acc[:, lanes] * alpha + _dot(vt, p))
        return jnp.concatenate(ms, axis=1), jnp.concatenate(accs, axis=1)

    def sel_steps(pieces, q, mask, carries):
        scores = [_dot(sel_keys(g, pieces), q[g]) for g in groups]
        if mask is not None:
            scores = [jnp.concatenate([sc[0:mask.shape[0]] + mask, sc[mask.shape[0]:]], axis=0) for sc in scores]
        return tuple(sel_update(scores[g], sel_values(g, pieces), carries[g]) for g in groups)

    def window_scores(g, qw):
        rows = pl.ds(pl.multiple_of(qi * Q_BLOCK, Q_BLOCK), WIN_KEYS)
        return _dot(kwaug_ref[g, rows, :], qw)

    def window(g, s):
        back = WINDOW // LANES
        ahead = Q_BLOCK // LANES
        vt = jnp.concatenate([vwt_ref[jnp.maximum(qi * ahead - back + k, 0), g * HEAD_DIM:(g + 1) * HEAD_DIM, :]
                              for k in range(back + ahead)], axis=1)
        vt = with_ones(vt)
        outs = []
        too_old = jnp.concatenate([wmask_ref[0]] * SLAB_HEADS, axis=1)
        ahead_of = jnp.concatenate([wmask_ref[1]] * SLAB_HEADS, axis=1)
        for lanes in head_slabs:
            sh = jnp.concatenate([s[0:Q_BLOCK, lanes] + too_old, s[Q_BLOCK:WINDOW, lanes],
                                  s[WINDOW:WIN_KEYS, lanes] + ahead_of], axis=0)
            m = jnp.max(sh, axis=0, keepdims=True)
            p = jnp.exp2(sh - m).astype(BF16)
            acc = _dot(vt, p)
            outs.append(acc[0:HEAD_DIM] * (1.0 / acc[HEAD_DIM:HEAD_DIM + 1]))
        return jnp.concatenate(outs, axis=1)

    groups = tuple(range(KV_GROUPS))
    qt, qw = zip(*[queries(g) for g in groups])
    s_cmp = [_dot(kc_ref[g], qw[g]) for g in groups]
    s_win = [window_scores(g, qw[g]) for g in groups]
    sel_out, o_win = [], []
    for g in groups:
        sel_out.append(select(g, qt[g], s_cmp[g]))
        o_win.append(window(g, s_win[g]))
    o_cmp, qaug, qfar, picked = zip(*sel_out)

    nq, n_first = Q_BLOCK // LANES, FIRST_KEYS // LANES
    future = (qi + 1) * nq
    diag_pieces = [(qi * nq, nq)]
    for back in range(1, PREV_KEYS // Q_BLOCK + 1):
        diag_pieces.append((jnp.where(qi >= back, (qi - back) * nq, future), nq))
    diag_pieces.append((jnp.where(qi * Q_BLOCK >= PREV_KEYS + FIRST_KEYS, 0, future), n_first))
    causal = jnp.concatenate([tailm_ref[...]] * hg, axis=1)
    init = (jnp.full((1, GQ), NEG, F32), jnp.zeros((HEAD_DIM + ONES_ROWS, GQ), F32))

    def load_carries():
        return tuple((m_ref[g, 0:1, :], acc_ref[g]) for g in groups)

    def store_carries(carries):
        for g, (m, acc) in enumerate(carries):
            m_ref[g, 0:1, :] = m
            acc_ref[g] = acc

    store_carries(sel_steps(diag_pieces, qaug, causal, tuple(init for _ in groups)))

    blk_per_chunk = FAR_CHUNK // SEL_BLOCK
    wanted = picked[0]
    for g in groups[1:]:
        wanted = wanted + picked[g]
    for c in range(n_blk // blk_per_chunk):
        hit = jnp.sum(wanted[c * blk_per_chunk:(c + 1) * blk_per_chunk, :])
        need_ref[c] = (hit > 0.0).astype(jnp.int32)

    def far_chunk(c, carry):
        @pl.when(need_ref[c] > 0)
        def _():
            store_carries(sel_steps([(c * (FAR_CHUNK // LANES), FAR_CHUNK // LANES)], qfar, None, load_carries()))
        return carry

    far_keys = jnp.maximum(qi * Q_BLOCK - PREV_KEYS, 0)
    lax.fori_loop(0, (far_keys + FAR_CHUNK - 1) // FAR_CHUNK, far_chunk, 0)
    o_slc = [acc[0:HEAD_DIM] * (1.0 / acc[HEAD_DIM:HEAD_DIM + 1]) for _, acc in load_carries()]

    for g in groups:
        for pr in range(hg // 2):
            outs = []
            for h in (2 * pr, 2 * pr + 1):
                hh = g * hg + h
                lanes = slice(h * Q_BLOCK, (h + 1) * Q_BLOCK)
                out = gates[3 * hh:3 * hh + 1, :] * o_cmp[g][:, lanes]
                out = out + gates[3 * hh + 1:3 * hh + 2, :] * o_slc[g][:, lanes]
                out = out + gates[3 * hh + 2:3 * hh + 3, :] * o_win[g][:, lanes]
                outs.append(out)
            pair = jnp.concatenate(outs, axis=0)
            o_ref[:, (g * 2 + pr) * LANES:(g * 2 + pr + 1) * LANES] = pair.T.astype(BF16)


def _nsa(qt, gt, ks, kw, vst, vwt, kc, vct, tables):
    b, _, t = qt.shape
    n_blk = t // SEL_BLOCK
    n_rows = t // CMP_STRIDE
    assert PREV_KEYS % Q_BLOCK == 0 and t % FAR_CHUNK == 0 and t % Q_BLOCK == 0
    assert t >= PREV_KEYS + 2 * Q_BLOCK
    assert N_SEL <= n_blk <= HEAD_DIM and n_blk % 16 == 0
    seq_spec = pl.BlockSpec((None, t, LANES), lambda i, j: (i, 0, 0))
    chunked = pl.BlockSpec((None, t // LANES, KV_WIDTH, LANES), lambda i, j: (i, 0, 0, 0))
    return pl.pallas_call(
        functools.partial(_nsa_kernel, t=t, n_blk=n_blk),
        grid=(b, t // Q_BLOCK),
        in_specs=[pl.BlockSpec((None, NSA_WIDTH, Q_BLOCK), lambda i, j: (i, 0, j)),
                  pl.BlockSpec((None, GATE_ROWS, Q_BLOCK), lambda i, j: (i, 0, j)),
                  seq_spec, seq_spec, chunked, chunked,
                  pl.BlockSpec((None, KV_GROUPS, n_rows, LANES), lambda i, j: (i, 0, 0, 0)),
                  pl.BlockSpec((None, KV_GROUPS, HEAD_DIM, n_rows), lambda i, j: (i, 0, 0, 0))]
                 + [_resident(a.shape) for a in tables],
        out_specs=pl.BlockSpec((None, Q_BLOCK, NSA_WIDTH), lambda i, j: (i, j, 0)),
        out_shape=jax.ShapeDtypeStruct((b, t, NSA_WIDTH), BF16),
        scratch_shapes=[pltpu.VMEM((KV_GROUPS, t, 2 * LANES), BF16),
                        pltpu.VMEM((KV_GROUPS, t + WINDOW, LANES), BF16),
                        pltpu.VMEM((KV_GROUPS, 8, GQ), F32),
                        pltpu.VMEM((KV_GROUPS, HEAD_DIM + ONES_ROWS, GQ), F32),
                        pltpu.SMEM((t // FAR_CHUNK,), jnp.int32)],
        compiler_params=pltpu.CompilerParams(
            dimension_semantics=("arbitrary", "arbitrary"), vmem_limit_bytes=VMEM_LIMIT),
        name="nsa",
    )(qt, gt, ks, kw, vst, vwt, kc, vct, *tables)


def _mix_ffn_kernel(x_ref, o_ref, u_ref, halo_ref, wp_ref, sc_ref, wo_ref, g2_ref, wg_ref, wu_ref, wd_ref,
                    *rest, tm, final):
    if final:
        fn_ref, y_ref, mix_ref, h_ref, acc_ref = rest
    else:
        y_ref, mix_ref, h_ref, acc_ref = rest
    ti = pl.program_id(1)
    mix_ref[:, 0:NSA_WIDTH] = o_ref[...]
    halo = jnp.where(ti > 0, halo_ref[...], 0.0)
    pos = ti * tm + lax.broadcasted_iota(jnp.int32, (tm, 1), 0)
    for gi, w in enumerate(POOL_WIDTHS):
        lanes = slice(gi * POOL_GROUP, (gi + 1) * POOL_GROUP)
        u = u_ref[:, lanes]
        ext = jnp.concatenate([halo[:, lanes], u], axis=0)
        run = ext
        step = 1
        while step < w:
            run = run + pltpu.roll(run, step, axis=0)
            step *= 2
        cnt = jnp.minimum(pos + 1, w).astype(F32)
        y = run[POOL_HALO:, :] / cnt - u
        yo = _dot(y.astype(BF16), wp_ref[gi]) * sc_ref[:, lanes]
        mix_ref[:, NSA_WIDTH + gi * POOL_GROUP:NSA_WIDTH + (gi + 1) * POOL_GROUP] = yo.astype(BF16)
    x2 = x_ref[...] + _dot(mix_ref[...], wo_ref[...])
    y = _swiglu_residual(x2, g2_ref, wg_ref, wu_ref, wd_ref, h_ref, acc_ref)
    if final:
        y = _rms(y, fn_ref[...])
    y_ref[...] = y


def _mix_ffn(x3, layer, o_nsa, u, wp, sc, wo, g2, wg, wu, wd, final_g=None, tm=512):
    b, t, d = x3.shape
    per = tm // POOL_HALO
    final = final_g is not None
    tail_specs = [_resident((1, d))] if final else []
    tail_args = [final_g] if final else []
    return pl.pallas_call(
        functools.partial(_mix_ffn_kernel, tm=tm, final=final),
        grid=(b, t // tm),
        in_specs=[pl.BlockSpec((None, tm, d), lambda i, j: (i, j, 0)),
                  pl.BlockSpec((None, tm, NSA_WIDTH), lambda i, j: (i, j, 0)),
                  pl.BlockSpec((None, tm, POOL_CH), lambda i, j: (i, j, 0)),
                  pl.BlockSpec((None, POOL_HALO, POOL_CH), lambda i, j: (i, jnp.maximum(j * per - 1, 0), 0)),
                  _resident(wp.shape[1:], layer), _resident(sc.shape), _resident(wo.shape[1:], layer),
                  _resident((1, d)), _resident(wg.shape[1:], layer), _resident(wu.shape[1:], layer),
                  _resident(wd.shape[1:], layer)] + tail_specs,
        out_specs=pl.BlockSpec((None, tm, d), lambda i, j: (i, j, 0)),
        out_shape=jax.ShapeDtypeStruct((b, t, d), F32),
        scratch_shapes=[pltpu.VMEM((tm, NSA_WIDTH + POOL_CH), BF16),
                        pltpu.VMEM((tm, d), BF16), pltpu.VMEM((tm, d), F32)],
        compiler_params=pltpu.CompilerParams(
            dimension_semantics=("parallel", "parallel"), vmem_limit_bytes=VMEM_LIMIT),
        name="mix_ffn_final" if final else "mix_ffn",
    )(x3, o_nsa, u, u, wp, sc, wo, g2, wg, wu, wd, *tail_args)


def kernel(x, ffn1_norm, ffn1_wg, ffn1_wu, ffn1_wd, mix_norm, w_in, cmp_pe_k, cmp_wk1, cmp_wk2,
           cmp_pe_v, cmp_wv1, cmp_wv2, pool_w, pool_scale, w_out, ffn2_norm, ffn2_wg, ffn2_wu,
           ffn2_wd, final_norm):
    b, t, d = x.shape
    depth = ffn1_norm.shape[0]
    tables = _nsa_tables(t)
    f1g, f1u, f1d, f2g, f2u, f2d, wp, wo = (
        a.astype(BF16) for a in (ffn1_wg, ffn1_wu, ffn1_wd, ffn2_wg, ffn2_wu, ffn2_wd, pool_w, w_out))
    for l in range(depth):
        x, kcin, vcin, ks, kw, u, qt, vst, vwt, gt = _ffn_proj(
            x, l, ffn1_norm[l][None], f1g, f1u, f1d, mix_norm[l][None], *_split_w_in(w_in[l]))
        kc, vct = _compress(kcin, vcin,
                            *_compress_weights(cmp_pe_k[l], cmp_wk1[l], cmp_wk2[l], False),
                            *_compress_weights(cmp_pe_v[l], cmp_wv1[l], cmp_wv2[l], True))
        o_nsa = _nsa(qt, gt, ks, kw, vst, vwt, kc, vct, tables)
        x = _mix_ffn(x, l, o_nsa, u, wp, pool_scale[l][None], wo, ffn2_norm[l][None], f2g, f2u, f2d,
                     final_g=final_norm[None] if l == depth - 1 else None)
    return x
```

```python
import functools

import numpy as np
import jax
import jax.numpy as jnp
from jax import lax
from jax.experimental import pallas as pl
from jax.experimental.pallas import tpu as pltpu

HEAD_DIM = 64
NSA_HEADS = 8
KV_GROUPS = 2
HEADS_PER_GROUP = NSA_HEADS // KV_GROUPS
NSA_WIDTH = NSA_HEADS * HEAD_DIM
KV_WIDTH = KV_GROUPS * HEAD_DIM
CMP_BLOCK = 32
CMP_STRIDE = 16
CMP_HIDDEN = 128
SEL_BLOCK = 64
N_SEL = 8
WINDOW = 512
Q_BLOCK = 256
POOL_WIDTHS = (2, 4, 8, 16)
POOL_GROUP = 128
POOL_CH = POOL_GROUP * len(POOL_WIDTHS)
N_GATES = 3 * NSA_HEADS
GATE_ROWS = 32
EPS = 1e-6

LANES = 128
GQ = HEADS_PER_GROUP * Q_BLOCK
PREV_KEYS = 512
FAR_CHUNK = 512
SLAB_HEADS = 2
FIRST_KEYS = 128
WIN_KEYS = WINDOW + Q_BLOCK
ONES_ROWS = 16
POOL_HALO = 16
FF_CHUNK = 256
LOG2E = 1.4426950408889634
NEG = -1e30
VMEM_LIMIT = 56 * 1024 * 1024

F32 = jnp.float32
BF16 = jnp.bfloat16


def _slopes():
    n = NSA_HEADS
    return (2.0 ** (-8.0 * np.arange(1, n + 1) / n)).astype(np.float32)


def _dot(a, b):
    return jnp.dot(a, b, preferred_element_type=F32)


def _dot_nt(a, b):
    return lax.dot_general(a, b, (((1,), (1,)), ((), ())), preferred_element_type=F32)


def _rms(x, g):
    return x * lax.rsqrt(jnp.mean(x * x, axis=-1, keepdims=True) + EPS) * g


def _resident(shape, layer=None):
    nd = len(shape)
    if layer is None:
        return pl.BlockSpec(shape, lambda *_: (0,) * nd, pipeline_mode=pl.Buffered(1))
    return pl.BlockSpec((None,) + tuple(shape), lambda *_: (layer,) + (0,) * nd, pipeline_mode=pl.Buffered(1))


def _swiglu_residual(x, g_ref, wg_ref, wu_ref, wd_ref, h_ref, acc_ref):
    h_ref[...] = _rms(x, g_ref[...]).astype(BF16)
    for c in range(wg_ref.shape[1] // FF_CHUNK):
        cols = slice(c * FF_CHUNK, (c + 1) * FF_CHUNK)
        h = h_ref[...]
        gate = _dot(h, wg_ref[:, cols])
        up = _dot(h, wu_ref[:, cols])
        a = (gate * jax.nn.sigmoid(gate) * up).astype(BF16)
        part = _dot(a, wd_ref[cols, :])
        if c == 0:
            acc_ref[...] = part
        else:
            acc_ref[...] += part
    return x + 0.5 * acc_ref[...]


_TOK_KC = (0, KV_WIDTH)
_TOK_VC = (KV_WIDTH, 2 * KV_WIDTH)
_TOK_KS = (2 * KV_WIDTH, 3 * KV_WIDTH)
_TOK_KW = (3 * KV_WIDTH, 4 * KV_WIDTH)
_TOK_U = (4 * KV_WIDTH, 4 * KV_WIDTH + POOL_CH)
_FEAT_Q = (0, NSA_WIDTH)
_FEAT_VS = (NSA_WIDTH, NSA_WIDTH + KV_WIDTH)
_FEAT_VW = (NSA_WIDTH + KV_WIDTH, NSA_WIDTH + 2 * KV_WIDTH)
_FEAT_GT = (NSA_WIDTH + 2 * KV_WIDTH, NSA_WIDTH + 2 * KV_WIDTH + GATE_ROWS)


def _ffn_proj_kernel(x_ref, g1_ref, wg_ref, wu_ref, wd_ref, g_ref, wt_ref, wf_ref,
                     x1_ref, kcin_ref, vcin_ref, ks_ref, kw_ref, u_ref,
                     qt_ref, vst_ref, vwt_ref, gt_ref, h_ref, acc_ref, *, tm):
    x1 = _swiglu_residual(x_ref[...], g1_ref, wg_ref, wu_ref, wd_ref, h_ref, acc_ref)
    x1_ref[...] = x1
    h = _rms(x1, g_ref[...]).astype(BF16)

    def tok(lo_hi):
        return _dot(h, wt_ref[:, lo_hi[0]:lo_hi[1]])

    kcin_ref[...] = tok(_TOK_KC)
    vcin_ref[...] = tok(_TOK_VC)
    ks_ref[...] = tok(_TOK_KS).astype(BF16)
    kw_ref[...] = tok(_TOK_KW).astype(BF16)
    u_ref[...] = tok(_TOK_U)
    zt = _dot_nt(wf_ref[...], h)
    qt_ref[...] = (zt[_FEAT_Q[0]:_FEAT_Q[1]] * (HEAD_DIM ** -0.5 * LOG2E)).astype(BF16)
    gt_ref[...] = zt[_FEAT_GT[0]:_FEAT_GT[1]]
    for c in range(tm // LANES):
        cols = slice(c * LANES, (c + 1) * LANES)
        vst_ref[c] = zt[_FEAT_VS[0]:_FEAT_VS[1], cols].astype(BF16)
        vwt_ref[c] = zt[_FEAT_VW[0]:_FEAT_VW[1], cols].astype(BF16)


def _ffn_proj(x3, layer, g1, wg, wu, wd, g, w_tok, w_feat, tm=512):
    b, t, d = x3.shape
    nck = tm // LANES
    tok = lambda wd: pl.BlockSpec((None, tm, wd), lambda i, j: (i, j, 0))
    chunked = pl.BlockSpec((None, nck, KV_WIDTH, LANES), lambda i, j: (i, j, 0, 0))
    return pl.pallas_call(
        functools.partial(_ffn_proj_kernel, tm=tm),
        grid=(b, t // tm),
        in_specs=[tok(d), _resident((1, d)),
                  _resident(wg.shape[1:], layer), _resident(wu.shape[1:], layer), _resident(wd.shape[1:], layer),
                  _resident((1, d)), _resident(w_tok.shape), _resident(w_feat.shape)],
        out_specs=[tok(d), tok(KV_WIDTH), tok(KV_WIDTH), tok(KV_WIDTH), tok(KV_WIDTH), tok(POOL_CH),
                   pl.BlockSpec((None, NSA_WIDTH, tm), lambda i, j: (i, 0, j)),
                   chunked, chunked,
                   pl.BlockSpec((None, GATE_ROWS, tm), lambda i, j: (i, 0, j))],
        out_shape=[jax.ShapeDtypeStruct((b, t, d), F32),
                   jax.ShapeDtypeStruct((b, t, KV_WIDTH), F32),
                   jax.ShapeDtypeStruct((b, t, KV_WIDTH), F32),
                   jax.ShapeDtypeStruct((b, t, KV_WIDTH), BF16),
                   jax.ShapeDtypeStruct((b, t, KV_WIDTH), BF16),
                   jax.ShapeDtypeStruct((b, t, POOL_CH), F32),
                   jax.ShapeDtypeStruct((b, NSA_WIDTH, t), BF16),
                   jax.ShapeDtypeStruct((b, t // LANES, KV_WIDTH, LANES), BF16),
                   jax.ShapeDtypeStruct((b, t // LANES, KV_WIDTH, LANES), BF16),
                   jax.ShapeDtypeStruct((b, GATE_ROWS, t), F32)],
        scratch_shapes=[pltpu.VMEM((tm, d), BF16), pltpu.VMEM((tm, d), F32)],
        compiler_params=pltpu.CompilerParams(
            dimension_semantics=("parallel", "parallel"), vmem_limit_bytes=VMEM_LIMIT),
        name="ffn_proj",
    )(x3, g1, wg, wu, wd, g, w_tok, w_feat)


def _split_w_in(w):
    q0, kv0 = 0, NSA_WIDTH
    col = lambda i: slice(kv0 + i * KV_WIDTH, kv0 + (i + 1) * KV_WIDTH)
    g0 = kv0 + 6 * KV_WIDTH
    w_tok = jnp.concatenate([w[:, col(0)], w[:, col(1)], w[:, col(2)], w[:, col(4)],
                             w[:, g0 + N_GATES:]], axis=1).astype(BF16)
    pad = jnp.zeros((w.shape[0], GATE_ROWS - N_GATES), w.dtype)
    w_feat = jnp.concatenate([w[:, q0:NSA_WIDTH], w[:, col(3)], w[:, col(5)],
                              w[:, g0:g0 + N_GATES], pad], axis=1).T.astype(BF16)
    return w_tok, w_feat


def _gelu_tanh(x):
    c = np.float32(np.sqrt(2.0 / np.pi))
    return 0.5 * x * (1.0 + jnp.tanh(c * (x + 0.044715 * (x * x * x))))


def _compress_kernel(kcin_ref, vcin_ref, pek_ref, wk1_ref, wk2_ref, pev_ref, wv1_ref, wv2_ref,
                     kc_ref, vct_ref, *, n_rows):
    nl = CMP_STRIDE
    hid2 = KV_GROUPS * CMP_HIDDEN

    def hidden(src_ref, pe_ref, w1_ref):
        acc = jnp.zeros((n_rows + 16, 2 * hid2), F32)
        for l in range(nl):
            xl = src_ref[pl.ds(l, n_rows, stride=nl), :]
            pa = jnp.broadcast_to(pe_ref[l:l + 1, :], (8, KV_WIDTH))
            pb = jnp.broadcast_to(pe_ref[nl + l:nl + l + 1, :], (8, KV_WIDTH))
            lhs = jnp.concatenate([xl, pa, pb], axis=0).astype(BF16)
            acc = acc + _dot(lhs, w1_ref[l])
        sa = acc[0:n_rows, 0:hid2]
        sb = acc[0:n_rows, hid2:2 * hid2]
        bias = acc[n_rows:n_rows + 1, 0:hid2] + acc[n_rows + 8:n_rows + 9, hid2:2 * hid2]
        hcur = sa + pltpu.roll(sb, n_rows - 1, axis=0) + bias
        return _gelu_tanh(hcur).astype(BF16)

    hk = hidden(kcin_ref, pek_ref, wk1_ref)
    kc = _dot(hk, wk2_ref[...])
    for g in range(KV_GROUPS):
        kc_ref[g] = kc[:, g * LANES:(g + 1) * LANES].astype(BF16)
    hv = hidden(vcin_ref, pev_ref, wv1_ref)
    for g in range(KV_GROUPS):
        vct_ref[g] = _dot_nt(wv2_ref[g], hv).astype(BF16)


def _compress_weights(pe, w1, w2, value_layout):
    nl, dh, hid = CMP_STRIDE, HEAD_DIM, CMP_HIDDEN
    w1r = w1.reshape(2, nl, dh, hid)
    z = jnp.zeros((nl, dh, hid), w1.dtype)
    row_g0 = jnp.concatenate([w1r[0], z, w1r[1], z], axis=-1)
    row_g1 = jnp.concatenate([z, w1r[0], z, w1r[1]], axis=-1)
    w1b = jnp.concatenate([row_g0, row_g1], axis=1).astype(BF16)
    pe2 = jnp.concatenate([pe, pe], axis=-1)
    zz = jnp.zeros((hid, dh), w2.dtype)
    if value_layout:
        w2b = jnp.stack([jnp.concatenate([w2, zz], axis=0).T,
                         jnp.concatenate([zz, w2], axis=0).T]).astype(BF16)
    else:
        top = jnp.concatenate([w2, zz, zz, zz], axis=-1)
        bot = jnp.concatenate([zz, zz, w2, zz], axis=-1)
        w2b = jnp.concatenate([top, bot], axis=0).astype(BF16)
    return pe2, w1b, w2b


def _nsa_tables(t):
    slopes = _slopes().reshape(KV_GROUPS, HEADS_PER_GROUP)
    n_rows = t // CMP_STRIDE
    n_blk = t // SEL_BLOCK
    i = np.arange(Q_BLOCK)
    c = np.arange(n_rows)
    d0c = (i[None, :] - (c[:, None] * CMP_STRIDE + CMP_BLOCK - 1)).astype(np.int32)
    tblc = -LOG2E * slopes[:, None, :, None].astype(np.float64) * d0c[None, :, None, :]
    tblc = tblc.astype(np.float32)
    tblc = tblc.reshape(KV_GROUPS, n_rows, GQ)
    e = np.arange(Q_BLOCK)
    wmask = np.stack([np.where(e[:, None] <= i[None, :], np.float32(NEG), np.float32(0)),
                      np.where(e[:, None] > i[None, :], np.float32(NEG), np.float32(0))])
    tailm = np.where(e[:, None] > i[None, :], np.float32(NEG), np.float32(0))
    ci = c[None, :] * CMP_STRIDE
    sj = np.arange(n_blk)[:, None] * SEL_BLOCK
    n_cmp = (t - CMP_BLOCK) // CMP_STRIDE + 1
    ov = ((ci <= sj + SEL_BLOCK - 1) & (ci + CMP_BLOCK - 1 >= sj) & (c[None, :] < n_cmp))
    ov = ov.astype(np.float32)
    pos = np.arange(t)
    ktag = np.zeros((t, 2 * LANES), np.float32)
    ktag[pos, HEAD_DIM + pos // SEL_BLOCK] = 1.0
    ktag[pos, LANES + pos // SEL_BLOCK] = 1.0
    ktag[:, LANES + HEAD_DIM] = pos % SEL_BLOCK
    ktag[:, LANES + HEAD_DIM + 1] = pos % SEL_BLOCK
    wtag = np.zeros((t, LANES), np.float32)
    wtag[:, HEAD_DIM + 1] = wtag[:, HEAD_DIM + 2] = pos // SEL_BLOCK
    wtag[:, HEAD_DIM + 3] = wtag[:, HEAD_DIM + 4] = pos % SEL_BLOCK
    s2 = np.repeat(slopes, Q_BLOCK, axis=1).astype(np.float64) * LOG2E
    s2_hi = s2.astype(BF16).astype(np.float64)
    b2 = s2 * SEL_BLOCK
    b2_hi = b2.astype(BF16).astype(np.float64)
    qflag = np.zeros((KV_GROUPS, HEAD_DIM, GQ), np.float32)
    qflag[:, 0, :] = NEG
    qflag[:, 1, :], qflag[:, 2, :] = b2_hi, b2 - b2_hi
    qflag[:, 3, :], qflag[:, 4, :] = s2_hi, s2 - s2_hi
    qslope = np.zeros((KV_GROUPS, HEAD_DIM, GQ), np.float32)
    qslope[:, 0, :] = s2_hi
    qslope[:, 1, :] = s2 - s2_hi
    return (jnp.asarray(tblc), jnp.asarray(wmask), jnp.asarray(tailm),
            jnp.asarray(ov, dtype=BF16), jnp.asarray(ktag, dtype=BF16), jnp.asarray(wtag, dtype=BF16),
            jnp.asarray(qflag, dtype=BF16), jnp.asarray(qslope, dtype=BF16))


def _nsa_kernel(qt_ref, gt_ref, ks_ref, kw_ref, vst_ref, vwt_ref,
                kcin_ref, vcin_ref, pek_ref, wk1_ref, wk2_ref, pev_ref, wv1_ref, wv2_ref,
                tblc_ref, wmask_ref, tailm_ref, ov_ref, ktag_ref, wtag_ref, qflag_ref, qslope_ref,
                o_ref,
                kaug_ref, kwaug_ref, kc_ref, vct_ref, m_ref, acc_ref, need_ref, *, t, n_blk):
    qi = pl.program_id(1)
    slopes = _slopes()
    hg = HEADS_PER_GROUP
    prep_rows = WINDOW

    @pl.when(qi == 0)
    def _prep():
        _compress_kernel(kcin_ref, vcin_ref, pek_ref, wk1_ref, wk2_ref, pev_ref, wv1_ref, wv2_ref,
                         kc_ref, vct_ref, n_rows=t // CMP_STRIDE)
        low = lax.broadcasted_iota(jnp.int32, (prep_rows, LANES), 1) < HEAD_DIM
        flag = jnp.where(lax.broadcasted_iota(jnp.int32, (WINDOW, LANES), 1) == HEAD_DIM, 1.0, 0.0)
        for g in range(KV_GROUPS):
            kwaug_ref[g, 0:WINDOW, :] = flag.astype(BF16)

        def body(c, carry):
            r0 = pl.multiple_of(c * prep_rows, prep_rows)
            rows = pl.ds(r0, prep_rows)
            rows_w = pl.ds(r0 + WINDOW, prep_rows)
            tag = ktag_ref[rows, 0:LANES].astype(F32)
            wtag = wtag_ref[rows, :].astype(F32)
            ks = ks_ref[rows, :].astype(F32)
            kw = kw_ref[rows, :].astype(F32)
            ks_sw = pltpu.roll(ks, HEAD_DIM, axis=1)
            kw_sw = pltpu.roll(kw, HEAD_DIM, axis=1)
            for g in range(KV_GROUPS):
                kaug_ref[g, rows, 0:LANES] = jnp.where(low, ks if g == 0 else ks_sw, tag).astype(BF16)
                kaug_ref[g, rows, LANES:2 * LANES] = ktag_ref[rows, LANES:2 * LANES]
                kwaug_ref[g, rows_w, :] = jnp.where(low, kw if g == 0 else kw_sw, wtag).astype(BF16)
            return carry

        lax.fori_loop(0, t // prep_rows, body, 0)

    gates = jax.nn.sigmoid(gt_ref[...])
    bi = lax.broadcasted_iota(jnp.int32, (n_blk, Q_BLOCK), 0)
    lq = lax.broadcasted_iota(jnp.int32, (n_blk, Q_BLOCK), 1)
    cur = (Q_BLOCK // SEL_BLOCK) * qi + lax.shift_right_logical(lq, SEL_BLOCK.bit_length() - 1)
    valid = bi <= cur
    near = (bi == 0) | (bi >= cur - 1)
    always = valid & (near | (cur < N_SEL))
    free = valid & jnp.logical_not(near)
    far = (bi >= FIRST_KEYS // SEL_BLOCK) & (bi < (qi * Q_BLOCK - PREV_KEYS) // SEL_BLOCK)
    rel_blk = (bi - cur).astype(F32)

    head_slabs = [slice(h * Q_BLOCK, (h + SLAB_HEADS) * Q_BLOCK) for h in range(0, hg, SLAB_HEADS)]

    q_pos = qi * Q_BLOCK + lax.broadcasted_iota(jnp.int32, (1, Q_BLOCK), 1)
    cmp_last = jnp.concatenate([lax.shift_right_arithmetic(q_pos - (CMP_BLOCK - 1), CMP_STRIDE.bit_length() - 1)] * hg,
                               axis=1)
    cmp_row = lax.broadcasted_iota(jnp.int32, (t // CMP_STRIDE, SLAB_HEADS * Q_BLOCK), 0)

    def with_ones(vt):
        first = lax.broadcasted_iota(jnp.int32, (ONES_ROWS, vt.shape[1]), 0) == 0
        return jnp.concatenate([vt, jnp.where(first, 1.0, 0.0).astype(BF16)], axis=0)

    def queries(g):
        heads = lambda ref: jnp.concatenate([ref[(g * hg + h) * HEAD_DIM:(g * hg + h + 1) * HEAD_DIM, :]
                                             for h in range(hg)], axis=1)
        qt = heads(qt_ref)
        return qt, jnp.concatenate([qt, qflag_ref[g]], axis=0)

    def select(g, qt, s):
        o_parts, psum = [], None
        for lanes in head_slabs:
            sh = s[:, lanes] + tblc_ref[g, :, lanes]
            sh = jnp.where(cmp_row <= cmp_last[:, lanes], sh, NEG)
            m = jnp.max(sh, axis=0, keepdims=True)
            m = jnp.where(m < 0.5 * NEG, 0.0, m)
            p = jnp.exp2(sh - m)
            l = jnp.sum(p, axis=0, keepdims=True)
            pn = p * (1.0 / jnp.maximum(l, 1e-30))
            o_parts.append(_dot(vct_ref[g], pn.astype(BF16)))
            for h in range(SLAB_HEADS):
                ph = pn[:, h * Q_BLOCK:(h + 1) * Q_BLOCK]
                psum = ph if psum is None else psum + ph
        o_cmp = jnp.concatenate(o_parts, axis=1)

        p_hi = psum.astype(BF16)
        p_lo = (psum - p_hi.astype(F32)).astype(BF16)
        imp = _dot(ov_ref[...], p_hi) + _dot(ov_ref[...], p_lo)
        val = jnp.where(free, imp, -1.0)
        sel = jnp.zeros((n_blk, Q_BLOCK), F32)
        for _ in range(N_SEL - 3):
            top = jnp.max(val, axis=0, keepdims=True)
            idx = jnp.min(jnp.where(val == top, bi, n_blk), axis=0, keepdims=True)
            pick = bi == idx
            sel = jnp.where(pick, 1.0, sel)
            val = jnp.where(pick, -2.0, val)
        chosen = jnp.where(always, 1.0, jnp.where(free, sel, 0.0)) > 0.0

        def tagged(blocks):
            bias = jnp.concatenate(
                [(float(slopes[g * hg + h]) * SEL_BLOCK * LOG2E) * rel_blk for h in range(hg)], axis=1)
            on = jnp.concatenate([blocks] * hg, axis=1)
            hi = bias.astype(BF16)
            lo = (bias - hi.astype(F32)).astype(BF16)
            pad = [jnp.zeros((HEAD_DIM - n_blk, GQ), BF16)] if n_blk < HEAD_DIM else []
            parts = [qt, jnp.where(on, hi, NEG)] + pad + [jnp.where(on, lo, 0.0)] + pad + [qslope_ref[g]]
            return jnp.concatenate(parts, axis=0)

        return o_cmp, tagged(chosen), tagged(chosen & far), jnp.where(chosen & far, 1.0, 0.0)

    def sel_keys(g, pieces):
        return jnp.concatenate([kaug_ref[g, pl.ds(pl.multiple_of(s0 * LANES, LANES), n * LANES), :]
                                for s0, n in pieces], axis=0)

    def sel_values(g, pieces):
        return with_ones(jnp.concatenate([vst_ref[s0 + k, g * HEAD_DIM:(g + 1) * HEAD_DIM, :]
                                          for s0, n in pieces for k in range(n)], axis=1))

    def sel_update(s, vt, carry):
        m, acc = carry
        ms, accs = [], []
        for lanes in head_slabs:
            sh, mh = s[:, lanes], m[:, lanes]
            m_new = jnp.maximum(mh, jnp.max(sh, axis=0, keepdims=True))
            alpha = jnp.exp2(mh - m_new)
            p = jnp.exp2(sh - m_new).astype(BF16)
            ms.append(m_new)
            accs.append(acc[:, lanes] * alpha + _dot(vt, p))
        return jnp.concatenate(ms, axis=1), jnp.concatenate(accs, axis=1)

    def sel_steps(pieces, q, mask, carries):
        scores = [_dot(sel_keys(g, pieces), q[g]) for g in groups]
        if mask is not None:
            scores = [jnp.concatenate([sc[0:mask.shape[0]] + mask, sc[mask.shape[0]:]], axis=0) for sc in scores]
        return tuple(sel_update(scores[g], sel_values(g, pieces), carries[g]) for g in groups)

    def window_scores(g, qw):
        rows = pl.ds(pl.multiple_of(qi * Q_BLOCK, Q_BLOCK), WIN_KEYS)
        return _dot(kwaug_ref[g, rows, :], qw)

    def window(g, s):
        back = WINDOW // LANES
        ahead = Q_BLOCK // LANES
        vt = jnp.concatenate([vwt_ref[jnp.maximum(qi * ahead - back + k, 0), g * HEAD_DIM:(g + 1) * HEAD_DIM, :]
                              for k in range(back + ahead)], axis=1)
        vt = with_ones(vt)
        outs = []
        too_old = jnp.concatenate([wmask_ref[0]] * SLAB_HEADS, axis=1)
        ahead_of = jnp.concatenate([wmask_ref[1]] * SLAB_HEADS, axis=1)
        for lanes in head_slabs:
            sh = jnp.concatenate([s[0:Q_BLOCK, lanes] + too_old, s[Q_BLOCK:WINDOW, lanes],
                                  s[WINDOW:WIN_KEYS, lanes] + ahead_of], axis=0)
            m = jnp.max(sh, axis=0, keepdims=True)
            p = jnp.exp2(sh - m).astype(BF16)
            acc = _dot(vt, p)
            outs.append(acc[0:HEAD_DIM] * (1.0 / acc[HEAD_DIM:HEAD_DIM + 1]))
        return jnp.concatenate(outs, axis=1)

    groups = tuple(range(KV_GROUPS))
    qt, qw = zip(*[queries(g) for g in groups])
    s_cmp = [_dot(kc_ref[g], qw[g]) for g in groups]
    s_win = [window_scores(g, qw[g]) for g in groups]
    sel_out, o_win = [], []
    for g in groups:
        sel_out.append(select(g, qt[g], s_cmp[g]))
        o_win.append(window(g, s_win[g]))
    o_cmp, qaug, qfar, picked = zip(*sel_out)

    nq, n_first = Q_BLOCK // LANES, FIRST_KEYS // LANES
    future = (qi + 1) * nq
    diag_pieces = [(qi * nq, nq)]
    for back in range(1, PREV_KEYS // Q_BLOCK + 1):
        diag_pieces.append((jnp.where(qi >= back, (qi - back) * nq, future), nq))
    diag_pieces.append((jnp.where(qi * Q_BLOCK >= PREV_KEYS + FIRST_KEYS, 0, future), n_first))
    causal = jnp.concatenate([tailm_ref[...]] * hg, axis=1)
    init = (jnp.full((1, GQ), NEG, F32), jnp.zeros((HEAD_DIM + ONES_ROWS, GQ), F32))

    def load_carries():
        return tuple((m_ref[g, 0:1, :], acc_ref[g]) for g in groups)

    def store_carries(carries):
        for g, (m, acc) in enumerate(carries):
            m_ref[g, 0:1, :] = m
            acc_ref[g] = acc

    store_carries(sel_steps(diag_pieces, qaug, causal, tuple(init for _ in groups)))

    blk_per_chunk = FAR_CHUNK // SEL_BLOCK
    wanted = picked[0]
    for g in groups[1:]:
        wanted = wanted + picked[g]
    for c in range(n_blk // blk_per_chunk):
        hit = jnp.sum(wanted[c * blk_per_chunk:(c + 1) * blk_per_chunk, :])
        need_ref[c] = (hit > 0.0).astype(jnp.int32)

    def far_chunk(c, carry):
        @pl.when(need_ref[c] > 0)
        def _():
            store_carries(sel_steps([(c * (FAR_CHUNK // LANES), FAR_CHUNK // LANES)], qfar, None, load_carries()))
        return carry

    far_keys = jnp.maximum(qi * Q_BLOCK - PREV_KEYS, 0)
    lax.fori_loop(0, (far_keys + FAR_CHUNK - 1) // FAR_CHUNK, far_chunk, 0)
    o_slc = [acc[0:HEAD_DIM] * (1.0 / acc[HEAD_DIM:HEAD_DIM + 1]) for _, acc in load_carries()]

    for g in groups:
        for pr in range(hg // 2):
            outs = []
            for h in (2 * pr, 2 * pr + 1):
                hh = g * hg + h
                lanes = slice(h * Q_BLOCK, (h + 1) * Q_BLOCK)
                out = gates[3 * hh:3 * hh + 1, :] * o_cmp[g][:, lanes]
                out = out + gates[3 * hh + 1:3 * hh + 2, :] * o_slc[g][:, lanes]
                out = out + gates[3 * hh + 2:3 * hh + 3, :] * o_win[g][:, lanes]
                outs.append(out)
            pair = jnp.concatenate(outs, axis=0)
            o_ref[:, (g * 2 + pr) * LANES:(g * 2 + pr + 1) * LANES] = pair.T.astype(BF16)


def _nsa(qt, gt, ks, kw, vst, vwt, kcin, vcin, cmp_weights, tables):
    b, _, t = qt.shape
    n_blk = t // SEL_BLOCK
    n_rows = t // CMP_STRIDE
    assert PREV_KEYS % Q_BLOCK == 0 and t % FAR_CHUNK == 0 and t % Q_BLOCK == 0
    assert t >= PREV_KEYS + 2 * Q_BLOCK
    assert N_SEL <= n_blk <= HEAD_DIM and n_blk % 16 == 0
    seq_spec = pl.BlockSpec((None, t, LANES), lambda i, j: (i, 0, 0))
    chunked = pl.BlockSpec((None, t // LANES, KV_WIDTH, LANES), lambda i, j: (i, 0, 0, 0))
    return pl.pallas_call(
        functools.partial(_nsa_kernel, t=t, n_blk=n_blk),
        grid=(b, t // Q_BLOCK),
        in_specs=[pl.BlockSpec((None, NSA_WIDTH, Q_BLOCK), lambda i, j: (i, 0, j)),
                  pl.BlockSpec((None, GATE_ROWS, Q_BLOCK), lambda i, j: (i, 0, j)),
                  seq_spec, seq_spec, chunked, chunked, seq_spec, seq_spec]
                 + [_resident(a.shape) for a in cmp_weights] + [_resident(a.shape) for a in tables],
        out_specs=pl.BlockSpec((None, Q_BLOCK, NSA_WIDTH), lambda i, j: (i, j, 0)),
        out_shape=jax.ShapeDtypeStruct((b, t, NSA_WIDTH), BF16),
        scratch_shapes=[pltpu.VMEM((KV_GROUPS, t, 2 * LANES), BF16),
                        pltpu.VMEM((KV_GROUPS, t + WINDOW, LANES), BF16),
                        pltpu.VMEM((KV_GROUPS, n_rows, LANES), BF16),
                        pltpu.VMEM((KV_GROUPS, HEAD_DIM, n_rows), BF16),
                        pltpu.VMEM((KV_GROUPS, 8, GQ), F32),
                        pltpu.VMEM((KV_GROUPS, HEAD_DIM + ONES_ROWS, GQ), F32),
                        pltpu.SMEM((t // FAR_CHUNK,), jnp.int32)],
        compiler_params=pltpu.CompilerParams(
            dimension_semantics=("arbitrary", "arbitrary"), vmem_limit_bytes=VMEM_LIMIT),
        name="nsa",
    )(qt, gt, ks, kw, vst, vwt, kcin, vcin, *cmp_weights, *tables)


def _mix_ffn_kernel(x_ref, o_ref, u_ref, halo_ref, wp_ref, sc_ref, wo_ref, g2_ref, wg_ref, wu_ref, wd_ref,
                    *rest, tm, final):
    if final:
        fn_ref, y_ref, mix_ref, h_ref, acc_ref = rest
    else:
        y_ref, mix_ref, h_ref, acc_ref = rest
    ti = pl.program_id(1)
    mix_ref[:, 0:NSA_WIDTH] = o_ref[...]
    halo = jnp.where(ti > 0, halo_ref[...], 0.0)
    pos = ti * tm + lax.broadcasted_iota(jnp.int32, (tm, 1), 0)
    for gi, w in enumerate(POOL_WIDTHS):
        lanes = slice(gi * POOL_GROUP, (gi + 1) * POOL_GROUP)
        u = u_ref[:, lanes]
        ext = jnp.concatenate([halo[:, lanes], u], axis=0)
        run = ext
        step = 1
        while step < w:
            run = run + pltpu.roll(run, step, axis=0)
            step *= 2
        cnt = jnp.minimum(pos + 1, w).astype(F32)
        y = run[POOL_HALO:, :] / cnt - u
        yo = _dot(y.astype(BF16), wp_ref[gi]) * sc_ref[:, lanes]
        mix_ref[:, NSA_WIDTH + gi * POOL_GROUP:NSA_WIDTH + (gi + 1) * POOL_GROUP] = yo.astype(BF16)
    x2 = x_ref[...] + _dot(mix_ref[...], wo_ref[...])
    y = _swiglu_residual(x2, g2_ref, wg_ref, wu_ref, wd_ref, h_ref, acc_ref)
    if final:
        y = _rms(y, fn_ref[...])
    y_ref[...] = y


def _mix_ffn(x3, layer, o_nsa, u, wp, sc, wo, g2, wg, wu, wd, final_g=None, tm=512):
    b, t, d = x3.shape
    per = tm // POOL_HALO
    final = final_g is not None
    tail_specs = [_resident((1, d))] if final else []
    tail_args = [final_g] if final else []
    return pl.pallas_call(
        functools.partial(_mix_ffn_kernel, tm=tm, final=final),
        grid=(b, t // tm),
        in_specs=[pl.BlockSpec((None, tm, d), lambda i, j: (i, j, 0)),
                  pl.BlockSpec((None, tm, NSA_WIDTH), lambda i, j: (i, j, 0)),
                  pl.BlockSpec((None, tm, POOL_CH), lambda i, j: (i, j, 0)),
                  pl.BlockSpec((None, POOL_HALO, POOL_CH), lambda i, j: (i, jnp.maximum(j * per - 1, 0), 0)),
                  _resident(wp.shape[1:], layer), _resident(sc.shape), _resident(wo.shape[1:], layer),
                  _resident((1, d)), _resident(wg.shape[1:], layer), _resident(wu.shape[1:], layer),
                  _resident(wd.shape[1:], layer)] + tail_specs,
        out_specs=pl.BlockSpec((None, tm, d), lambda i, j: (i, j, 0)),
        out_shape=jax.ShapeDtypeStruct((b, t, d), F32),
        scratch_shapes=[pltpu.VMEM((tm, NSA_WIDTH + POOL_CH), BF16),
                        pltpu.VMEM((tm, d), BF16), pltpu.VMEM((tm, d), F32)],
        compiler_params=pltpu.CompilerParams(
            dimension_semantics=("parallel", "parallel"), vmem_limit_bytes=VMEM_LIMIT),
        name="mix_ffn_final" if final else "mix_ffn",
    )(x3, o_nsa, u, u, wp, sc, wo, g2, wg, wu, wd, *tail_args)


def kernel(x, ffn1_norm, ffn1_wg, ffn1_wu, ffn1_wd, mix_norm, w_in, cmp_pe_k, cmp_wk1, cmp_wk2,
           cmp_pe_v, cmp_wv1, cmp_wv2, pool_w, pool_scale, w_out, ffn2_norm, ffn2_wg, ffn2_wu,
           ffn2_wd, final_norm):
    b, t, d = x.shape
    depth = ffn1_norm.shape[0]
    tables = _nsa_tables(t)
    f1g, f1u, f1d, f2g, f2u, f2d, wp, wo = (
        a.astype(BF16) for a in (ffn1_wg, ffn1_wu, ffn1_wd, ffn2_wg, ffn2_wu, ffn2_wd, pool_w, w_out))
    for l in range(depth):
        x, kcin, vcin, ks, kw, u, qt, vst, vwt, gt = _ffn_proj(
            x, l, ffn1_norm[l][None], f1g, f1u, f1d, mix_norm[l][None], *_split_w_in(w_in[l]))
        cmp_weights = (_compress_weights(cmp_pe_k[l], cmp_wk1[l], cmp_wk2[l], False)
                       + _compress_weights(cmp_pe_v[l], cmp_wv1[l], cmp_wv2[l], True))
        o_nsa = _nsa(qt, gt, ks, kw, vst, vwt, kcin, vcin, cmp_weights, tables)
        x = _mix_ffn(x, l, o_nsa, u, wp, pool_scale[l][None], wo, ffn2_norm[l][None], f2g, f2u, f2d,
                     final_g=final_norm[None] if l == depth - 1 else None)
    return x
```

```python
import functools

import numpy as np
import jax
import jax.numpy as jnp
from jax import lax
from jax.experimental import pallas as pl
from jax.experimental.pallas import tpu as pltpu

HEAD_DIM = 64
NSA_HEADS = 8
KV_GROUPS = 2
HEADS_PER_GROUP = NSA_HEADS // KV_GROUPS
NSA_WIDTH = NSA_HEADS * HEAD_DIM
KV_WIDTH = KV_GROUPS * HEAD_DIM
CMP_BLOCK = 32
CMP_STRIDE = 16
CMP_HIDDEN = 128
SEL_BLOCK = 64
N_SEL = 8
WINDOW = 512
Q_BLOCK = 128
POOL_WIDTHS = (2, 4, 8, 16)
POOL_GROUP = 128
POOL_CH = POOL_GROUP * len(POOL_WIDTHS)
N_GATES = 3 * NSA_HEADS
GATE_ROWS = 32
EPS = 1e-6

LANES = 128
GQ = HEADS_PER_GROUP * Q_BLOCK
PREV_KEYS = 512
FAR_CHUNK = 512
SLAB_HEADS = 2
FIRST_KEYS = 128
WIN_KEYS = WINDOW + Q_BLOCK
ONES_ROWS = 16
POOL_HALO = 16
FF_CHUNK = 256
LOG2E = 1.4426950408889634
NEG = -1e30
VMEM_LIMIT = 56 * 1024 * 1024

F32 = jnp.float32
BF16 = jnp.bfloat16


def _slopes():
    n = NSA_HEADS
    return (2.0 ** (-8.0 * np.arange(1, n + 1) / n)).astype(np.float32)


def _dot(a, b):
    return jnp.dot(a, b, preferred_element_type=F32)


def _dot_nt(a, b):
    return lax.dot_general(a, b, (((1,), (1,)), ((), ())), preferred_element_type=F32)


def _rms(x, g):
    return x * lax.rsqrt(jnp.mean(x * x, axis=-1, keepdims=True) + EPS) * g


def _resident(shape, layer=None):
    nd = len(shape)
    if layer is None:
        return pl.BlockSpec(shape, lambda *_: (0,) * nd, pipeline_mode=pl.Buffered(1))
    return pl.BlockSpec((None,) + tuple(shape), lambda *_: (layer,) + (0,) * nd, pipeline_mode=pl.Buffered(1))


def _swiglu_residual(x, g_ref, wg_ref, wu_ref, wd_ref, h_ref, acc_ref):
    h_ref[...] = _rms(x, g_ref[...]).astype(BF16)
    for c in range(wg_ref.shape[1] // FF_CHUNK):
        cols = slice(c * FF_CHUNK, (c + 1) * FF_CHUNK)
        h = h_ref[...]
        gate = _dot(h, wg_ref[:, cols])
        up = _dot(h, wu_ref[:, cols])
        a = (gate * jax.nn.sigmoid(gate) * up).astype(BF16)
        part = _dot(a, wd_ref[cols, :])
        if c == 0:
            acc_ref[...] = part
        else:
            acc_ref[...] += part
    return x + 0.5 * acc_ref[...]


_TOK_KC = (0, KV_WIDTH)
_TOK_VC = (KV_WIDTH, 2 * KV_WIDTH)
_TOK_KS = (2 * KV_WIDTH, 3 * KV_WIDTH)
_TOK_KW = (3 * KV_WIDTH, 4 * KV_WIDTH)
_TOK_U = (4 * KV_WIDTH, 4 * KV_WIDTH + POOL_CH)
_FEAT_Q = (0, NSA_WIDTH)
_FEAT_VS = (NSA_WIDTH, NSA_WIDTH + KV_WIDTH)
_FEAT_VW = (NSA_WIDTH + KV_WIDTH, NSA_WIDTH + 2 * KV_WIDTH)
_FEAT_GT = (NSA_WIDTH + 2 * KV_WIDTH, NSA_WIDTH + 2 * KV_WIDTH + GATE_ROWS)


def _ffn_proj_kernel(x_ref, g1_ref, wg_ref, wu_ref, wd_ref, g_ref, wt_ref, wf_ref,
                     x1_ref, kcin_ref, vcin_ref, ks_ref, kw_ref, u_ref,
                     qt_ref, vst_ref, vwt_ref, gt_ref, h_ref, acc_ref, *, tm):
    x1 = _swiglu_residual(x_ref[...], g1_ref, wg_ref, wu_ref, wd_ref, h_ref, acc_ref)
    x1_ref[...] = x1
    h = _rms(x1, g_ref[...]).astype(BF16)

    def tok(lo_hi):
        return _dot(h, wt_ref[:, lo_hi[0]:lo_hi[1]])

    kcin_ref[...] = tok(_TOK_KC)
    vcin_ref[...] = tok(_TOK_VC)
    ks_ref[...] = tok(_TOK_KS).astype(BF16)
    kw_ref[...] = tok(_TOK_KW).astype(BF16)
    u_ref[...] = tok(_TOK_U)
    zt = _dot_nt(wf_ref[...], h)
    qt_ref[...] = (zt[_FEAT_Q[0]:_FEAT_Q[1]] * (HEAD_DIM ** -0.5 * LOG2E)).astype(BF16)
    gt_ref[...] = zt[_FEAT_GT[0]:_FEAT_GT[1]]
    for c in range(tm // LANES):
        cols = slice(c * LANES, (c + 1) * LANES)
        vst_ref[c] = zt[_FEAT_VS[0]:_FEAT_VS[1], cols].astype(BF16)
        vwt_ref[c] = zt[_FEAT_VW[0]:_FEAT_VW[1], cols].astype(BF16)


def _ffn_proj(x3, layer, g1, wg, wu, wd, g, w_tok, w_feat, tm=512):
    b, t, d = x3.shape
    nck = tm // LANES
    tok = lambda wd: pl.BlockSpec((None, tm, wd), lambda i, j: (i, j, 0))
    chunked = pl.BlockSpec((None, nck, KV_WIDTH, LANES), lambda i, j: (i, j, 0, 0))
    return pl.pallas_call(
        functools.partial(_ffn_proj_kernel, tm=tm),
        grid=(b, t // tm),
        in_specs=[tok(d), _resident((1, d)),
                  _resident(wg.shape[1:], layer), _resident(wu.shape[1:], layer), _resident(wd.shape[1:], layer),
                  _resident((1, d)), _resident(w_tok.shape), _resident(w_feat.shape)],
        out_specs=[tok(d), tok(KV_WIDTH), tok(KV_WIDTH), tok(KV_WIDTH), tok(KV_WIDTH), tok(POOL_CH),
                   pl.BlockSpec((None, NSA_WIDTH, tm), lambda i, j: (i, 0, j)),
                   chunked, chunked,
                   pl.BlockSpec((None, GATE_ROWS, tm), lambda i, j: (i, 0, j))],
        out_shape=[jax.ShapeDtypeStruct((b, t, d), F32),
                   jax.ShapeDtypeStruct((b, t, KV_WIDTH), F32),
                   jax.ShapeDtypeStruct((b, t, KV_WIDTH), F32),
                   jax.ShapeDtypeStruct((b, t, KV_WIDTH), BF16),
                   jax.ShapeDtypeStruct((b, t, KV_WIDTH), BF16),
                   jax.ShapeDtypeStruct((b, t, POOL_CH), F32),
                   jax.ShapeDtypeStruct((b, NSA_WIDTH, t), BF16),
                   jax.ShapeDtypeStruct((b, t // LANES, KV_WIDTH, LANES), BF16),
                   jax.ShapeDtypeStruct((b, t // LANES, KV_WIDTH, LANES), BF16),
                   jax.ShapeDtypeStruct((b, GATE_ROWS, t), F32)],
        scratch_shapes=[pltpu.VMEM((tm, d), BF16), pltpu.VMEM((tm, d), F32)],
        compiler_params=pltpu.CompilerParams(
            dimension_semantics=("parallel", "parallel"), vmem_limit_bytes=VMEM_LIMIT),
        name="ffn_proj",
    )(x3, g1, wg, wu, wd, g, w_tok, w_feat)


def _split_w_in(w):
    q0, kv0 = 0, NSA_WIDTH
    col = lambda i: slice(kv0 + i * KV_WIDTH, kv0 + (i + 1) * KV_WIDTH)
    g0 = kv0 + 6 * KV_WIDTH
    w_tok = jnp.concatenate([w[:, col(0)], w[:, col(1)], w[:, col(2)], w[:, col(4)],
                             w[:, g0 + N_GATES:]], axis=1).astype(BF16)
    pad = jnp.zeros((w.shape[0], GATE_ROWS - N_GATES), w.dtype)
    w_feat = jnp.concatenate([w[:, q0:NSA_WIDTH], w[:, col(3)], w[:, col(5)],
                              w[:, g0:g0 + N_GATES], pad], axis=1).T.astype(BF16)
    return w_tok, w_feat


def _gelu_tanh(x):
    c = np.float32(np.sqrt(2.0 / np.pi))
    return 0.5 * x * (1.0 + jnp.tanh(c * (x + 0.044715 * (x * x * x))))


def _compress_kernel(kcin_ref, vcin_ref, pek_ref, wk1_ref, wk2_ref, pev_ref, wv1_ref, wv2_ref,
                     kc_ref, vct_ref, *, n_rows):
    nl = CMP_STRIDE
    hid2 = KV_GROUPS * CMP_HIDDEN

    def hidden(src_ref, pe_ref, w1_ref):
        acc = jnp.zeros((n_rows + 16, 2 * hid2), F32)
        for l in range(nl):
            xl = src_ref[pl.ds(l, n_rows, stride=nl), :]
            pa = jnp.broadcast_to(pe_ref[l:l + 1, :], (8, KV_WIDTH))
            pb = jnp.broadcast_to(pe_ref[nl + l:nl + l + 1, :], (8, KV_WIDTH))
            lhs = jnp.concatenate([xl, pa, pb], axis=0).astype(BF16)
            acc = acc + _dot(lhs, w1_ref[l])
        sa = acc[0:n_rows, 0:hid2]
        sb = acc[0:n_rows, hid2:2 * hid2]
        bias = acc[n_rows:n_rows + 1, 0:hid2] + acc[n_rows + 8:n_rows + 9, hid2:2 * hid2]
        hcur = sa + pltpu.roll(sb, n_rows - 1, axis=0) + bias
        return _gelu_tanh(hcur).astype(BF16)

    hk = hidden(kcin_ref, pek_ref, wk1_ref)
    kc = _dot(hk, wk2_ref[...])
    for g in range(KV_GROUPS):
        kc_ref[g] = kc[:, g * LANES:(g + 1) * LANES].astype(BF16)
    hv = hidden(vcin_ref, pev_ref, wv1_ref)
    for g in range(KV_GROUPS):
        vct_ref[g] = _dot_nt(wv2_ref[g], hv).astype(BF16)


def _compress(kcin, vcin, pek2, wk1b, wk2b, pev2, wv1b, wv2t):
    b, t, _ = kcin.shape
    n_rows = t // CMP_STRIDE
    seq_spec = pl.BlockSpec((None, t, KV_WIDTH), lambda i: (i, 0, 0))
    return pl.pallas_call(
        functools.partial(_compress_kernel, n_rows=n_rows),
        grid=(b,),
        in_specs=[seq_spec, seq_spec,
                  _resident(pek2.shape), _resident(wk1b.shape), _resident(wk2b.shape),
                  _resident(pev2.shape), _resident(wv1b.shape), _resident(wv2t.shape)],
        out_specs=[pl.BlockSpec((None, KV_GROUPS, n_rows, LANES), lambda i: (i, 0, 0, 0)),
                   pl.BlockSpec((None, KV_GROUPS, HEAD_DIM, n_rows), lambda i: (i, 0, 0, 0))],
        out_shape=[jax.ShapeDtypeStruct((b, KV_GROUPS, n_rows, LANES), BF16),
                   jax.ShapeDtypeStruct((b, KV_GROUPS, HEAD_DIM, n_rows), BF16)],
        compiler_params=pltpu.CompilerParams(
            dimension_semantics=("parallel",), vmem_limit_bytes=VMEM_LIMIT),
        name="compress",
    )(kcin, vcin, pek2, wk1b, wk2b, pev2, wv1b, wv2t)


def _compress_weights(pe, w1, w2, value_layout):
    nl, dh, hid = CMP_STRIDE, HEAD_DIM, CMP_HIDDEN
    w1r = w1.reshape(2, nl, dh, hid)
    z = jnp.zeros((nl, dh, hid), w1.dtype)
    row_g0 = jnp.concatenate([w1r[0], z, w1r[1], z], axis=-1)
    row_g1 = jnp.concatenate([z, w1r[0], z, w1r[1]], axis=-1)
    w1b = jnp.concatenate([row_g0, row_g1], axis=1).astype(BF16)
    pe2 = jnp.concatenate([pe, pe], axis=-1)
    zz = jnp.zeros((hid, dh), w2.dtype)
    if value_layout:
        w2b = jnp.stack([jnp.concatenate([w2, zz], axis=0).T,
                         jnp.concatenate([zz, w2], axis=0).T]).astype(BF16)
    else:
        top = jnp.concatenate([w2, zz, zz, zz], axis=-1)
        bot = jnp.concatenate([zz, zz, w2, zz], axis=-1)
        w2b = jnp.concatenate([top, bot], axis=0).astype(BF16)
    return pe2, w1b, w2b


def _nsa_tables(t):
    slopes = _slopes().reshape(KV_GROUPS, HEADS_PER_GROUP)
    n_rows = t // CMP_STRIDE
    n_blk = t // SEL_BLOCK
    i = np.arange(Q_BLOCK)
    c = np.arange(n_rows)
    d0c = (i[None, :] - (c[:, None] * CMP_STRIDE + CMP_BLOCK - 1)).astype(np.int32)
    tblc = -LOG2E * slopes[:, None, :, None].astype(np.float64) * d0c[None, :, None, :]
    tblc = tblc.astype(np.float32)
    tblc = tblc.reshape(KV_GROUPS, n_rows, GQ)
    e = np.arange(Q_BLOCK)
    wmask = np.stack([np.where(e[:, None] <= i[None, :], np.float32(NEG), np.float32(0)),
                      np.where(e[:, None] > i[None, :], np.float32(NEG), np.float32(0))])
    tailm = np.where(e[:, None] > i[None, :], np.float32(NEG), np.float32(0))
    ci = c[None, :] * CMP_STRIDE
    sj = np.arange(n_blk)[:, None] * SEL_BLOCK
    n_cmp = (t - CMP_BLOCK) // CMP_STRIDE + 1
    ov = ((ci <= sj + SEL_BLOCK - 1) & (ci + CMP_BLOCK - 1 >= sj) & (c[None, :] < n_cmp))
    ov = ov.astype(np.float32)
    pos = np.arange(t)
    ktag = np.zeros((t, 2 * LANES), np.float32)
    ktag[pos, HEAD_DIM + pos // SEL_BLOCK] = 1.0
    ktag[pos, LANES + pos // SEL_BLOCK] = 1.0
    ktag[:, LANES + HEAD_DIM] = pos % SEL_BLOCK
    ktag[:, LANES + HEAD_DIM + 1] = pos % SEL_BLOCK
    wtag = np.zeros((t, LANES), np.float32)
    wtag[:, HEAD_DIM + 1] = wtag[:, HEAD_DIM + 2] = pos // SEL_BLOCK
    wtag[:, HEAD_DIM + 3] = wtag[:, HEAD_DIM + 4] = pos % SEL_BLOCK
    s2 = np.repeat(slopes, Q_BLOCK, axis=1).astype(np.float64) * LOG2E
    s2_hi = s2.astype(BF16).astype(np.float64)
    b2 = s2 * SEL_BLOCK
    b2_hi = b2.astype(BF16).astype(np.float64)
    qflag = np.zeros((KV_GROUPS, HEAD_DIM, GQ), np.float32)
    qflag[:, 0, :] = NEG
    qflag[:, 1, :], qflag[:, 2, :] = b2_hi, b2 - b2_hi
    qflag[:, 3, :], qflag[:, 4, :] = s2_hi, s2 - s2_hi
    qslope = np.zeros((KV_GROUPS, HEAD_DIM, GQ), np.float32)
    qslope[:, 0, :] = s2_hi
    qslope[:, 1, :] = s2 - s2_hi
    return (jnp.asarray(tblc), jnp.asarray(wmask), jnp.asarray(tailm),
            jnp.asarray(ov, dtype=BF16), jnp.asarray(ktag, dtype=BF16), jnp.asarray(wtag, dtype=BF16),
            jnp.asarray(qflag, dtype=BF16), jnp.asarray(qslope, dtype=BF16))


def _nsa_kernel(qt_ref, gt_ref, ks_ref, kw_ref, vst_ref, vwt_ref, kc_ref, vct_ref,
                tblc_ref, wmask_ref, tailm_ref, ov_ref, ktag_ref, wtag_ref, qflag_ref, qslope_ref,
                o_ref,
                kaug_ref, kwaug_ref, m_ref, acc_ref, need_ref, *, t, n_blk):
    qi = pl.program_id(1)
    slopes = _slopes()
    hg = HEADS_PER_GROUP
    prep_rows = WINDOW

    @pl.when(qi == 0)
    def _prep():
        low = lax.broadcasted_iota(jnp.int32, (prep_rows, LANES), 1) < HEAD_DIM
        flag = jnp.where(lax.broadcasted_iota(jnp.int32, (WINDOW, LANES), 1) == HEAD_DIM, 1.0, 0.0)
        for g in range(KV_GROUPS):
            kwaug_ref[g, 0:WINDOW, :] = flag.astype(BF16)

        def body(c, carry):
            r0 = pl.multiple_of(c * prep_rows, prep_rows)
            rows = pl.ds(r0, prep_rows)
            rows_w = pl.ds(r0 + WINDOW, prep_rows)
            tag = ktag_ref[rows, 0:LANES].astype(F32)
            wtag = wtag_ref[rows, :].astype(F32)
            ks = ks_ref[rows, :].astype(F32)
            kw = kw_ref[rows, :].astype(F32)
            ks_sw = pltpu.roll(ks, HEAD_DIM, axis=1)
            kw_sw = pltpu.roll(kw, HEAD_DIM, axis=1)
            for g in range(KV_GROUPS):
                kaug_ref[g, rows, 0:LANES] = jnp.where(low, ks if g == 0 else ks_sw, tag).astype(BF16)
                kaug_ref[g, rows, LANES:2 * LANES] = ktag_ref[rows, LANES:2 * LANES]
                kwaug_ref[g, rows_w, :] = jnp.where(low, kw if g == 0 else kw_sw, wtag).astype(BF16)
            return carry

        lax.fori_loop(0, t // prep_rows, body, 0)

    gates = jax.nn.sigmoid(gt_ref[...])
    bi = lax.broadcasted_iota(jnp.int32, (n_blk, Q_BLOCK), 0)
    lq = lax.broadcasted_iota(jnp.int32, (n_blk, Q_BLOCK), 1)
    cur = (Q_BLOCK // SEL_BLOCK) * qi + lax.shift_right_logical(lq, SEL_BLOCK.bit_length() - 1)
    valid = bi <= cur
    near = (bi == 0) | (bi >= cur - 1)
    always = valid & (near | (cur < N_SEL))
    free = valid & jnp.logical_not(near)
    far = (bi >= FIRST_KEYS // SEL_BLOCK) & (bi < (qi * Q_BLOCK - PREV_KEYS) // SEL_BLOCK)
    rel_blk = (bi - cur).astype(F32)

    head_slabs = [slice(h * Q_BLOCK, (h + SLAB_HEADS) * Q_BLOCK) for h in range(0, hg, SLAB_HEADS)]

    q_pos = qi * Q_BLOCK + lax.broadcasted_iota(jnp.int32, (1, Q_BLOCK), 1)
    cmp_last = jnp.concatenate([lax.shift_right_arithmetic(q_pos - (CMP_BLOCK - 1), CMP_STRIDE.bit_length() - 1)] * hg,
                               axis=1)
    cmp_row = lax.broadcasted_iota(jnp.int32, (t // CMP_STRIDE, SLAB_HEADS * Q_BLOCK), 0)

    def with_ones(vt):
        first = lax.broadcasted_iota(jnp.int32, (ONES_ROWS, vt.shape[1]), 0) == 0
        return jnp.concatenate([vt, jnp.where(first, 1.0, 0.0).astype(BF16)], axis=0)

    def queries(g):
        heads = lambda ref: jnp.concatenate([ref[(g * hg + h) * HEAD_DIM:(g * hg + h + 1) * HEAD_DIM, :]
                                             for h in range(hg)], axis=1)
        qt = heads(qt_ref)
        return qt, jnp.concatenate([qt, qflag_ref[g]], axis=0)

    def select(g, qt, s):
        o_parts, psum = [], None
        for lanes in head_slabs:
            sh = s[:, lanes] + tblc_ref[g, :, lanes]
            sh = jnp.where(cmp_row <= cmp_last[:, lanes], sh, NEG)
            m = jnp.max(sh, axis=0, keepdims=True)
            m = jnp.where(m < 0.5 * NEG, 0.0, m)
            p = jnp.exp2(sh - m)
            l = jnp.sum(p, axis=0, keepdims=True)
            pn = p * (1.0 / jnp.maximum(l, 1e-30))
            o_parts.append(_dot(vct_ref[g], pn.astype(BF16)))
            for h in range(SLAB_HEADS):
                ph = pn[:, h * Q_BLOCK:(h + 1) * Q_BLOCK]
                psum = ph if psum is None else psum + ph
        o_cmp = jnp.concatenate(o_parts, axis=1)

        p_hi = psum.astype(BF16)
        p_lo = (psum - p_hi.astype(F32)).astype(BF16)
        imp = _dot(ov_ref[...], p_hi) + _dot(ov_ref[...], p_lo)
        val = jnp.where(free, imp, -1.0)
        sel = jnp.zeros((n_blk, Q_BLOCK), F32)
        for _ in range(N_SEL - 3):
            top = jnp.max(val, axis=0, keepdims=True)
            idx = jnp.min(jnp.where(val == top, bi, n_blk), axis=0, keepdims=True)
            pick = bi == idx
            sel = jnp.where(pick, 1.0, sel)
            val = jnp.where(pick, -2.0, val)
        chosen = jnp.where(always, 1.0, jnp.where(free, sel, 0.0)) > 0.0

        def tagged(blocks):
            bias = jnp.concatenate(
                [(float(slopes[g * hg + h]) * SEL_BLOCK * LOG2E) * rel_blk for h in range(hg)], axis=1)
            on = jnp.concatenate([blocks] * hg, axis=1)
            hi = bias.astype(BF16)
            lo = (bias - hi.astype(F32)).astype(BF16)
            pad = [jnp.zeros((HEAD_DIM - n_blk, GQ), BF16)] if n_blk < HEAD_DIM else []
            parts = [qt, jnp.where(on, hi, NEG)] + pad + [jnp.where(on, lo, 0.0)] + pad + [qslope_ref[g]]
            return jnp.concatenate(parts, axis=0)

        return o_cmp, tagged(chosen), tagged(chosen & far), jnp.where(chosen & far, 1.0, 0.0)

    def sel_keys(g, pieces):
        return jnp.concatenate([kaug_ref[g, pl.ds(pl.multiple_of(s0 * LANES, LANES), n * LANES), :]
                                for s0, n in pieces], axis=0)

    def sel_values(g, pieces):
        return with_ones(jnp.concatenate([vst_ref[s0 + k, g * HEAD_DIM:(g + 1) * HEAD_DIM, :]
                                          for s0, n in pieces for k in range(n)], axis=1))

    def sel_update(s, vt, carry):
        m, acc = carry
        ms, accs = [], []
        for lanes in head_slabs:
            sh, mh = s[:, lanes], m[:, lanes]
            m_new = jnp.maximum(mh, jnp.max(sh, axis=0, keepdims=True))
            alpha = jnp.exp2(mh - m_new)
            p = jnp.exp2(sh - m_new).astype(BF16)
            ms.append(m_new)
            accs.append(acc[:, lanes] * alpha + _dot(vt, p))
        return jnp.concatenate(ms, axis=1), jnp.concatenate(accs, axis=1)

    def sel_steps(pieces, q, mask, carries):
        scores = [_dot(sel_keys(g, pieces), q[g]) for g in groups]
        if mask is not None:
            scores = [jnp.concatenate([sc[0:mask.shape[0]] + mask, sc[mask.shape[0]:]], axis=0) for sc in scores]
        return tuple(sel_update(scores[g], sel_values(g, pieces), carries[g]) for g in groups)

    def window_scores(g, qw):
        rows = pl.ds(pl.multiple_of(qi * Q_BLOCK, Q_BLOCK), WIN_KEYS)
        return _dot(kwaug_ref[g, rows, :], qw)

    def window(g, s):
        back = WINDOW // LANES
        ahead = Q_BLOCK // LANES
        vt = jnp.concatenate([vwt_ref[jnp.maximum(qi * ahead - back + k, 0), g * HEAD_DIM:(g + 1) * HEAD_DIM, :]
                              for k in range(back + ahead)], axis=1)
        vt = with_ones(vt)
        outs = []
        too_old = jnp.concatenate([wmask_ref[0]] * SLAB_HEADS, axis=1)
        ahead_of = jnp.concatenate([wmask_ref[1]] * SLAB_HEADS, axis=1)
        for lanes in head_slabs:
            sh = jnp.concatenate([s[0:Q_BLOCK, lanes] + too_old, s[Q_BLOCK:WINDOW, lanes],
                                  s[WINDOW:WIN_KEYS, lanes] + ahead_of], axis=0)
            m = jnp.max(sh, axis=0, keepdims=True)
            p = jnp.exp2(sh - m).astype(BF16)
            acc = _dot(vt, p)
            outs.append(acc[0:HEAD_DIM] * (1.0 / acc[HEAD_DIM:HEAD_DIM + 1]))
        return jnp.concatenate(outs, axis=1)

    groups = tuple(range(KV_GROUPS))
    qt, qw = zip(*[queries(g) for g in groups])
    s_cmp = [_dot(kc_ref[g], qw[g]) for g in groups]
    s_win = [window_scores(g, qw[g]) for g in groups]
    sel_out, o_win = [], []
    for g in groups:
        sel_out.append(select(g, qt[g], s_cmp[g]))
        o_win.append(window(g, s_win[g]))
    o_cmp, qaug, qfar, picked = zip(*sel_out)

    nq, n_first = Q_BLOCK // LANES, FIRST_KEYS // LANES
    future = (qi + 1) * nq
    diag_pieces = [(qi * nq, nq)]
    for back in range(1, PREV_KEYS // Q_BLOCK + 1):
        diag_pieces.append((jnp.where(qi >= back, (qi - back) * nq, future), nq))
    diag_pieces.append((jnp.where(qi * Q_BLOCK >= PREV_KEYS + FIRST_KEYS, 0, future), n_first))
    causal = jnp.concatenate([tailm_ref[...]] * hg, axis=1)
    init = (jnp.full((1, GQ), NEG, F32), jnp.zeros((HEAD_DIM + ONES_ROWS, GQ), F32))

    def load_carries():
        return tuple((m_ref[g, 0:1, :], acc_ref[g]) for g in groups)

    def store_carries(carries):
        for g, (m, acc) in enumerate(carries):
            m_ref[g, 0:1, :] = m
            acc_ref[g] = acc

    store_carries(sel_steps(diag_pieces, qaug, causal, tuple(init for _ in groups)))

    blk_per_chunk = FAR_CHUNK // SEL_BLOCK
    wanted = picked[0]
    for g in groups[1:]:
        wanted = wanted + picked[g]
    for c in range(n_blk // blk_per_chunk):
        hit = jnp.sum(wanted[c * blk_per_chunk:(c + 1) * blk_per_chunk, :])
        need_ref[c] = (hit > 0.0).astype(jnp.int32)

    def far_chunk(c, carry):
        @pl.when(need_ref[c] > 0)
        def _():
            store_carries(sel_steps([(c * (FAR_CHUNK // LANES), FAR_CHUNK // LANES)], qfar, None, load_carries()))
        return carry

    far_keys = jnp.maximum(qi * Q_BLOCK - PREV_KEYS, 0)
    lax.fori_loop(0, (far_keys + FAR_CHUNK - 1) // FAR_CHUNK, far_chunk, 0)
    o_slc = [acc[0:HEAD_DIM] * (1.0 / acc[HEAD_DIM:HEAD_DIM + 1]) for _, acc in load_carries()]

    for g in groups:
        for pr in range(hg // 2):
            outs = []
            for h in (2 * pr, 2 * pr + 1):
                hh = g * hg + h
                lanes = slice(h * Q_BLOCK, (h + 1) * Q_BLOCK)
                out = gates[3 * hh:3 * hh + 1, :] * o_cmp[g][:, lanes]
                out = out + gates[3 * hh + 1:3 * hh + 2, :] * o_slc[g][:, lanes]
                out = out + gates[3 * hh + 2:3 * hh + 3, :] * o_win[g][:, lanes]
                outs.append(out)
            pair = jnp.concatenate(outs, axis=0)
            o_ref[:, (g * 2 + pr) * LANES:(g * 2 + pr + 1) * LANES] = pair.T.astype(BF16)


def _nsa(qt, gt, ks, kw, vst, vwt, kc, vct, tables):
    b, _, t = qt.shape
    n_blk = t // SEL_BLOCK
    n_rows = t // CMP_STRIDE
    assert PREV_KEYS % Q_BLOCK == 0 and t % FAR_CHUNK == 0 and t % Q_BLOCK == 0
    assert t >= PREV_KEYS + 2 * Q_BLOCK
    assert N_SEL <= n_blk <= HEAD_DIM and n_blk % 16 == 0
    seq_spec = pl.BlockSpec((None, t, LANES), lambda i, j: (i, 0, 0))
    chunked = pl.BlockSpec((None, t // LANES, KV_WIDTH, LANES), lambda i, j: (i, 0, 0, 0))
    return pl.pallas_call(
        functools.partial(_nsa_kernel, t=t, n_blk=n_blk),
        grid=(b, t // Q_BLOCK),
        in_specs=[pl.BlockSpec((None, NSA_WIDTH, Q_BLOCK), lambda i, j: (i, 0, j)),
                  pl.BlockSpec((None, GATE_ROWS, Q_BLOCK), lambda i, j: (i, 0, j)),
                  seq_spec, seq_spec, chunked, chunked,
                  pl.BlockSpec((None, KV_GROUPS, n_rows, LANES), lambda i, j: (i, 0, 0, 0)),
                  pl.BlockSpec((None, KV_GROUPS, HEAD_DIM, n_rows), lambda i, j: (i, 0, 0, 0))]
                 + [_resident(a.shape) for a in tables],
        out_specs=pl.BlockSpec((None, Q_BLOCK, NSA_WIDTH), lambda i, j: (i, j, 0)),
        out_shape=jax.ShapeDtypeStruct((b, t, NSA_WIDTH), BF16),
        scratch_shapes=[pltpu.VMEM((KV_GROUPS, t, 2 * LANES), BF16),
                        pltpu.VMEM((KV_GROUPS, t + WINDOW, LANES), BF16),
                        pltpu.VMEM((KV_GROUPS, 8, GQ), F32),
                        pltpu.VMEM((KV_GROUPS, HEAD_DIM + ONES_ROWS, GQ), F32),
                        pltpu.SMEM((t // FAR_CHUNK,), jnp.int32)],
        compiler_params=pltpu.CompilerParams(
            dimension_semantics=("arbitrary", "arbitrary"), vmem_limit_bytes=VMEM_LIMIT),
        name="nsa",
    )(qt, gt, ks, kw, vst, vwt, kc, vct, *tables)


def _mix_ffn_kernel(x_ref, o_ref, u_ref, halo_ref, wp_ref, sc_ref, wo_ref, g2_ref, wg_ref, wu_ref, wd_ref,
                    *rest, tm, final):
    if final:
        fn_ref, y_ref, mix_ref, h_ref, acc_ref = rest
    else:
        y_ref, mix_ref, h_ref, acc_ref = rest
    ti = pl.program_id(1)
    mix_ref[:, 0:NSA_WIDTH] = o_ref[...]
    halo = jnp.where(ti > 0, halo_ref[...], 0.0)
    pos = ti * tm + lax.broadcasted_iota(jnp.int32, (tm, 1), 0)
    for gi, w in enumerate(POOL_WIDTHS):
        lanes = slice(gi * POOL_GROUP, (gi + 1) * POOL_GROUP)
        u = u_ref[:, lanes]
        ext = jnp.concatenate([halo[:, lanes], u], axis=0)
        run = ext
        step = 1
        while step < w:
            run = run + pltpu.roll(run, step, axis=0)
            step *= 2
        cnt = jnp.minimum(pos + 1, w).astype(F32)
        y = run[POOL_HALO:, :] / cnt - u
        yo = _dot(y.astype(BF16), wp_ref[gi]) * sc_ref[:, lanes]
        mix_ref[:, NSA_WIDTH + gi * POOL_GROUP:NSA_WIDTH + (gi + 1) * POOL_GROUP] = yo.astype(BF16)
    x2 = x_ref[...] + _dot(mix_ref[...], wo_ref[...])
    y = _swiglu_residual(x2, g2_ref, wg_ref, wu_ref, wd_ref, h_ref, acc_ref)
    if final:
        y = _rms(y, fn_ref[...])
    y_ref[...] = y


def _mix_ffn(x3, layer, o_nsa, u, wp, sc, wo, g2, wg, wu, wd, final_g=None, tm=512):
    b, t, d = x3.shape
    per = tm // POOL_HALO
    final = final_g is not None
    tail_specs = [_resident((1, d))] if final else []
    tail_args = [final_g] if final else []
    return pl.pallas_call(
        functools.partial(_mix_ffn_kernel, tm=tm, final=final),
        grid=(b, t // tm),
        in_specs=[pl.BlockSpec((None, tm, d), lambda i, j: (i, j, 0)),
                  pl.BlockSpec((None, tm, NSA_WIDTH), lambda i, j: (i, j, 0)),
                  pl.BlockSpec((None, tm, POOL_CH), lambda i, j: (i, j, 0)),
                  pl.BlockSpec((None, POOL_HALO, POOL_CH), lambda i, j: (i, jnp.maximum(j * per - 1, 0), 0)),
                  _resident(wp.shape[1:], layer), _resident(sc.shape), _resident(wo.shape[1:], layer),
                  _resident((1, d)), _resident(wg.shape[1:], layer), _resident(wu.shape[1:], layer),
                  _resident(wd.shape[1:], layer)] + tail_specs,
        out_specs=pl.BlockSpec((None, tm, d), lambda i, j: (i, j, 0)),
        out_shape=jax.ShapeDtypeStruct((b, t, d), F32),
        scratch_shapes=[pltpu.VMEM((tm, NSA_WIDTH + POOL_CH), BF16),
                        pltpu.VMEM((tm, d), BF16), pltpu.VMEM((tm, d), F32)],
        compiler_params=pltpu.CompilerParams(
            dimension_semantics=("parallel", "parallel"), vmem_limit_bytes=VMEM_LIMIT),
        name="mix_ffn_final" if final else "mix_ffn",
    )(x3, o_nsa, u, u, wp, sc, wo, g2, wg, wu, wd, *tail_args)


def kernel(x, ffn1_norm, ffn1_wg, ffn1_wu, ffn1_wd, mix_norm, w_in, cmp_pe_k, cmp_wk1, cmp_wk2,
           cmp_pe_v, cmp_wv1, cmp_wv2, pool_w, pool_scale, w_out, ffn2_norm, ffn2_wg, ffn2_wu,
           ffn2_wd, final_norm):
    b, t, d = x.shape
    depth = ffn1_norm.shape[0]
    tables = _nsa_tables(t)
    f1g, f1u, f1d, f2g, f2u, f2d, wp, wo = (
        a.astype(BF16) for a in (ffn1_wg, ffn1_wu, ffn1_wd, ffn2_wg, ffn2_wu, ffn2_wd, pool_w, w_out))
    for l in range(depth):
        x, kcin, vcin, ks, kw, u, qt, vst, vwt, gt = _ffn_proj(
            x, l, ffn1_norm[l][None], f1g, f1u, f1d, mix_norm[l][None], *_split_w_in(w_in[l]))
        kc, vct = _compress(kcin, vcin,
                            *_compress_weights(cmp_pe_k[l], cmp_wk1[l], cmp_wk2[l], False),
                            *_compress_weights(cmp_pe_v[l], cmp_wv1[l], cmp_wv2[l], True))
        o_nsa = _nsa(qt, gt, ks, kw, vst, vwt, kc, vct, tables)
        x = _mix_ffn(x, l, o_nsa, u, wp, pool_scale[l][None], wo, ffn2_norm[l][None], f2g, f2u, f2d,
                     final_g=final_norm[None] if l == depth - 1 else None)
    return x
```
